```python
import math
import jax
import jax.numpy as jnp
from jax import lax
import numpy as np

D_MODEL = 1024
BATCH = 2
SEQ = 16384
DEPTH = 1
DEC_BATCH = 128
DEC_SEQ = 8
PAST_LEN = 8192
PAGE_SIZE = 128

D_MIX = D_MODEL
NSA_HEADS = 8
NSA_KV_HEADS = 2
NSA_GROUP = NSA_HEADS // NSA_KV_HEADS
HEAD_DIM = D_MIX // 2 // NSA_HEADS
CMP_STRIDE = 16
CMP_BLOCK = 2 * CMP_STRIDE
SEL_BLOCK = 64
SEL_TOPK = 16
WINDOW = 512
Q_BLOCK = 128
GLA_HEADS = 4
GLA_DV = D_MIX // 2 // GLA_HEADS
GLA_DK = GLA_DV // 2
GLA_GATE_RANK = 16
GLA_GATE_TEMP = 16.0
GLA_CHUNK = 64
PEER_HEADS = 8
PEER_NKEYS = 128
PEER_KEY_DIM = 256
PEER_TOPK = 16
N_EXPERTS = PEER_NKEYS * PEER_NKEYS
PEER_TOKEN_BLOCK = 128
KV_ROWS = 4
WIN_ROWS = 2
RMS_EPS = 1e-6
MASK_VALUE = -1e30
FORCE_SCORE = 1e4
IN_SIZES = (NSA_HEADS * HEAD_DIM, 6 * NSA_KV_HEADS * HEAD_DIM, 3 * NSA_HEADS,
            GLA_HEADS * GLA_DK, GLA_HEADS * GLA_DK, GLA_HEADS * GLA_DV,
            GLA_GATE_RANK, GLA_HEADS * GLA_DV)
D_IN = sum(IN_SIZES)

kernel_name = 'nsa_gla_peer_hybrid_step'


def rmsnorm(x, g):
    xf = x.astype(jnp.float32)
    y = xf * lax.rsqrt(jnp.mean(xf * xf, axis=-1, keepdims=True) + RMS_EPS)
    return (y * g.astype(jnp.float32)).astype(x.dtype)


def mixer_inputs(x, norm1_g, w_in, q_norm_g, k_norm_g):
    b, t = x.shape[:2]
    p = rmsnorm(x, norm1_g) @ w_in
    q, kv, gate, gq, gk, gv, glr, gog = jnp.split(p, np.cumsum(IN_SIZES)[:-1].tolist(), axis=-1)
    q = rmsnorm(q.reshape(b, t, NSA_KV_HEADS, NSA_GROUP, HEAD_DIM), q_norm_g) * (HEAD_DIM ** -0.5)
    kv = kv.reshape(b, t, 6, NSA_KV_HEADS, HEAD_DIM)
    kv_rows = jnp.stack([kv[:, :, 0], kv[:, :, 1], rmsnorm(kv[:, :, 2], k_norm_g[1]), kv[:, :, 3]], axis=2)
    win_rows = jnp.stack([rmsnorm(kv[:, :, 4], k_norm_g[2]), kv[:, :, 5]], axis=2)
    gate = jax.nn.sigmoid(gate.astype(jnp.float32)).reshape(b, t, NSA_KV_HEADS, NSA_GROUP, 3)
    return q, kv_rows, win_rows, gate, (gq, gk, gv, glr, gog)


def compress(rows, pos, w):
    b, l = rows.shape[:2]
    nc = (l - CMP_BLOCK) // CMP_STRIDE + 1
    chunks = rows[:, :(nc + 1) * CMP_STRIDE].reshape(b, nc + 1, CMP_STRIDE, NSA_KV_HEADS, HEAD_DIM)
    first = jnp.einsum('bnphd,pde->bnhe', chunks, w[:CMP_STRIDE])
    second = jnp.einsum('bnphd,pde->bnhe', chunks, w[CMP_STRIDE:])
    bias = jnp.einsum('pd,pde->e', pos, w)
    return first[:, :-1] + second[:, 1:] + bias


def nsa_keys(kv_all, cmp_pos, w_cmp, k_norm_g):
    kc = rmsnorm(compress(kv_all[:, :, 0], cmp_pos[0], w_cmp[0]), k_norm_g[0])
    vc = compress(kv_all[:, :, 1], cmp_pos[1], w_cmp[1])
    return kc, vc, kv_all[:, :, 2], kv_all[:, :, 3]


def nsa_block(q, t_pos, kc, vc, ks, vs, kw, vw, w_pos, gate):
    b, nq = q.shape[:2]
    nc = kc.shape[1]
    l = ks.shape[1]
    ns = -(-l // SEL_BLOCK)
    f32 = jnp.float32
    cmp_end = jnp.arange(nc) * CMP_STRIDE + (CMP_BLOCK - 1)
    m_c = cmp_end[None, :] <= t_pos[:, None]
    s_c = jnp.einsum('bqhgd,bnhd->bhgqn', q, kc).astype(f32)
    p_c = jax.nn.softmax(jnp.where(m_c, s_c, MASK_VALUE), axis=-1) * m_c
    o_c = jnp.einsum('bhgqn,bnhd->bqhgd', p_c.astype(vc.dtype), vc)
    ratio = SEL_BLOCK // CMP_STRIDE
    imp = jnp.pad(p_c.sum(axis=2), ((0, 0), (0, 0), (0, 0), (0, ns * ratio - nc)))
    imp = imp.reshape(b, NSA_KV_HEADS, nq, ns, ratio)
    imp = imp.sum(-1) + jnp.pad(imp[..., :-1, ratio - 1], ((0, 0), (0, 0), (0, 0), (1, 0)))
    blk = jnp.arange(ns)
    visible = blk[None, :] * SEL_BLOCK <= t_pos[:, None]
    forced = (blk[None, :] == 0) | (blk[None, :] == t_pos[:, None] // SEL_BLOCK)
    score = jnp.where(forced, FORCE_SCORE, jnp.where(visible, imp, -1.0))
    _, idx = lax.top_k(score, min(SEL_TOPK, ns))
    tok = idx[..., None] * SEL_BLOCK + jnp.arange(SEL_BLOCK)
    bi = jnp.arange(b)[:, None, None, None, None]
    hi = jnp.arange(NSA_KV_HEADS)[None, :, None, None, None]
    safe_tok = jnp.minimum(tok, l - 1)
    ks_g = ks[bi, safe_tok, hi]
    vs_g = vs[bi, safe_tok, hi]
    m_s = (tok <= t_pos[None, None, :, None, None])[:, :, None]
    s_s = jnp.einsum('bqhgd,bhqkpd->bhgqkp', q, ks_g).astype(f32)
    s_s = jnp.where(m_s, s_s, MASK_VALUE)
    p_s = jax.nn.softmax(s_s.reshape(s_s.shape[:4] + (-1,)), axis=-1).reshape(s_s.shape)
    o_s = jnp.einsum('bhgqkp,bhqkpd->bqhgd', p_s.astype(vs.dtype), vs_g)
    d = t_pos[:, None] - w_pos[None, :]
    m_w = (d >= 0) & (d <= WINDOW) & (w_pos[None, :] >= 0)
    s_w = jnp.einsum('bqhgd,bkhd->bhgqk', q, kw).astype(f32)
    p_w = jax.nn.softmax(jnp.where(m_w, s_w, MASK_VALUE), axis=-1)
    o_w = jnp.einsum('bhgqk,bkhd->bqhgd', p_w.astype(vw.dtype), vw)
    out = gate[..., 0:1] * o_c + gate[..., 1:2] * o_s + gate[..., 2:3] * o_w
    return out.reshape(b, nq, NSA_HEADS * HEAD_DIM).astype(q.dtype)


def nsa_prompt(q, kv_rows, win_rows, gate, cmp_pos, w_cmp, k_norm_g):
    b, t = q.shape[:2]
    kc, vc, ks, vs = nsa_keys(kv_rows, cmp_pos, w_cmp, k_norm_g)
    win = jnp.pad(win_rows, ((0, 0), (WINDOW, 0), (0, 0), (0, 0), (0, 0)))

    def one_block(blk):
        s = blk * Q_BLOCK
        qb = lax.dynamic_slice_in_dim(q, s, Q_BLOCK, axis=1)
        gb = lax.dynamic_slice_in_dim(gate, s, Q_BLOCK, axis=1)
        wb = lax.dynamic_slice_in_dim(win, s, WINDOW + Q_BLOCK, axis=1)
        t_pos = s + jnp.arange(Q_BLOCK)
        w_pos = s - WINDOW + jnp.arange(WINDOW + Q_BLOCK)
        return nsa_block(qb, t_pos, kc, vc, ks, vs, wb[:, :, 0], wb[:, :, 1], w_pos, gb)

    out = lax.map(one_block, jnp.arange(t // Q_BLOCK))
    return out.transpose(1, 0, 2, 3).reshape(b, t, -1)


def nsa_sample(q, kv_rows, win_rows, gate, cache_kv, layer, cache_win, page_table, cmp_pos, w_cmp, k_norm_g):
    bd, t = q.shape[:2]
    past_len = page_table.shape[1] * PAGE_SIZE
    past = cache_kv[layer, page_table].reshape(bd, past_len, KV_ROWS, NSA_KV_HEADS, HEAD_DIM)
    kc, vc, ks, vs = nsa_keys(jnp.concatenate([past, kv_rows], axis=1), cmp_pos, w_cmp, k_norm_g)
    wbuf = cache_win.shape[1]
    win_all = jnp.concatenate([cache_win, win_rows], axis=1)
    t_pos = past_len + jnp.arange(t)
    w_pos = past_len - wbuf + jnp.arange(wbuf + t)
    out = nsa_block(q, t_pos, kc, vc, ks, vs, win_all[:, :, 0], win_all[:, :, 1], w_pos, gate)
    return out, win_all[:, -min(WINDOW, past_len + t):]


def gla_chunked(q, k, v, log_a, s0):
    b, t, h = q.shape[:3]
    c = math.gcd(t, GLA_CHUNK)
    n = t // c

    def to_chunks(a):
        return a.reshape(b, n, c, h, a.shape[-1]).transpose(1, 0, 3, 2, 4)

    causal = jnp.tril(jnp.ones((c, c), dtype=bool))

    def step(S, inp):
        qc, kc, vc, ac = inp
        cum = jnp.cumsum(ac, axis=2)
        diff = jnp.minimum(cum[:, :, :, None] - cum[:, :, None, :], 0.0)
        decay = jnp.where(causal[..., None], jnp.exp(diff), 0.0)
        att = jnp.einsum('bhid,bhjd,bhijd->bhij', qc, kc, decay)
        o = jnp.einsum('bhij,bhjv->bhiv', att, vc) + jnp.einsum('bhid,bhdv->bhiv', qc * jnp.exp(cum), S)
        last = cum[:, :, -1:]
        S = jnp.exp(last)[:, :, 0, :, None] * S + jnp.einsum('bhjd,bhjv->bhdv', kc * jnp.exp(last - cum), vc)
        return S, o

    S, o = lax.scan(step, s0, (to_chunks(q), to_chunks(k), to_chunks(v), to_chunks(log_a)))
    return o.transpose(1, 0, 3, 2, 4).reshape(b, t, h, -1), S


def gla_mixer(gq, gk, gv, glr, gog, s0, w_gate, b_gate, norm_g):
    b, t = gq.shape[:2]
    f32 = jnp.float32
    q = gq.reshape(b, t, GLA_HEADS, GLA_DK).astype(f32) * (GLA_DK ** -0.5)
    k = gk.reshape(b, t, GLA_HEADS, GLA_DK).astype(f32)
    v = gv.reshape(b, t, GLA_HEADS, GLA_DV).astype(f32)
    log_a = jax.nn.log_sigmoid((glr @ w_gate + b_gate).astype(f32)).reshape(b, t, GLA_HEADS, GLA_DK) / GLA_GATE_TEMP
    o, S = gla_chunked(q, k, v, log_a, s0.astype(f32))
    o = rmsnorm(o, norm_g) * jax.nn.silu(gog.astype(f32)).reshape(b, t, GLA_HEADS, GLA_DV)
    return o.reshape(b, t, -1).astype(gq.dtype), S


def peer_ffn(h, w_query, subkeys, u, v):
    b, t, d = h.shape
    n = b * t
    nb = -(-n // PEER_TOKEN_BLOCK)
    flat = jnp.pad(h.reshape(n, d), ((0, nb * PEER_TOKEN_BLOCK - n), (0, 0))).reshape(nb, PEER_TOKEN_BLOCK, d)

    def one_block(xb):
        qh = (xb @ w_query).reshape(-1, PEER_HEADS, 2, PEER_KEY_DIM // 2)
        s = jnp.einsum('nhcd,hckd->nhck', qh, subkeys).astype(jnp.float32)
        s1, i1 = lax.top_k(s[:, :, 0], PEER_TOPK)
        s2, i2 = lax.top_k(s[:, :, 1], PEER_TOPK)
        cand = (s1[..., :, None] + s2[..., None, :]).reshape(s1.shape[:-1] + (-1,))
        cidx = (i1[..., :, None] * PEER_NKEYS + i2[..., None, :]).reshape(i1.shape[:-1] + (-1,))
        top, pos = lax.top_k(cand, PEER_TOPK)
        eidx = jnp.take_along_axis(cidx, pos, axis=-1)
        g = jax.nn.softmax(top, axis=-1)
        act = jax.nn.gelu(jnp.einsum('nhkd,nd->nhk', jnp.take(u, eidx, axis=0), xb).astype(jnp.float32))
        return jnp.einsum('nhk,nhkd->nd', (g * act).astype(xb.dtype), jnp.take(v, eidx, axis=0))

    out = lax.map(one_block, flat)
    return out.reshape(-1, d)[:n].reshape(b, t, d)


def residual_tail(x, mix, w_out, norm2_g, w_query, subkeys, u, v):
    x = x + (mix @ w_out).astype(x.dtype)
    return x + peer_ffn(rmsnorm(x, norm2_g), w_query, subkeys, u, v).astype(x.dtype)


def setup_inputs(seed: int = 0) -> dict:
    key = jax.random.key(seed)
    ks = jax.random.split(key, 24)
    f32 = jnp.float32
    n_pages = PAST_LEN // PAGE_SIZE
    n_used = DEC_BATCH * n_pages
    n_pool = n_used + max(1, n_used // 4)
    wbuf = min(WINDOW, PAST_LEN)

    def nrm(k, shape, scale):
        return jax.random.normal(k, shape, f32) * scale

    page_table = jax.random.permutation(ks[0], n_pool)[:n_used].reshape(DEC_BATCH, n_pages).astype(jnp.int32)
    return {
        'x_prompt': nrm(ks[1], (BATCH, SEQ, D_MODEL), 1.0),
        'x_sample': nrm(ks[2], (DEC_BATCH, DEC_SEQ, D_MODEL), 1.0),
        'cache_kv': nrm(ks[3], (DEPTH, n_pool, PAGE_SIZE, KV_ROWS, NSA_KV_HEADS, HEAD_DIM), 1.0),
        'cache_win': nrm(ks[4], (DEPTH, DEC_BATCH, wbuf, WIN_ROWS, NSA_KV_HEADS, HEAD_DIM), 1.0),
        'state_gla': nrm(ks[5], (DEPTH, DEC_BATCH, GLA_HEADS, GLA_DK, GLA_DV), 0.5),
        'page_table': page_table,
        'norm1_g': 1.0 + nrm(ks[6], (DEPTH, D_MODEL), 0.02),
        'w_in': nrm(ks[7], (DEPTH, D_MODEL, D_IN), D_MODEL ** -0.5),
        'q_norm_g': 1.0 + nrm(ks[8], (DEPTH, HEAD_DIM), 0.02),
        'k_norm_g': 1.0 + nrm(ks[9], (DEPTH, 3, HEAD_DIM), 0.02),
        'cmp_pos': nrm(ks[10], (DEPTH, 2, CMP_BLOCK, HEAD_DIM), 0.5),
        'w_cmp': nrm(ks[11], (DEPTH, 2, CMP_BLOCK, HEAD_DIM, HEAD_DIM), (CMP_BLOCK * HEAD_DIM) ** -0.5),
        'gla_w_gate': nrm(ks[12], (DEPTH, GLA_GATE_RANK, GLA_HEADS * GLA_DK), GLA_GATE_RANK ** -0.5),
        'gla_b_gate': nrm(ks[13], (DEPTH, GLA_HEADS * GLA_DK), 0.1),
        'gla_norm_g': 1.0 + nrm(ks[14], (DEPTH, GLA_DV), 0.02),
        'w_out': nrm(ks[15], (DEPTH, D_MIX, D_MODEL), D_MIX ** -0.5),
        'norm2_g': 1.0 + nrm(ks[16], (DEPTH, D_MODEL), 0.02),
        'peer_w_query': nrm(ks[17], (DEPTH, D_MODEL, PEER_HEADS * PEER_KEY_DIM), D_MODEL ** -0.5),
        'peer_subkeys': nrm(ks[18], (DEPTH, PEER_HEADS, 2, PEER_NKEYS, PEER_KEY_DIM // 2), (PEER_KEY_DIM // 2) ** -0.5),
        'peer_u': nrm(ks[19], (DEPTH, N_EXPERTS, D_MODEL), D_MODEL ** -0.5),
        'peer_v': nrm(ks[20], (DEPTH, N_EXPERTS, D_MODEL), PEER_HEADS ** -0.5),
    }


def reference(x_prompt, x_sample, cache_kv, cache_win, state_gla, page_table, norm1_g, w_in, q_norm_g,
              k_norm_g, cmp_pos, w_cmp, gla_w_gate, gla_b_gate, gla_norm_g, w_out, norm2_g,
              peer_w_query, peer_subkeys, peer_u, peer_v):
    xp, xs = x_prompt, x_sample
    kv_p, win_p, gla_p, kv_s, win_s, gla_s = [], [], [], [], [], []
    for l in range(DEPTH):
        q, kv_rows, win_rows, gate, gla_in = mixer_inputs(xp, norm1_g[l], w_in[l], q_norm_g[l], k_norm_g[l])
        a = nsa_prompt(q, kv_rows, win_rows, gate, cmp_pos[l], w_cmp[l], k_norm_g[l])
        s0 = jnp.zeros((xp.shape[0], GLA_HEADS, GLA_DK, GLA_DV), jnp.float32)
        g, s_new = gla_mixer(*gla_in, s0, gla_w_gate[l], gla_b_gate[l], gla_norm_g[l])
        kv_p.append(kv_rows)
        win_p.append(win_rows[:, -min(WINDOW, xp.shape[1]):])
        gla_p.append(s_new.astype(state_gla.dtype))
        xp = residual_tail(xp, jnp.concatenate([a, g], axis=-1), w_out[l], norm2_g[l],
                           peer_w_query[l], peer_subkeys[l], peer_u[l], peer_v[l])
        q, kv_rows, win_rows, gate, gla_in = mixer_inputs(xs, norm1_g[l], w_in[l], q_norm_g[l], k_norm_g[l])
        a, win_new = nsa_sample(q, kv_rows, win_rows, gate, cache_kv, l, cache_win[l], page_table,
                                cmp_pos[l], w_cmp[l], k_norm_g[l])
        g, s_new = gla_mixer(*gla_in, state_gla[l], gla_w_gate[l], gla_b_gate[l], gla_norm_g[l])
        kv_s.append(kv_rows)
        win_s.append(win_new)
        gla_s.append(s_new.astype(state_gla.dtype))
        xs = residual_tail(xs, jnp.concatenate([a, g], axis=-1), w_out[l], norm2_g[l],
                           peer_w_query[l], peer_subkeys[l], peer_u[l], peer_v[l])
    return (xp, xs, jnp.stack(kv_p), jnp.stack(win_p), jnp.stack(gla_p), jnp.stack(kv_s), jnp.stack(win_s), jnp.stack(gla_s))
```

```python
import functools
import math

import jax
import jax.numpy as jnp
import numpy as np
from jax import lax
from jax.experimental import pallas as pl
from jax.experimental.pallas import tpu as pltpu

D_MODEL = 1024
NSA_HEADS = 8
NSA_KV_HEADS = 2
NSA_GROUP = NSA_HEADS // NSA_KV_HEADS
HEAD_DIM = 64
CMP_STRIDE = 16
CMP_BLOCK = 32
SEL_BLOCK = 64
SEL_TOPK = 16
WINDOW = 512
Q_BLOCK = 128
PAGE_SIZE = 128
GLA_HEADS = 4
GLA_DV = 128
GLA_DK = 64
GLA_GATE_RANK = 16
GLA_GATE_TEMP = 16.0
GLA_CHUNK = 64
PEER_HEADS = 8
PEER_NKEYS = 128
PEER_KEY_DIM = 256
PEER_TOPK = 16
PEER_TOKEN_BLOCK = 128
KV_ROWS = 4
RMS_EPS = 1e-6
MASK_VALUE = -1e30
FORCE_SCORE = 1e4

Q_W = NSA_HEADS * HEAD_DIM
KV_W = 6 * NSA_KV_HEADS * HEAD_DIM
GATE_W = 3 * NSA_HEADS
GQ_W = GLA_HEADS * GLA_DK
GV_W = GLA_HEADS * GLA_DV
MISC_W = 128
IN_SIZES = (Q_W, KV_W, GATE_W, GQ_W, GQ_W, GV_W, GLA_GATE_RANK, GV_W)
P_W = Q_W + KV_W + GQ_W + GQ_W + GV_W + GV_W + MISC_W

V7X_VMEM_LIMIT = 56 * 1024 * 1024


def _rmsnorm(x, g):
    xf = x.astype(jnp.float32)
    y = xf * lax.rsqrt(jnp.mean(xf * xf, axis=-1, keepdims=True) + RMS_EPS)
    return (y * g.astype(jnp.float32)).astype(x.dtype)


def _head_group_ones(width, group, dtype):
    r = lax.broadcasted_iota(jnp.int32, (width, width), 0) // group
    c = lax.broadcasted_iota(jnp.int32, (width, width), 1) // group
    return jnp.where(r == c, 1.0, 0.0).astype(dtype)


def _group_mean_sq(x, group):
    sq = x * x
    hi = sq.astype(jnp.bfloat16)
    lo = (sq - hi.astype(jnp.float32)).astype(jnp.bfloat16)
    ones = _head_group_ones(x.shape[-1], group, jnp.bfloat16)
    s = jnp.dot(hi, ones, preferred_element_type=jnp.float32)
    s = s + jnp.dot(lo, ones, preferred_element_type=jnp.float32)
    return s * (1.0 / group)


def _inproj_kernel(x_ref, g1_ref, w_ref, qg_ref, ksg_ref, kwg_ref,
                   q_ref, kv_ref, win_ref, misc_ref, gq_ref, gk_ref, gv_ref, gog_ref):
    x = x_ref[...]
    h = x * lax.rsqrt(jnp.mean(x * x, axis=-1, keepdims=True) + RMS_EPS) * g1_ref[...]
    p = jnp.dot(h.astype(jnp.bfloat16), w_ref[...], preferred_element_type=jnp.float32)
    o = 0
    q = p[:, o:o + Q_W]; o += Q_W
    kv = p[:, o:o + KV_W]; o += KV_W
    gq_ref[...] = p[:, o:o + GQ_W]; o += GQ_W
    gk_ref[...] = p[:, o:o + GQ_W]; o += GQ_W
    gv_ref[...] = p[:, o:o + GV_W]; o += GV_W
    gog_ref[...] = p[:, o:o + GV_W]; o += GV_W
    misc_ref[...] = p[:, o:o + MISC_W]
    qn = q * lax.rsqrt(_group_mean_sq(q, HEAD_DIM) + RMS_EPS) * qg_ref[...] * (HEAD_DIM ** -0.5)
    q_ref[...] = qn
    hw = NSA_KV_HEADS * HEAD_DIM
    k_sel = kv[:, 2 * hw:3 * hw]
    k_sel = k_sel * lax.rsqrt(_group_mean_sq(k_sel, HEAD_DIM) + RMS_EPS) * ksg_ref[...]
    k_win = kv[:, 4 * hw:5 * hw]
    k_win = k_win * lax.rsqrt(_group_mean_sq(k_win, HEAD_DIM) + RMS_EPS) * kwg_ref[...]
    kv_ref[:, 0:2 * hw] = kv[:, 0:2 * hw]
    kv_ref[:, 2 * hw:3 * hw] = k_sel
    kv_ref[:, 3 * hw:4 * hw] = kv[:, 3 * hw:4 * hw]
    win_ref[:, 0:hw] = k_win
    win_ref[:, hw:2 * hw] = kv[:, 5 * hw:6 * hw]


def _reorder_w_in(w_in):
    offs = np.cumsum((0,) + IN_SIZES)
    q, kv, gate, gq, gk, gv, glr, gog = [w_in[:, offs[i]:offs[i + 1]] for i in range(8)]
    pad = jnp.zeros((w_in.shape[0], MISC_W - GATE_W - GLA_GATE_RANK), w_in.dtype)
    return jnp.concatenate([q, kv, gq, gk, gv, gog, gate, glr, pad], axis=1).astype(jnp.bfloat16)


def _inproj(x2d, norm1_g, w_in_r, q_norm_g, k_norm_g, *, tm):
    n = x2d.shape[0]
    assert n % tm == 0
    hw = NSA_KV_HEADS * HEAD_DIM
    f32 = jnp.float32
    row = lambda i: (i, 0)
    const = lambda i: (0, 0)
    widths = (Q_W, 4 * hw, 2 * hw, MISC_W, GQ_W, GQ_W, GV_W, GV_W)
    return pl.pallas_call(
        _inproj_kernel,
        grid=(n // tm,),
        in_specs=[
            pl.BlockSpec((tm, D_MODEL), row),
            pl.BlockSpec((1, D_MODEL), const),
            pl.BlockSpec((D_MODEL, P_W), const),
            pl.BlockSpec((1, Q_W), const),
            pl.BlockSpec((1, hw), const),
            pl.BlockSpec((1, hw), const),
        ],
        out_specs=[pl.BlockSpec((tm, w), row) for w in widths],
        out_shape=[jax.ShapeDtypeStruct((n, w), f32) for w in widths],
        compiler_params=pltpu.CompilerParams(
            dimension_semantics=("arbitrary",), vmem_limit_bytes=V7X_VMEM_LIMIT),
        name="inproj",
    )(x2d, norm1_g.reshape(1, -1), w_in_r,
      jnp.tile(q_norm_g, NSA_HEADS).reshape(1, -1),
      jnp.tile(k_norm_g[1], NSA_KV_HEADS).reshape(1, -1),
      jnp.tile(k_norm_g[2], NSA_KV_HEADS).reshape(1, -1))


def _compress(rows, pos, w):
    b, l = rows.shape[:2]
    nc = (l - CMP_BLOCK) // CMP_STRIDE + 1
    chunks = rows[:, :(nc + 1) * CMP_STRIDE].reshape(b, nc + 1, CMP_STRIDE, NSA_KV_HEADS, HEAD_DIM)
    first = jnp.einsum('bnphd,pde->bnhe', chunks, w[:CMP_STRIDE])
    second = jnp.einsum('bnphd,pde->bnhe', chunks, w[CMP_STRIDE:])
    bias = jnp.einsum('pd,pde->e', pos, w)
    return first[:, :-1] + second[:, 1:] + bias


def _nsa_keys(kv_all, cmp_pos, w_cmp, k_norm_g):
    kc = _rmsnorm(_compress(kv_all[:, :, 0], cmp_pos[0], w_cmp[0]), k_norm_g[0])
    vc = _compress(kv_all[:, :, 1], cmp_pos[1], w_cmp[1])
    return kc, vc, kv_all[:, :, 2], kv_all[:, :, 3]


def _nsa_block(q, t_pos, kc, vc, ks, vs, kw, vw, w_pos, gate):
    b, nq = q.shape[:2]
    nc = kc.shape[1]
    l = ks.shape[1]
    ns = -(-l // SEL_BLOCK)
    f32 = jnp.float32
    cmp_end = jnp.arange(nc) * CMP_STRIDE + (CMP_BLOCK - 1)
    m_c = cmp_end[None, :] <= t_pos[:, None]
    s_c = jnp.einsum('bqhgd,bnhd->bhgqn', q, kc).astype(f32)
    p_c = jax.nn.softmax(jnp.where(m_c, s_c, MASK_VALUE), axis=-1) * m_c
    o_c = jnp.einsum('bhgqn,bnhd->bqhgd', p_c.astype(vc.dtype), vc)
    ratio = SEL_BLOCK // CMP_STRIDE
    imp = jnp.pad(p_c.sum(axis=2), ((0, 0), (0, 0), (0, 0), (0, ns * ratio - nc)))
    imp = imp.reshape(b, NSA_KV_HEADS, nq, ns, ratio)
    imp = imp.sum(-1) + jnp.pad(imp[..., :-1, ratio - 1], ((0, 0), (0, 0), (0, 0), (1, 0)))
    blk = jnp.arange(ns)
    visible = blk[None, :] * SEL_BLOCK <= t_pos[:, None]
    forced = (blk[None, :] == 0) | (blk[None, :] == t_pos[:, None] // SEL_BLOCK)
    score = jnp.where(forced, FORCE_SCORE, jnp.where(visible, imp, -1.0))
    _, idx = lax.top_k(score, min(SEL_TOPK, ns))
    tok = idx[..., None] * SEL_BLOCK + jnp.arange(SEL_BLOCK)
    bi = jnp.arange(b)[:, None, None, None, None]
    hi = jnp.arange(NSA_KV_HEADS)[None, :, None, None, None]
    safe_tok = jnp.minimum(tok, l - 1)
    ks_g = ks[bi, safe_tok, hi]
    vs_g = vs[bi, safe_tok, hi]
    m_s = (tok <= t_pos[None, None, :, None, None])[:, :, None]
    s_s = jnp.einsum('bqhgd,bhqkpd->bhgqkp', q, ks_g).astype(f32)
    s_s = jnp.where(m_s, s_s, MASK_VALUE)
    p_s = jax.nn.softmax(s_s.reshape(s_s.shape[:4] + (-1,)), axis=-1).reshape(s_s.shape)
    o_s = jnp.einsum('bhgqkp,bhqkpd->bqhgd', p_s.astype(vs.dtype), vs_g)
    d = t_pos[:, None] - w_pos[None, :]
    m_w = (d >= 0) & (d <= WINDOW) & (w_pos[None, :] >= 0)
    s_w = jnp.einsum('bqhgd,bkhd->bhgqk', q, kw).astype(f32)
    p_w = jax.nn.softmax(jnp.where(m_w, s_w, MASK_VALUE), axis=-1)
    o_w = jnp.einsum('bhgqk,bkhd->bqhgd', p_w.astype(vw.dtype), vw)
    out = gate[..., 0:1] * o_c + gate[..., 1:2] * o_s + gate[..., 2:3] * o_w
    return out.reshape(b, nq, NSA_HEADS * HEAD_DIM).astype(q.dtype)


def _nsa_prompt(q, kv_rows, win_rows, gate, cmp_pos, w_cmp, k_norm_g):
    b, t = q.shape[:2]
    kc, vc, ks, vs = _nsa_keys(kv_rows, cmp_pos, w_cmp, k_norm_g)
    win = jnp.pad(win_rows, ((0, 0), (WINDOW, 0), (0, 0), (0, 0), (0, 0)))

    def one_block(blk):
        s = blk * Q_BLOCK
        qb = lax.dynamic_slice_in_dim(q, s, Q_BLOCK, axis=1)
        gb = lax.dynamic_slice_in_dim(gate, s, Q_BLOCK, axis=1)
        wb = lax.dynamic_slice_in_dim(win, s, WINDOW + Q_BLOCK, axis=1)
        t_pos = s + jnp.arange(Q_BLOCK)
        w_pos = s - WINDOW + jnp.arange(WINDOW + Q_BLOCK)
        return _nsa_block(qb, t_pos, kc, vc, ks, vs, wb[:, :, 0], wb[:, :, 1], w_pos, gb)

    out = lax.map(one_block, jnp.arange(t // Q_BLOCK))
    return out.transpose(1, 0, 2, 3).reshape(b, t, -1)


def _nsa_sample(q, kv_rows, win_rows, gate, cache_kv_l, cache_win, page_table, cmp_pos, w_cmp, k_norm_g):
    bd, t = q.shape[:2]
    past_len = page_table.shape[1] * PAGE_SIZE
    past = cache_kv_l[page_table].reshape(bd, past_len, KV_ROWS, NSA_KV_HEADS, HEAD_DIM)
    kc, vc, ks, vs = _nsa_keys(jnp.concatenate([past, kv_rows], axis=1), cmp_pos, w_cmp, k_norm_g)
    wbuf = cache_win.shape[1]
    win_all = jnp.concatenate([cache_win, win_rows], axis=1)
    t_pos = past_len + jnp.arange(t)
    w_pos = past_len - wbuf + jnp.arange(wbuf + t)
    out = _nsa_block(q, t_pos, kc, vc, ks, vs, win_all[:, :, 0], win_all[:, :, 1], w_pos, gate)
    return out, win_all[:, -min(WINDOW, past_len + t):]


def _gla_chunked(q, k, v, log_a, s0):
    b, t, h = q.shape[:3]
    c = math.gcd(t, GLA_CHUNK)
    n = t // c

    def to_chunks(a):
        return a.reshape(b, n, c, h, a.shape[-1]).transpose(1, 0, 3, 2, 4)

    causal = jnp.tril(jnp.ones((c, c), dtype=bool))

    def step(S, inp):
        qc, kc, vc, ac = inp
        cum = jnp.cumsum(ac, axis=2)
        diff = jnp.minimum(cum[:, :, :, None] - cum[:, :, None, :], 0.0)
        decay = jnp.where(causal[..., None], jnp.exp(diff), 0.0)
        att = jnp.einsum('bhid,bhjd,bhijd->bhij', qc, kc, decay)
        o = jnp.einsum('bhij,bhjv->bhiv', att, vc) + jnp.einsum('bhid,bhdv->bhiv', qc * jnp.exp(cum), S)
        last = cum[:, :, -1:]
        S = jnp.exp(last)[:, :, 0, :, None] * S + jnp.einsum('bhjd,bhjv->bhdv', kc * jnp.exp(last - cum), vc)
        return S, o

    S, o = lax.scan(step, s0, (to_chunks(q), to_chunks(k), to_chunks(v), to_chunks(log_a)))
    return o.transpose(1, 0, 3, 2, 4).reshape(b, t, h, -1), S


def _gla_mixer(gq, gk, gv, glr, gog, s0, w_gate, b_gate, norm_g):
    b, t = gq.shape[:2]
    f32 = jnp.float32
    q = gq.reshape(b, t, GLA_HEADS, GLA_DK).astype(f32) * (GLA_DK ** -0.5)
    k = gk.reshape(b, t, GLA_HEADS, GLA_DK).astype(f32)
    v = gv.reshape(b, t, GLA_HEADS, GLA_DV).astype(f32)
    log_a = jax.nn.log_sigmoid((glr @ w_gate + b_gate).astype(f32)).reshape(b, t, GLA_HEADS, GLA_DK) / GLA_GATE_TEMP
    o, S = _gla_chunked(q, k, v, log_a, s0.astype(f32))
    o = _rmsnorm(o, norm_g) * jax.nn.silu(gog.astype(f32)).reshape(b, t, GLA_HEADS, GLA_DV)
    return o.reshape(b, t, -1).astype(gq.dtype), S


def _peer_ffn(h, w_query, subkeys, u, v):
    b, t, d = h.shape
    n = b * t
    nb = -(-n // PEER_TOKEN_BLOCK)
    flat = jnp.pad(h.reshape(n, d), ((0, nb * PEER_TOKEN_BLOCK - n), (0, 0))).reshape(nb, PEER_TOKEN_BLOCK, d)

    def one_block(xb):
        qh = (xb @ w_query).reshape(-1, PEER_HEADS, 2, PEER_KEY_DIM // 2)
        s = jnp.einsum('nhcd,hckd->nhck', qh, subkeys).astype(jnp.float32)
        s1, i1 = lax.top_k(s[:, :, 0], PEER_TOPK)
        s2, i2 = lax.top_k(s[:, :, 1], PEER_TOPK)
        cand = (s1[..., :, None] + s2[..., None, :]).reshape(s1.shape[:-1] + (-1,))
        cidx = (i1[..., :, None] * PEER_NKEYS + i2[..., None, :]).reshape(i1.shape[:-1] + (-1,))
        top, pos = lax.top_k(cand, PEER_TOPK)
        eidx = jnp.take_along_axis(cidx, pos, axis=-1)
        g = jax.nn.softmax(top, axis=-1)
        act = jax.nn.gelu(jnp.einsum('nhkd,nd->nhk', jnp.take(u, eidx, axis=0), xb).astype(jnp.float32))
        return jnp.einsum('nhk,nhkd->nd', (g * act).astype(xb.dtype), jnp.take(v, eidx, axis=0))

    out = lax.map(one_block, flat)
    return out.reshape(-1, d)[:n].reshape(b, t, d)


def _residual_tail(x, mix, w_out, norm2_g, w_query, subkeys, u, v):
    x = x + (mix @ w_out).astype(x.dtype)
    return x + _peer_ffn(_rmsnorm(x, norm2_g), w_query, subkeys, u, v).astype(x.dtype)


def _mixer_inputs(x, norm1_g, w_in_r, q_norm_g, k_norm_g, *, tm):
    b, t = x.shape[:2]
    q, kvr, winr, misc, gq, gk, gv, gog = _inproj(
        x.reshape(b * t, D_MODEL), norm1_g, w_in_r, q_norm_g, k_norm_g, tm=tm)
    q = q.reshape(b, t, NSA_KV_HEADS, NSA_GROUP, HEAD_DIM)
    kv_rows = kvr.reshape(b, t, 4, NSA_KV_HEADS, HEAD_DIM)
    win_rows = winr.reshape(b, t, 2, NSA_KV_HEADS, HEAD_DIM)
    gate = jax.nn.sigmoid(misc[:, :GATE_W]).reshape(b, t, NSA_KV_HEADS, NSA_GROUP, 3)
    glr = misc[:, GATE_W:GATE_W + GLA_GATE_RANK].reshape(b, t, -1)
    rs = lambda a: a.reshape(b, t, -1)
    return q, kv_rows, win_rows, gate, (rs(gq), rs(gk), rs(gv), glr, rs(gog))


def _token_tile(n):
    for tm in (512, 256, 128, 64, 32, 16, 8):
        if n % tm == 0:
            return tm
    raise ValueError(n)


def kernel(x_prompt, x_sample, cache_kv, cache_win, state_gla, page_table, norm1_g, w_in, q_norm_g, k_norm_g, cmp_pos, w_cmp, gla_w_gate, gla_b_gate, gla_norm_g, w_out, norm2_g, peer_w_query, peer_subkeys, peer_u, peer_v):
    depth = w_in.shape[0]
    xp, xs = x_prompt, x_sample
    kv_p, win_p, gla_p, kv_s, win_s, gla_s = [], [], [], [], [], []
    for l in range(depth):
        w_in_r = _reorder_w_in(w_in[l])
        tmp = _token_tile(xp.shape[0] * xp.shape[1])
        q, kv_rows, win_rows, gate, gla_in = _mixer_inputs(xp, norm1_g[l], w_in_r, q_norm_g[l], k_norm_g[l], tm=tmp)
        a = _nsa_prompt(q, kv_rows, win_rows, gate, cmp_pos[l], w_cmp[l], k_norm_g[l])
        s0 = jnp.zeros((xp.shape[0], GLA_HEADS, GLA_DK, GLA_DV), jnp.float32)
        g, s_new = _gla_mixer(*gla_in, s0, gla_w_gate[l], gla_b_gate[l], gla_norm_g[l])
        kv_p.append(kv_rows)
        win_p.append(win_rows[:, -min(WINDOW, xp.shape[1]):])
        gla_p.append(s_new.astype(state_gla.dtype))
        xp = _residual_tail(xp, jnp.concatenate([a, g], axis=-1), w_out[l], norm2_g[l],
                            peer_w_query[l], peer_subkeys[l], peer_u[l], peer_v[l])
        tms = _token_tile(xs.shape[0] * xs.shape[1])
        q, kv_rows, win_rows, gate, gla_in = _mixer_inputs(xs, norm1_g[l], w_in_r, q_norm_g[l], k_norm_g[l], tm=tms)
        a, win_new = _nsa_sample(q, kv_rows, win_rows, gate, cache_kv[l], cache_win[l], page_table,
                                 cmp_pos[l], w_cmp[l], k_norm_g[l])
        g, s_new = _gla_mixer(*gla_in, state_gla[l], gla_w_gate[l], gla_b_gate[l], gla_norm_g[l])
        kv_s.append(kv_rows)
        win_s.append(win_new)
        gla_s.append(s_new.astype(state_gla.dtype))
        xs = _residual_tail(xs, jnp.concatenate([a, g], axis=-1), w_out[l], norm2_g[l],
                            peer_w_query[l], peer_subkeys[l], peer_u[l], peer_v[l])
    return (xp, xs, jnp.stack(kv_p), jnp.stack(win_p), jnp.stack(gla_p),
            jnp.stack(kv_s), jnp.stack(win_s), jnp.stack(gla_s))
```

```python
import functools
import math

import jax
import jax.numpy as jnp
import numpy as np
from jax import lax
from jax.experimental import pallas as pl
from jax.experimental.pallas import tpu as pltpu

D_MODEL = 1024
NSA_HEADS = 8
NSA_KV_HEADS = 2
NSA_GROUP = NSA_HEADS // NSA_KV_HEADS
HEAD_DIM = 64
CMP_STRIDE = 16
CMP_BLOCK = 32
SEL_BLOCK = 64
SEL_TOPK = 16
WINDOW = 512
Q_BLOCK = 128
PAGE_SIZE = 128
GLA_HEADS = 4
GLA_DV = 128
GLA_DK = 64
GLA_GATE_RANK = 16
GLA_GATE_TEMP = 16.0
GLA_CHUNK = 64
PEER_HEADS = 8
PEER_NKEYS = 128
PEER_KEY_DIM = 256
PEER_TOPK = 16
PEER_TOKEN_BLOCK = 128
KV_ROWS = 4
RMS_EPS = 1e-6
MASK_VALUE = -1e30
FORCE_SCORE = 1e4

Q_W = NSA_HEADS * HEAD_DIM
KV_W = 6 * NSA_KV_HEADS * HEAD_DIM
GATE_W = 3 * NSA_HEADS
GQ_W = GLA_HEADS * GLA_DK
GV_W = GLA_HEADS * GLA_DV
MISC_W = 128
IN_SIZES = (Q_W, KV_W, GATE_W, GQ_W, GQ_W, GV_W, GLA_GATE_RANK, GV_W)
P_W = Q_W + KV_W + GQ_W + GQ_W + GV_W + GV_W + MISC_W

V7X_VMEM_LIMIT = 56 * 1024 * 1024


def _rmsnorm(x, g):
    xf = x.astype(jnp.float32)
    y = xf * lax.rsqrt(jnp.mean(xf * xf, axis=-1, keepdims=True) + RMS_EPS)
    return (y * g.astype(jnp.float32)).astype(x.dtype)


def _head_group_ones(width, group, dtype):
    r = lax.broadcasted_iota(jnp.int32, (width, width), 0) // group
    c = lax.broadcasted_iota(jnp.int32, (width, width), 1) // group
    return jnp.where(r == c, 1.0, 0.0).astype(dtype)


def _group_mean_sq(x, group):
    sq = x * x
    hi = sq.astype(jnp.bfloat16)
    lo = (sq - hi.astype(jnp.float32)).astype(jnp.bfloat16)
    ones = _head_group_ones(x.shape[-1], group, jnp.bfloat16)
    s = jnp.dot(hi, ones, preferred_element_type=jnp.float32)
    s = s + jnp.dot(lo, ones, preferred_element_type=jnp.float32)
    return s * (1.0 / group)


def _inproj_kernel(x_ref, g1_ref, w_ref, qg_ref, ksg_ref, kwg_ref,
                   q_ref, kv_ref, win_ref, misc_ref, gq_ref, gk_ref, gv_ref, gog_ref, kvb_ref):
    x = x_ref[...]
    h = x * lax.rsqrt(jnp.mean(x * x, axis=-1, keepdims=True) + RMS_EPS) * g1_ref[...]
    p = jnp.dot(h.astype(jnp.bfloat16), w_ref[...], preferred_element_type=jnp.float32)
    o = 0
    q = p[:, o:o + Q_W]; o += Q_W
    kv = p[:, o:o + KV_W]; o += KV_W
    gq_ref[...] = p[:, o:o + GQ_W]; o += GQ_W
    gk_ref[...] = p[:, o:o + GQ_W]; o += GQ_W
    gv_ref[...] = p[:, o:o + GV_W]; o += GV_W
    gog_ref[...] = p[:, o:o + GV_W]; o += GV_W
    misc_ref[...] = p[:, o:o + MISC_W]
    qn = q * lax.rsqrt(_group_mean_sq(q, HEAD_DIM) + RMS_EPS) * qg_ref[...] * (HEAD_DIM ** -0.5)
    q_ref[...] = qn.astype(q_ref.dtype)
    hw = NSA_KV_HEADS * HEAD_DIM
    k_sel = kv[:, 2 * hw:3 * hw]
    k_sel = k_sel * lax.rsqrt(_group_mean_sq(k_sel, HEAD_DIM) + RMS_EPS) * ksg_ref[...]
    k_win = kv[:, 4 * hw:5 * hw]
    k_win = k_win * lax.rsqrt(_group_mean_sq(k_win, HEAD_DIM) + RMS_EPS) * kwg_ref[...]
    kv_ref[:, 0:2 * hw] = kv[:, 0:2 * hw]
    kv_ref[:, 2 * hw:3 * hw] = k_sel
    kv_ref[:, 3 * hw:4 * hw] = kv[:, 3 * hw:4 * hw]
    win_ref[:, 0:hw] = k_win
    win_ref[:, hw:2 * hw] = kv[:, 5 * hw:6 * hw]
    kvb_ref[:, 0:hw] = k_sel.astype(kvb_ref.dtype)
    kvb_ref[:, hw:2 * hw] = kv[:, 3 * hw:4 * hw].astype(kvb_ref.dtype)
    kvb_ref[:, 2 * hw:3 * hw] = k_win.astype(kvb_ref.dtype)
    kvb_ref[:, 3 * hw:4 * hw] = kv[:, 5 * hw:6 * hw].astype(kvb_ref.dtype)


def _reorder_w_in(w_in):
    offs = np.cumsum((0,) + IN_SIZES)
    q, kv, gate, gq, gk, gv, glr, gog = [w_in[:, offs[i]:offs[i + 1]] for i in range(8)]
    pad = jnp.zeros((w_in.shape[0], MISC_W - GATE_W - GLA_GATE_RANK), w_in.dtype)
    return jnp.concatenate([q, kv, gq, gk, gv, gog, gate, glr, pad], axis=1).astype(jnp.bfloat16)


def _inproj(x2d, norm1_g, w_in_r, q_norm_g, k_norm_g, *, tm):
    n = x2d.shape[0]
    assert n % tm == 0
    hw = NSA_KV_HEADS * HEAD_DIM
    f32 = jnp.float32
    row = lambda i: (i, 0)
    const = lambda i: (0, 0)
    widths = (Q_W, 4 * hw, 2 * hw, MISC_W, GQ_W, GQ_W, GV_W, GV_W, 4 * hw)
    bf16 = jnp.bfloat16
    dtypes = (bf16, f32, f32, f32, f32, f32, f32, f32, bf16)
    return pl.pallas_call(
        _inproj_kernel,
        grid=(n // tm,),
        in_specs=[
            pl.BlockSpec((tm, D_MODEL), row),
            pl.BlockSpec((1, D_MODEL), const),
            pl.BlockSpec((D_MODEL, P_W), const),
            pl.BlockSpec((1, Q_W), const),
            pl.BlockSpec((1, hw), const),
            pl.BlockSpec((1, hw), const),
        ],
        out_specs=[pl.BlockSpec((tm, w), row) for w in widths],
        out_shape=[jax.ShapeDtypeStruct((n, w), dt) for w, dt in zip(widths, dtypes)],
        compiler_params=pltpu.CompilerParams(
            dimension_semantics=("arbitrary",), vmem_limit_bytes=V7X_VMEM_LIMIT),
        name="inproj",
    )(x2d, norm1_g.reshape(1, -1), w_in_r,
      jnp.tile(q_norm_g, NSA_HEADS).reshape(1, -1),
      jnp.tile(k_norm_g[1], NSA_KV_HEADS).reshape(1, -1),
      jnp.tile(k_norm_g[2], NSA_KV_HEADS).reshape(1, -1))


CMP_LANES = 2 * NSA_KV_HEADS * HEAD_DIM
ROW_LANES = KV_ROWS * NSA_KV_HEADS * HEAD_DIM


def _chunk_map(x, w_ref):
    acc = None
    for p in range(CMP_STRIDE):
        xp = x[:, p * ROW_LANES:p * ROW_LANES + CMP_LANES].astype(jnp.bfloat16)
        d = jnp.dot(xp, w_ref[p], preferred_element_type=jnp.float32)
        acc = d if acc is None else acc + d
    return acc


def _compress_kernel(x_ref, xn_ref, pos_ref, wf_ref, ws_ref, kg_ref, kc_ref, vc_ref):
    tn = x_ref.shape[1]
    x = x_ref[0]
    first = _chunk_map(x, wf_ref)
    second = _chunk_map(x, ws_ref)
    second_next = _chunk_map(xn_ref[0], ws_ref)
    bias = _chunk_map(pos_ref[0], wf_ref) + _chunk_map(pos_ref[1], ws_ref)
    rows = lax.broadcasted_iota(jnp.int32, second.shape, 0)
    shifted = jnp.where(rows == tn - 1, second_next[0:1, :], pltpu.roll(second, tn - 1, axis=0))
    out = first + shifted + bias[0:1, :]
    hw = NSA_KV_HEADS * HEAD_DIM
    kc = out[:, 0:hw]
    kc_ref[0] = kc * lax.rsqrt(_group_mean_sq(kc, HEAD_DIM) + RMS_EPS) * kg_ref[...]
    vc_ref[0] = out[:, hw:2 * hw]


def _compress_weights(w_cmp, cmp_pos):
    eye = jnp.eye(NSA_KV_HEADS, dtype=w_cmp.dtype)

    def bd(p):
        blocks = [jnp.kron(eye, w_cmp[r, p]) for r in range(2)]
        z = jnp.zeros_like(blocks[0])
        return jnp.concatenate([jnp.concatenate([blocks[0], z], 1), jnp.concatenate([z, blocks[1]], 1)], 0)

    wf = jnp.stack([bd(p) for p in range(CMP_STRIDE)]).astype(jnp.bfloat16)
    ws = jnp.stack([bd(p + CMP_STRIDE) for p in range(CMP_STRIDE)]).astype(jnp.bfloat16)

    def pos_rows(lo):
        pk = jnp.tile(cmp_pos[0, lo:lo + CMP_STRIDE], (1, NSA_KV_HEADS))
        pv = jnp.tile(cmp_pos[1, lo:lo + CMP_STRIDE], (1, NSA_KV_HEADS))
        row = jnp.concatenate([pk, pv, jnp.zeros_like(pk), jnp.zeros_like(pv)], axis=1)
        flat = row.reshape(1, CMP_STRIDE * ROW_LANES)
        return jnp.concatenate([flat, jnp.zeros((7, flat.shape[1]), flat.dtype)], axis=0)

    pos = jnp.stack([pos_rows(0), pos_rows(CMP_STRIDE)])
    return wf, ws, pos


def _compress_call(kv_chunks, wf, ws, pos, k_norm0, *, tn):
    b, nch, width = kv_chunks.shape
    assert nch % tn == 0 and tn % 8 == 0
    hw = NSA_KV_HEADS * HEAD_DIM
    last8 = nch // 8 - 1
    return pl.pallas_call(
        _compress_kernel,
        grid=(b, nch // tn),
        in_specs=[
            pl.BlockSpec((1, tn, width), lambda i, j: (i, j, 0)),
            pl.BlockSpec((1, 8, width), lambda i, j: (i, jnp.minimum((j + 1) * (tn // 8), last8), 0)),
            pl.BlockSpec((2, 8, width), lambda i, j: (0, 0, 0)),
            pl.BlockSpec((CMP_STRIDE, CMP_LANES, CMP_LANES), lambda i, j: (0, 0, 0)),
            pl.BlockSpec((CMP_STRIDE, CMP_LANES, CMP_LANES), lambda i, j: (0, 0, 0)),
            pl.BlockSpec((1, hw), lambda i, j: (0, 0)),
        ],
        out_specs=[pl.BlockSpec((1, tn, hw), lambda i, j: (i, j, 0))] * 2,
        out_shape=[jax.ShapeDtypeStruct((b, nch, hw), jnp.float32)] * 2,
        compiler_params=pltpu.CompilerParams(
            dimension_semantics=("arbitrary", "arbitrary"), vmem_limit_bytes=V7X_VMEM_LIMIT),
        name="compress",
    )(kv_chunks, kv_chunks, pos, wf, ws, jnp.tile(k_norm0, NSA_KV_HEADS).reshape(1, -1))


SEL_TILE = 1024
SEL_PER_TILE = SEL_TILE // SEL_BLOCK
WIN_KEYS = WINDOW + Q_BLOCK
WIN_BLOCKS = WIN_KEYS // Q_BLOCK
ROW_CHUNK = 64
NEG_BIG = -3.0e38


def _lane_tile(x, reps):
    return jnp.concatenate([x] * reps, axis=1)


def _nsa_prompt_kernel(q_ref, misc_ref, kc_ref, vc_ref, ksvs_ref, w0, w1, w2, w3, w4,
                       psel_ref, pselt_ref, ebig_ref, o_ref,
                       q4_ref, s_ref, p_ref, bias_ref, bq_ref, psum_ref, m_ref, l_ref, al_ref,
                       kw_ref, vw_ref, oc_ref, os_ref, ow_ref, *, n_sel_blocks):
    i = pl.program_id(1)
    s0 = i * Q_BLOCK
    f32, bf16 = jnp.float32, jnp.bfloat16
    hw = NSA_KV_HEADS * HEAD_DIM
    n_chunks = NSA_GROUP * Q_BLOCK // ROW_CHUNK
    halves = Q_BLOCK // ROW_CHUNK

    qb = q_ref[...]
    for h in range(NSA_KV_HEADS):
        for g in range(NSA_GROUP):
            piece = jnp.dot(qb, psel_ref[h * NSA_GROUP + g], preferred_element_type=f32)
            q4_ref[h, g * Q_BLOCK:(g + 1) * Q_BLOCK, :] = piece.astype(bf16)

    for j, w in enumerate((w0, w1, w2, w3, w4)):
        kw_ref[j * Q_BLOCK:(j + 1) * Q_BLOCK, :] = w[:, 0:hw]
        vw_ref[j * Q_BLOCK:(j + 1) * Q_BLOCK, :] = w[:, hw:2 * hw]

    def chunk_rows(c):
        return pl.ds(pl.multiple_of(c * ROW_CHUNK, ROW_CHUNK), ROW_CHUNK)

    def chunk_t(c):
        r = lax.broadcasted_iota(jnp.int32, (ROW_CHUNK, 1), 0)
        return s0 + (c % halves) * ROW_CHUNK + r

    for h in range(NSA_KV_HEADS):
        q4 = q4_ref[h]

        ncp = kc_ref.shape[1]
        nsb = ncp // 4
        s_ref[:, 0:ncp] = lax.dot_general(q4, kc_ref[0], (((1,), (1,)), ((), ())), preferred_element_type=f32)
        psum_ref[...] = jnp.zeros_like(psum_ref)

        def cmp_chunk(c, carry):
            rows = chunk_rows(c)
            t = chunk_t(c)
            col = lax.broadcasted_iota(jnp.int32, (ROW_CHUNK, ncp), 1)
            cidx = (col % nsb) * 4 + col // nsb
            valid = cidx * CMP_STRIDE + (CMP_BLOCK - 1) <= t
            s = jnp.where(valid, s_ref[rows, 0:ncp], MASK_VALUE)
            mx = jnp.max(s, axis=1, keepdims=True)
            e = jnp.exp(s - mx)
            p = jnp.where(valid, e / jnp.sum(e, axis=1, keepdims=True), 0.0)
            p_ref[rows, 0:ncp] = p.astype(bf16)
            hrows = pl.ds(pl.multiple_of((c % halves) * ROW_CHUNK, ROW_CHUNK), ROW_CHUNK)
            psum_ref[hrows, :] += p
            return carry

        lax.fori_loop(0, n_chunks, cmp_chunk, 0)
        oc_ref[h] = jnp.dot(p_ref[:, 0:ncp], vc_ref[0], preferred_element_type=f32)

        ps = psum_ref[...]
        a3 = ps[:, 3 * nsb:4 * nsb]
        blk = lax.broadcasted_iota(jnp.int32, (Q_BLOCK, nsb), 1)
        tq = s0 + lax.broadcasted_iota(jnp.int32, (Q_BLOCK, nsb), 0)
        imp = ps[:, 0:nsb] + ps[:, nsb:2 * nsb] + ps[:, 2 * nsb:3 * nsb] + a3
        imp = imp + jnp.where(blk == 0, 0.0, pltpu.roll(a3, 1, axis=1))
        visible = blk * SEL_BLOCK <= tq
        forced = (blk == 0) | (blk == tq // SEL_BLOCK)
        score = jnp.where(forced, FORCE_SCORE, jnp.where(visible, imp, -1.0))
        blkf = blk.astype(f32)

        def pick(_, carry):
            sc, selm = carry
            mx = jnp.max(sc, axis=1, keepdims=True)
            first = jnp.min(jnp.where(sc == mx, blkf, float(nsb)), axis=1, keepdims=True)
            hit = blkf == first
            return jnp.where(hit, NEG_BIG, sc), jnp.where(hit, 1.0, selm)

        _, selm = lax.fori_loop(0, min(SEL_TOPK, n_sel_blocks), pick, (score, jnp.zeros_like(score)))
        bq_ref[...] = jnp.where(selm > 0.0, 0.0, MASK_VALUE).astype(bf16)

        m_ref[...] = jnp.full_like(m_ref, NEG_BIG)
        l_ref[...] = jnp.zeros_like(l_ref)
        os_ref[h] = jnp.zeros((NSA_GROUP * Q_BLOCK, hw), f32)

        def sel_tile(kt, carry):
            k0 = pl.multiple_of(kt * SEL_TILE, SEL_TILE)
            e_off = pl.multiple_of(nsb - kt * SEL_PER_TILE, SEL_PER_TILE)
            bias_ref[...] = jnp.dot(bq_ref[...], ebig_ref[pl.ds(e_off, nsb), :], preferred_element_type=f32)
            s_ref[...] = lax.dot_general(q4, ksvs_ref[pl.ds(k0, SEL_TILE), 0:hw],
                                         (((1,), (1,)), ((), ())), preferred_element_type=f32)

            def sel_chunk(c, carry2):
                rows = chunk_rows(c)
                hrows = pl.ds(pl.multiple_of((c % halves) * ROW_CHUNK, ROW_CHUNK), ROW_CHUNK)
                t = chunk_t(c)
                key = k0 + lax.broadcasted_iota(jnp.int32, (ROW_CHUNK, SEL_TILE), 1)
                s = jnp.where(key <= t, s_ref[rows, :] + bias_ref[hrows, :], MASK_VALUE)
                m_old = m_ref[rows, :]
                m_new = jnp.maximum(m_old, jnp.max(s, axis=1, keepdims=True))
                p = jnp.exp(s - _lane_tile(m_new, SEL_TILE // 128))
                alpha = jnp.exp(m_old - m_new)
                l_ref[rows, :] = alpha * l_ref[rows, :] + jnp.sum(p, axis=1, keepdims=True)
                m_ref[rows, :] = m_new
                al_ref[rows, :] = alpha
                p_ref[rows, :] = p.astype(bf16)
                return carry2

            lax.fori_loop(0, n_chunks, sel_chunk, 0)
            pv = jnp.dot(p_ref[...], ksvs_ref[pl.ds(k0, SEL_TILE), hw:2 * hw], preferred_element_type=f32)
            os_ref[h] = os_ref[h] * al_ref[...] + pv
            return carry

        lax.fori_loop(0, (s0 + Q_BLOCK - 1) // SEL_TILE + 1, sel_tile, 0)
        os_ref[h] = os_ref[h] / l_ref[...]

        s_ref[:, 0:WIN_KEYS] = lax.dot_general(q4, kw_ref[...], (((1,), (1,)), ((), ())),
                                               preferred_element_type=f32)

        def win_chunk(c, carry):
            rows = chunk_rows(c)
            t = chunk_t(c)
            pos = s0 - WINDOW + lax.broadcasted_iota(jnp.int32, (ROW_CHUNK, WIN_KEYS), 1)
            d = t - pos
            valid = (d >= 0) & (d <= WINDOW) & (pos >= 0)
            s = jnp.where(valid, s_ref[rows, 0:WIN_KEYS], MASK_VALUE)
            mx = jnp.max(s, axis=1, keepdims=True)
            e = jnp.exp(s - mx)
            p_ref[rows, 0:WIN_KEYS] = (e / jnp.sum(e, axis=1, keepdims=True)).astype(bf16)
            return carry

        lax.fori_loop(0, n_chunks, win_chunk, 0)
        ow_ref[h] = jnp.dot(p_ref[:, 0:WIN_KEYS], vw_ref[...], preferred_element_type=f32)

    gsig = jax.nn.sigmoid(misc_ref[...])
    out = jnp.zeros((Q_BLOCK, Q_W), f32)
    for h in range(NSA_KV_HEADS):
        for g in range(NSA_GROUP):
            hg = h * NSA_GROUP + g
            r = slice(g * Q_BLOCK, (g + 1) * Q_BLOCK)
            mix = (gsig[:, 3 * hg:3 * hg + 1] * oc_ref[h, r, :]
                   + gsig[:, 3 * hg + 1:3 * hg + 2] * os_ref[h, r, :]
                   + gsig[:, 3 * hg + 2:3 * hg + 3] * ow_ref[h, r, :])
            out = out + jnp.dot(mix.astype(bf16), pselt_ref[hg], preferred_element_type=f32)
    o_ref[...] = out.astype(o_ref.dtype)


def _nsa_constants(n_sel_blocks):
    hw = NSA_KV_HEADS * HEAD_DIM
    psel = np.zeros((NSA_HEADS, Q_W, hw), np.float32)
    for h in range(NSA_KV_HEADS):
        for g in range(NSA_GROUP):
            hg = h * NSA_GROUP + g
            for d in range(HEAD_DIM):
                psel[hg, hg * HEAD_DIM + d, h * HEAD_DIM + d] = 1.0
    pselt = np.transpose(psel, (0, 2, 1))
    r = np.arange(2 * n_sel_blocks)[:, None] - n_sel_blocks
    ebig = (r == (np.arange(SEL_TILE)[None, :] // SEL_BLOCK)).astype(np.float32)
    return (jnp.asarray(psel, jnp.bfloat16), jnp.asarray(pselt, jnp.bfloat16), jnp.asarray(ebig, jnp.bfloat16))


def _nsa_prompt_call(q, misc, kcp, vcp, kvb, batch, seq):
    assert seq % SEL_TILE == 0 and seq % Q_BLOCK == 0
    nb = seq // Q_BLOCK
    nsb = seq // SEL_BLOCK
    hw = NSA_KV_HEADS * HEAD_DIM
    psel, pselt, ebig = _nsa_constants(nsb)
    rows4 = NSA_GROUP * Q_BLOCK
    f32, bf16 = jnp.float32, jnp.bfloat16

    def win_spec(j):
        return pl.BlockSpec((Q_BLOCK, 2 * hw),
                            lambda b, i: (b * nb + jnp.maximum(i - (WIN_BLOCKS - 1) + j, 0), 1))

    return pl.pallas_call(
        functools.partial(_nsa_prompt_kernel, n_sel_blocks=nsb),
        grid=(batch, nb),
        in_specs=[
            pl.BlockSpec((Q_BLOCK, Q_W), lambda b, i: (b * nb + i, 0)),
            pl.BlockSpec((Q_BLOCK, MISC_W), lambda b, i: (b * nb + i, 0)),
            pl.BlockSpec((1, seq // CMP_STRIDE, hw), lambda b, i: (b, 0, 0)),
            pl.BlockSpec((1, seq // CMP_STRIDE, hw), lambda b, i: (b, 0, 0)),
            pl.BlockSpec((seq, 2 * hw), lambda b, i: (b, 0)),
        ] + [win_spec(j) for j in range(WIN_BLOCKS)] + [
            pl.BlockSpec(psel.shape, lambda b, i: (0, 0, 0)),
            pl.BlockSpec(pselt.shape, lambda b, i: (0, 0, 0)),
            pl.BlockSpec(ebig.shape, lambda b, i: (0, 0)),
        ],
        out_specs=pl.BlockSpec((Q_BLOCK, Q_W), lambda b, i: (b * nb + i, 0)),
        out_shape=jax.ShapeDtypeStruct((batch * seq, Q_W), bf16),
        scratch_shapes=[
            pltpu.VMEM((NSA_KV_HEADS, rows4, hw), bf16),
            pltpu.VMEM((rows4, SEL_TILE), f32),
            pltpu.VMEM((rows4, SEL_TILE), bf16),
            pltpu.VMEM((Q_BLOCK, SEL_TILE), f32),
            pltpu.VMEM((Q_BLOCK, nsb), bf16),
            pltpu.VMEM((Q_BLOCK, seq // CMP_STRIDE), f32),
            pltpu.VMEM((rows4, hw), f32),
            pltpu.VMEM((rows4, hw), f32),
            pltpu.VMEM((rows4, hw), f32),
            pltpu.VMEM((WIN_KEYS, hw), bf16),
            pltpu.VMEM((WIN_KEYS, hw), bf16),
            pltpu.VMEM((NSA_KV_HEADS, rows4, hw), f32),
            pltpu.VMEM((NSA_KV_HEADS, rows4, hw), f32),
            pltpu.VMEM((NSA_KV_HEADS, rows4, hw), f32),
        ],
        compiler_params=pltpu.CompilerParams(
            dimension_semantics=("arbitrary", "arbitrary"), vmem_limit_bytes=V7X_VMEM_LIMIT),
        name="nsa_prompt",
    )(q, misc, kcp, vcp, kvb, kvb, kvb, kvb, kvb, kvb, psel, pselt, ebig)


def _nsa_prompt_pallas(q2d, misc, kvr2d, kvb, cmp_pos, w_cmp, k_norm_g, batch, seq):
    nch = seq // CMP_STRIDE
    wf, ws, pos = _compress_weights(w_cmp, cmp_pos)
    kc, vc = _compress_call(kvr2d.reshape(batch, nch, CMP_STRIDE * ROW_LANES), wf, ws, pos, k_norm_g[0],
                            tn=min(256, nch))

    def perm(a):
        return a.reshape(batch, nch // 4, 4, a.shape[-1]).transpose(0, 2, 1, 3).reshape(batch, nch, -1).astype(jnp.bfloat16)

    return _nsa_prompt_call(q2d, misc, perm(kc), perm(vc), kvb, batch, seq)


def _compress(rows, pos, w):
    b, l = rows.shape[:2]
    nc = (l - CMP_BLOCK) // CMP_STRIDE + 1
    chunks = rows[:, :(nc + 1) * CMP_STRIDE].reshape(b, nc + 1, CMP_STRIDE, NSA_KV_HEADS, HEAD_DIM)
    first = jnp.einsum('bnphd,pde->bnhe', chunks, w[:CMP_STRIDE])
    second = jnp.einsum('bnphd,pde->bnhe', chunks, w[CMP_STRIDE:])
    bias = jnp.einsum('pd,pde->e', pos, w)
    return first[:, :-1] + second[:, 1:] + bias


def _nsa_keys(kv_all, cmp_pos, w_cmp, k_norm_g):
    kc = _rmsnorm(_compress(kv_all[:, :, 0], cmp_pos[0], w_cmp[0]), k_norm_g[0])
    vc = _compress(kv_all[:, :, 1], cmp_pos[1], w_cmp[1])
    return kc, vc, kv_all[:, :, 2], kv_all[:, :, 3]


def _nsa_block(q, t_pos, kc, vc, ks, vs, kw, vw, w_pos, gate):
    b, nq = q.shape[:2]
    nc = kc.shape[1]
    l = ks.shape[1]
    ns = -(-l // SEL_BLOCK)
    f32 = jnp.float32
    cmp_end = jnp.arange(nc) * CMP_STRIDE + (CMP_BLOCK - 1)
    m_c = cmp_end[None, :] <= t_pos[:, None]
    s_c = jnp.einsum('bqhgd,bnhd->bhgqn', q, kc).astype(f32)
    p_c = jax.nn.softmax(jnp.where(m_c, s_c, MASK_VALUE), axis=-1) * m_c
    o_c = jnp.einsum('bhgqn,bnhd->bqhgd', p_c.astype(vc.dtype), vc)
    ratio = SEL_BLOCK // CMP_STRIDE
    imp = jnp.pad(p_c.sum(axis=2), ((0, 0), (0, 0), (0, 0), (0, ns * ratio - nc)))
    imp = imp.reshape(b, NSA_KV_HEADS, nq, ns, ratio)
    imp = imp.sum(-1) + jnp.pad(imp[..., :-1, ratio - 1], ((0, 0), (0, 0), (0, 0), (1, 0)))
    blk = jnp.arange(ns)
    visible = blk[None, :] * SEL_BLOCK <= t_pos[:, None]
    forced = (blk[None, :] == 0) | (blk[None, :] == t_pos[:, None] // SEL_BLOCK)
    score = jnp.where(forced, FORCE_SCORE, jnp.where(visible, imp, -1.0))
    _, idx = lax.top_k(score, min(SEL_TOPK, ns))
    tok = idx[..., None] * SEL_BLOCK + jnp.arange(SEL_BLOCK)
    bi = jnp.arange(b)[:, None, None, None, None]
    hi = jnp.arange(NSA_KV_HEADS)[None, :, None, None, None]
    safe_tok = jnp.minimum(tok, l - 1)
    ks_g = ks[bi, safe_tok, hi]
    vs_g = vs[bi, safe_tok, hi]
    m_s = (tok <= t_pos[None, None, :, None, None])[:, :, None]
    s_s = jnp.einsum('bqhgd,bhqkpd->bhgqkp', q, ks_g).astype(f32)
    s_s = jnp.where(m_s, s_s, MASK_VALUE)
    p_s = jax.nn.softmax(s_s.reshape(s_s.shape[:4] + (-1,)), axis=-1).reshape(s_s.shape)
    o_s = jnp.einsum('bhgqkp,bhqkpd->bqhgd', p_s.astype(vs.dtype), vs_g)
    d = t_pos[:, None] - w_pos[None, :]
    m_w = (d >= 0) & (d <= WINDOW) & (w_pos[None, :] >= 0)
    s_w = jnp.einsum('bqhgd,bkhd->bhgqk', q, kw).astype(f32)
    p_w = jax.nn.softmax(jnp.where(m_w, s_w, MASK_VALUE), axis=-1)
    o_w = jnp.einsum('bhgqk,bkhd->bqhgd', p_w.astype(vw.dtype), vw)
    out = gate[..., 0:1] * o_c + gate[..., 1:2] * o_s + gate[..., 2:3] * o_w
    return out.reshape(b, nq, NSA_HEADS * HEAD_DIM).astype(q.dtype)


def _nsa_prompt(q, kv_rows, win_rows, gate, cmp_pos, w_cmp, k_norm_g):
    b, t = q.shape[:2]
    kc, vc, ks, vs = _nsa_keys(kv_rows, cmp_pos, w_cmp, k_norm_g)
    win = jnp.pad(win_rows, ((0, 0), (WINDOW, 0), (0, 0), (0, 0), (0, 0)))

    def one_block(blk):
        s = blk * Q_BLOCK
        qb = lax.dynamic_slice_in_dim(q, s, Q_BLOCK, axis=1)
        gb = lax.dynamic_slice_in_dim(gate, s, Q_BLOCK, axis=1)
        wb = lax.dynamic_slice_in_dim(win, s, WINDOW + Q_BLOCK, axis=1)
        t_pos = s + jnp.arange(Q_BLOCK)
        w_pos = s - WINDOW + jnp.arange(WINDOW + Q_BLOCK)
        return _nsa_block(qb, t_pos, kc, vc, ks, vs, wb[:, :, 0], wb[:, :, 1], w_pos, gb)

    out = lax.map(one_block, jnp.arange(t // Q_BLOCK))
    return out.transpose(1, 0, 2, 3).reshape(b, t, -1)


def _nsa_sample(q, kv_rows, win_rows, gate, cache_kv_l, cache_win, page_table, cmp_pos, w_cmp, k_norm_g):
    bd, t = q.shape[:2]
    past_len = page_table.shape[1] * PAGE_SIZE
    past = cache_kv_l[page_table].reshape(bd, past_len, KV_ROWS, NSA_KV_HEADS, HEAD_DIM)
    kc, vc, ks, vs = _nsa_keys(jnp.concatenate([past, kv_rows], axis=1), cmp_pos, w_cmp, k_norm_g)
    wbuf = cache_win.shape[1]
    win_all = jnp.concatenate([cache_win, win_rows], axis=1)
    t_pos = past_len + jnp.arange(t)
    w_pos = past_len - wbuf + jnp.arange(wbuf + t)
    out = _nsa_block(q, t_pos, kc, vc, ks, vs, win_all[:, :, 0], win_all[:, :, 1], w_pos, gate)
    return out, win_all[:, -min(WINDOW, past_len + t):]


def _gla_chunked(q, k, v, log_a, s0):
    b, t, h = q.shape[:3]
    c = math.gcd(t, GLA_CHUNK)
    n = t // c

    def to_chunks(a):
        return a.reshape(b, n, c, h, a.shape[-1]).transpose(1, 0, 3, 2, 4)

    causal = jnp.tril(jnp.ones((c, c), dtype=bool))

    def step(S, inp):
        qc, kc, vc, ac = inp
        cum = jnp.cumsum(ac, axis=2)
        diff = jnp.minimum(cum[:, :, :, None] - cum[:, :, None, :], 0.0)
        decay = jnp.where(causal[..., None], jnp.exp(diff), 0.0)
        att = jnp.einsum('bhid,bhjd,bhijd->bhij', qc, kc, decay)
        o = jnp.einsum('bhij,bhjv->bhiv', att, vc) + jnp.einsum('bhid,bhdv->bhiv', qc * jnp.exp(cum), S)
        last = cum[:, :, -1:]
        S = jnp.exp(last)[:, :, 0, :, None] * S + jnp.einsum('bhjd,bhjv->bhdv', kc * jnp.exp(last - cum), vc)
        return S, o

    S, o = lax.scan(step, s0, (to_chunks(q), to_chunks(k), to_chunks(v), to_chunks(log_a)))
    return o.transpose(1, 0, 3, 2, 4).reshape(b, t, h, -1), S


def _gla_mixer(gq, gk, gv, glr, gog, s0, w_gate, b_gate, norm_g):
    b, t = gq.shape[:2]
    f32 = jnp.float32
    q = gq.reshape(b, t, GLA_HEADS, GLA_DK).astype(f32) * (GLA_DK ** -0.5)
    k = gk.reshape(b, t, GLA_HEADS, GLA_DK).astype(f32)
    v = gv.reshape(b, t, GLA_HEADS, GLA_DV).astype(f32)
    log_a = jax.nn.log_sigmoid((glr @ w_gate + b_gate).astype(f32)).reshape(b, t, GLA_HEADS, GLA_DK) / GLA_GATE_TEMP
    o, S = _gla_chunked(q, k, v, log_a, s0.astype(f32))
    o = _rmsnorm(o, norm_g) * jax.nn.silu(gog.astype(f32)).reshape(b, t, GLA_HEADS, GLA_DV)
    return o.reshape(b, t, -1).astype(gq.dtype), S


def _peer_ffn(h, w_query, subkeys, u, v):
    b, t, d = h.shape
    n = b * t
    nb = -(-n // PEER_TOKEN_BLOCK)
    flat = jnp.pad(h.reshape(n, d), ((0, nb * PEER_TOKEN_BLOCK - n), (0, 0))).reshape(nb, PEER_TOKEN_BLOCK, d)

    def one_block(xb):
        qh = (xb @ w_query).reshape(-1, PEER_HEADS, 2, PEER_KEY_DIM // 2)
        s = jnp.einsum('nhcd,hckd->nhck', qh, subkeys).astype(jnp.float32)
        s1, i1 = lax.top_k(s[:, :, 0], PEER_TOPK)
        s2, i2 = lax.top_k(s[:, :, 1], PEER_TOPK)
        cand = (s1[..., :, None] + s2[..., None, :]).reshape(s1.shape[:-1] + (-1,))
        cidx = (i1[..., :, None] * PEER_NKEYS + i2[..., None, :]).reshape(i1.shape[:-1] + (-1,))
        top, pos = lax.top_k(cand, PEER_TOPK)
        eidx = jnp.take_along_axis(cidx, pos, axis=-1)
        g = jax.nn.softmax(top, axis=-1)
        act = jax.nn.gelu(jnp.einsum('nhkd,nd->nhk', jnp.take(u, eidx, axis=0), xb).astype(jnp.float32))
        return jnp.einsum('nhk,nhkd->nd', (g * act).astype(xb.dtype), jnp.take(v, eidx, axis=0))

    out = lax.map(one_block, flat)
    return out.reshape(-1, d)[:n].reshape(b, t, d)


def _residual_tail(x, mix, w_out, norm2_g, w_query, subkeys, u, v):
    x = x + (mix @ w_out).astype(x.dtype)
    return x + _peer_ffn(_rmsnorm(x, norm2_g), w_query, subkeys, u, v).astype(x.dtype)


def _mixer_inputs(x, norm1_g, w_in_r, q_norm_g, k_norm_g, *, tm):
    b, t = x.shape[:2]
    q2d, kvr, winr, misc, gq, gk, gv, gog, kvb = _inproj(
        x.reshape(b * t, D_MODEL), norm1_g, w_in_r, q_norm_g, k_norm_g, tm=tm)
    q = q2d.astype(jnp.float32).reshape(b, t, NSA_KV_HEADS, NSA_GROUP, HEAD_DIM)
    kv_rows = kvr.reshape(b, t, 4, NSA_KV_HEADS, HEAD_DIM)
    win_rows = winr.reshape(b, t, 2, NSA_KV_HEADS, HEAD_DIM)
    gate = jax.nn.sigmoid(misc[:, :GATE_W]).reshape(b, t, NSA_KV_HEADS, NSA_GROUP, 3)
    glr = misc[:, GATE_W:GATE_W + GLA_GATE_RANK].reshape(b, t, -1)
    rs = lambda a: a.reshape(b, t, -1)
    return q, kv_rows, win_rows, gate, (rs(gq), rs(gk), rs(gv), glr, rs(gog)), (q2d, misc, kvr, kvb)


def _token_tile(n):
    for tm in (512, 256, 128, 64, 32, 16, 8):
        if n % tm == 0:
            return tm
    raise ValueError(n)


def kernel(x_prompt, x_sample, cache_kv, cache_win, state_gla, page_table, norm1_g, w_in, q_norm_g, k_norm_g, cmp_pos, w_cmp, gla_w_gate, gla_b_gate, gla_norm_g, w_out, norm2_g, peer_w_query, peer_subkeys, peer_u, peer_v):
    depth = w_in.shape[0]
    xp, xs = x_prompt, x_sample
    kv_p, win_p, gla_p, kv_s, win_s, gla_s = [], [], [], [], [], []
    for l in range(depth):
        w_in_r = _reorder_w_in(w_in[l])
        tmp = _token_tile(xp.shape[0] * xp.shape[1])
        q, kv_rows, win_rows, gate, gla_in, (q2d, misc, kvr, kvb) = _mixer_inputs(
            xp, norm1_g[l], w_in_r, q_norm_g[l], k_norm_g[l], tm=tmp)
        a = _nsa_prompt_pallas(q2d, misc, kvr, kvb, cmp_pos[l], w_cmp[l], k_norm_g[l],
                               xp.shape[0], xp.shape[1]).astype(jnp.float32).reshape(xp.shape[0], xp.shape[1], -1)
        s0 = jnp.zeros((xp.shape[0], GLA_HEADS, GLA_DK, GLA_DV), jnp.float32)
        g, s_new = _gla_mixer(*gla_in, s0, gla_w_gate[l], gla_b_gate[l], gla_norm_g[l])
        kv_p.append(kv_rows)
        win_p.append(win_rows[:, -min(WINDOW, xp.shape[1]):])
        gla_p.append(s_new.astype(state_gla.dtype))
        xp = _residual_tail(xp, jnp.concatenate([a, g], axis=-1), w_out[l], norm2_g[l],
                            peer_w_query[l], peer_subkeys[l], peer_u[l], peer_v[l])
        tms = _token_tile(xs.shape[0] * xs.shape[1])
        q, kv_rows, win_rows, gate, gla_in, _ = _mixer_inputs(xs, norm1_g[l], w_in_r, q_norm_g[l], k_norm_g[l], tm=tms)
        a, win_new = _nsa_sample(q, kv_rows, win_rows, gate, cache_kv[l], cache_win[l], page_table,
                                 cmp_pos[l], w_cmp[l], k_norm_g[l])
        g, s_new = _gla_mixer(*gla_in, state_gla[l], gla_w_gate[l], gla_b_gate[l], gla_norm_g[l])
        kv_s.append(kv_rows)
        win_s.append(win_new)
        gla_s.append(s_new.astype(state_gla.dtype))
        xs = _residual_tail(xs, jnp.concatenate([a, g], axis=-1), w_out[l], norm2_g[l],
                            peer_w_query[l], peer_subkeys[l], peer_u[l], peer_v[l])
    return (xp, xs, jnp.stack(kv_p), jnp.stack(win_p), jnp.stack(gla_p),
            jnp.stack(kv_s), jnp.stack(win_s), jnp.stack(gla_s))
```

```python
import functools
import math

import jax
import jax.numpy as jnp
import numpy as np
from jax import lax
from jax.experimental import pallas as pl
from jax.experimental.pallas import tpu as pltpu
from jax.experimental.pallas import tpu_sc as plsc

D_MODEL = 1024
NSA_HEADS = 8
NSA_KV_HEADS = 2
NSA_GROUP = NSA_HEADS // NSA_KV_HEADS
HEAD_DIM = 64
CMP_STRIDE = 16
CMP_BLOCK = 32
SEL_BLOCK = 64
SEL_TOPK = 16
WINDOW = 512
Q_BLOCK = 128
PAGE_SIZE = 128
GLA_HEADS = 4
GLA_DV = 128
GLA_DK = 64
GLA_GATE_RANK = 16
GLA_GATE_TEMP = 16.0
GLA_CHUNK = 64
PEER_HEADS = 8
PEER_NKEYS = 128
PEER_KEY_DIM = 256
PEER_TOPK = 16
PEER_TOKEN_BLOCK = 128
KV_ROWS = 4
RMS_EPS = 1e-6
MASK_VALUE = -1e30
FORCE_SCORE = 1e4

Q_W = NSA_HEADS * HEAD_DIM
KV_W = 6 * NSA_KV_HEADS * HEAD_DIM
GATE_W = 3 * NSA_HEADS
GQ_W = GLA_HEADS * GLA_DK
GV_W = GLA_HEADS * GLA_DV
MISC_W = 128
IN_SIZES = (Q_W, KV_W, GATE_W, GQ_W, GQ_W, GV_W, GLA_GATE_RANK, GV_W)
P_W = Q_W + KV_W + GQ_W + GQ_W + GV_W + GV_W + MISC_W

V7X_VMEM_LIMIT = 56 * 1024 * 1024


def _rmsnorm(x, g):
    xf = x.astype(jnp.float32)
    y = xf * lax.rsqrt(jnp.mean(xf * xf, axis=-1, keepdims=True) + RMS_EPS)
    return (y * g.astype(jnp.float32)).astype(x.dtype)


def _head_group_ones(width, group, dtype):
    r = lax.broadcasted_iota(jnp.int32, (width, width), 0) // group
    c = lax.broadcasted_iota(jnp.int32, (width, width), 1) // group
    return jnp.where(r == c, 1.0, 0.0).astype(dtype)


def _group_mean_sq(x, group):
    sq = x * x
    hi = sq.astype(jnp.bfloat16)
    lo = (sq - hi.astype(jnp.float32)).astype(jnp.bfloat16)
    ones = _head_group_ones(x.shape[-1], group, jnp.bfloat16)
    s = jnp.dot(hi, ones, preferred_element_type=jnp.float32)
    s = s + jnp.dot(lo, ones, preferred_element_type=jnp.float32)
    return s * (1.0 / group)


def _inproj_kernel(x_ref, g1_ref, w_ref, qg_ref, ksg_ref, kwg_ref,
                   q_ref, kv_ref, win_ref, misc_ref, gq_ref, gk_ref, gv_ref, gog_ref, kvb_ref):
    x = x_ref[...]
    h = x * lax.rsqrt(jnp.mean(x * x, axis=-1, keepdims=True) + RMS_EPS) * g1_ref[...]
    p = jnp.dot(h.astype(jnp.bfloat16), w_ref[...], preferred_element_type=jnp.float32)
    o = 0
    q = p[:, o:o + Q_W]; o += Q_W
    kv = p[:, o:o + KV_W]; o += KV_W
    gq_ref[...] = p[:, o:o + GQ_W]; o += GQ_W
    gk_ref[...] = p[:, o:o + GQ_W]; o += GQ_W
    gv_ref[...] = p[:, o:o + GV_W]; o += GV_W
    gog_ref[...] = p[:, o:o + GV_W]; o += GV_W
    misc_ref[...] = p[:, o:o + MISC_W]
    qn = q * lax.rsqrt(_group_mean_sq(q, HEAD_DIM) + RMS_EPS) * qg_ref[...] * (HEAD_DIM ** -0.5)
    q_ref[...] = qn.astype(q_ref.dtype)
    hw = NSA_KV_HEADS * HEAD_DIM
    k_sel = kv[:, 2 * hw:3 * hw]
    k_sel = k_sel * lax.rsqrt(_group_mean_sq(k_sel, HEAD_DIM) + RMS_EPS) * ksg_ref[...]
    k_win = kv[:, 4 * hw:5 * hw]
    k_win = k_win * lax.rsqrt(_group_mean_sq(k_win, HEAD_DIM) + RMS_EPS) * kwg_ref[...]
    kv_ref[:, 0:2 * hw] = kv[:, 0:2 * hw]
    kv_ref[:, 2 * hw:3 * hw] = k_sel
    kv_ref[:, 3 * hw:4 * hw] = kv[:, 3 * hw:4 * hw]
    win_ref[:, 0:hw] = k_win
    win_ref[:, hw:2 * hw] = kv[:, 5 * hw:6 * hw]
    kvb_ref[:, 0:hw] = k_sel.astype(kvb_ref.dtype)
    kvb_ref[:, hw:2 * hw] = kv[:, 3 * hw:4 * hw].astype(kvb_ref.dtype)
    kvb_ref[:, 2 * hw:3 * hw] = k_win.astype(kvb_ref.dtype)
    kvb_ref[:, 3 * hw:4 * hw] = kv[:, 5 * hw:6 * hw].astype(kvb_ref.dtype)


def _reorder_w_in(w_in):
    offs = np.cumsum((0,) + IN_SIZES)
    q, kv, gate, gq, gk, gv, glr, gog = [w_in[:, offs[i]:offs[i + 1]] for i in range(8)]
    pad = jnp.zeros((w_in.shape[0], MISC_W - GATE_W - GLA_GATE_RANK), w_in.dtype)
    return jnp.concatenate([q, kv, gq, gk, gv, gog, gate, glr, pad], axis=1).astype(jnp.bfloat16)


def _inproj(x2d, norm1_g, w_in_r, q_norm_g, k_norm_g, *, tm):
    n = x2d.shape[0]
    assert n % tm == 0
    hw = NSA_KV_HEADS * HEAD_DIM
    f32 = jnp.float32
    row = lambda i: (i, 0)
    const = lambda i: (0, 0)
    widths = (Q_W, 4 * hw, 2 * hw, MISC_W, GQ_W, GQ_W, GV_W, GV_W, 4 * hw)
    bf16 = jnp.bfloat16
    dtypes = (bf16, f32, f32, f32, f32, f32, f32, f32, bf16)
    return pl.pallas_call(
        _inproj_kernel,
        grid=(n // tm,),
        in_specs=[
            pl.BlockSpec((tm, D_MODEL), row),
            pl.BlockSpec((1, D_MODEL), const),
            pl.BlockSpec((D_MODEL, P_W), const),
            pl.BlockSpec((1, Q_W), const),
            pl.BlockSpec((1, hw), const),
            pl.BlockSpec((1, hw), const),
        ],
        out_specs=[pl.BlockSpec((tm, w), row) for w in widths],
        out_shape=[jax.ShapeDtypeStruct((n, w), dt) for w, dt in zip(widths, dtypes)],
        compiler_params=pltpu.CompilerParams(
            dimension_semantics=("arbitrary",), vmem_limit_bytes=V7X_VMEM_LIMIT),
        name="inproj",
    )(x2d, norm1_g.reshape(1, -1), w_in_r,
      jnp.tile(q_norm_g, NSA_HEADS).reshape(1, -1),
      jnp.tile(k_norm_g[1], NSA_KV_HEADS).reshape(1, -1),
      jnp.tile(k_norm_g[2], NSA_KV_HEADS).reshape(1, -1))


CMP_LANES = 2 * NSA_KV_HEADS * HEAD_DIM
ROW_LANES = KV_ROWS * NSA_KV_HEADS * HEAD_DIM


def _chunk_map(x, w_ref):
    acc = None
    for p in range(CMP_STRIDE):
        xp = x[:, p * ROW_LANES:p * ROW_LANES + CMP_LANES].astype(jnp.bfloat16)
        d = jnp.dot(xp, w_ref[p], preferred_element_type=jnp.float32)
        acc = d if acc is None else acc + d
    return acc


def _compress_kernel(x_ref, xn_ref, pos_ref, wf_ref, ws_ref, kg_ref, kc_ref, vc_ref):
    tn = x_ref.shape[1]
    x = x_ref[0]
    first = _chunk_map(x, wf_ref)
    second = _chunk_map(x, ws_ref)
    second_next = _chunk_map(xn_ref[0], ws_ref)
    bias = _chunk_map(pos_ref[0], wf_ref) + _chunk_map(pos_ref[1], ws_ref)
    rows = lax.broadcasted_iota(jnp.int32, second.shape, 0)
    shifted = jnp.where(rows == tn - 1, second_next[0:1, :], pltpu.roll(second, tn - 1, axis=0))
    out = first + shifted + bias[0:1, :]
    hw = NSA_KV_HEADS * HEAD_DIM
    kc = out[:, 0:hw]
    kc_ref[0] = kc * lax.rsqrt(_group_mean_sq(kc, HEAD_DIM) + RMS_EPS) * kg_ref[...]
    vc_ref[0] = out[:, hw:2 * hw]


def _compress_weights(w_cmp, cmp_pos):
    eye = jnp.eye(NSA_KV_HEADS, dtype=w_cmp.dtype)

    def bd(p):
        blocks = [jnp.kron(eye, w_cmp[r, p]) for r in range(2)]
        z = jnp.zeros_like(blocks[0])
        return jnp.concatenate([jnp.concatenate([blocks[0], z], 1), jnp.concatenate([z, blocks[1]], 1)], 0)

    wf = jnp.stack([bd(p) for p in range(CMP_STRIDE)]).astype(jnp.bfloat16)
    ws = jnp.stack([bd(p + CMP_STRIDE) for p in range(CMP_STRIDE)]).astype(jnp.bfloat16)

    def pos_rows(lo):
        pk = jnp.tile(cmp_pos[0, lo:lo + CMP_STRIDE], (1, NSA_KV_HEADS))
        pv = jnp.tile(cmp_pos[1, lo:lo + CMP_STRIDE], (1, NSA_KV_HEADS))
        row = jnp.concatenate([pk, pv, jnp.zeros_like(pk), jnp.zeros_like(pv)], axis=1)
        flat = row.reshape(1, CMP_STRIDE * ROW_LANES)
        return jnp.concatenate([flat, jnp.zeros((7, flat.shape[1]), flat.dtype)], axis=0)

    pos = jnp.stack([pos_rows(0), pos_rows(CMP_STRIDE)])
    return wf, ws, pos


def _compress_call(kv_chunks, wf, ws, pos, k_norm0, *, tn):
    b, nch, width = kv_chunks.shape
    assert nch % tn == 0 and tn % 8 == 0
    hw = NSA_KV_HEADS * HEAD_DIM
    last8 = nch // 8 - 1
    return pl.pallas_call(
        _compress_kernel,
        grid=(b, nch // tn),
        in_specs=[
            pl.BlockSpec((1, tn, width), lambda i, j: (i, j, 0)),
            pl.BlockSpec((1, 8, width), lambda i, j: (i, jnp.minimum((j + 1) * (tn // 8), last8), 0)),
            pl.BlockSpec((2, 8, width), lambda i, j: (0, 0, 0)),
            pl.BlockSpec((CMP_STRIDE, CMP_LANES, CMP_LANES), lambda i, j: (0, 0, 0)),
            pl.BlockSpec((CMP_STRIDE, CMP_LANES, CMP_LANES), lambda i, j: (0, 0, 0)),
            pl.BlockSpec((1, hw), lambda i, j: (0, 0)),
        ],
        out_specs=[pl.BlockSpec((1, tn, hw), lambda i, j: (i, j, 0))] * 2,
        out_shape=[jax.ShapeDtypeStruct((b, nch, hw), jnp.float32)] * 2,
        compiler_params=pltpu.CompilerParams(
            dimension_semantics=("arbitrary", "arbitrary"), vmem_limit_bytes=V7X_VMEM_LIMIT),
        name="compress",
    )(kv_chunks, kv_chunks, pos, wf, ws, jnp.tile(k_norm0, NSA_KV_HEADS).reshape(1, -1))


SEL_TILE = 1024
SEL_PER_TILE = SEL_TILE // SEL_BLOCK
WIN_KEYS = WINDOW + Q_BLOCK
WIN_BLOCKS = WIN_KEYS // Q_BLOCK
ROW_CHUNK = 64
NEG_BIG = -3.0e38


def _lane_tile(x, reps):
    return jnp.concatenate([x] * reps, axis=1)


def _nsa_prompt_kernel(q_ref, misc_ref, kc_ref, vc_ref, ksvs_ref, w0, w1, w2, w3, w4,
                       psel_ref, pselt_ref, ebig_ref, o_ref,
                       q4_ref, s_ref, p_ref, bias_ref, bq_ref, psum_ref, m_ref, l_ref, al_ref,
                       kw_ref, vw_ref, oc_ref, os_ref, ow_ref, *, n_sel_blocks):
    i = pl.program_id(1)
    s0 = i * Q_BLOCK
    f32, bf16 = jnp.float32, jnp.bfloat16
    hw = NSA_KV_HEADS * HEAD_DIM
    n_chunks = NSA_GROUP * Q_BLOCK // ROW_CHUNK
    halves = Q_BLOCK // ROW_CHUNK

    qb = q_ref[...]
    for h in range(NSA_KV_HEADS):
        for g in range(NSA_GROUP):
            piece = jnp.dot(qb, psel_ref[h * NSA_GROUP + g], preferred_element_type=f32)
            q4_ref[h, g * Q_BLOCK:(g + 1) * Q_BLOCK, :] = piece.astype(bf16)

    for j, w in enumerate((w0, w1, w2, w3, w4)):
        kw_ref[j * Q_BLOCK:(j + 1) * Q_BLOCK, :] = w[:, 0:hw]
        vw_ref[j * Q_BLOCK:(j + 1) * Q_BLOCK, :] = w[:, hw:2 * hw]

    def chunk_rows(c):
        return pl.ds(pl.multiple_of(c * ROW_CHUNK, ROW_CHUNK), ROW_CHUNK)

    def chunk_t(c):
        r = lax.broadcasted_iota(jnp.int32, (ROW_CHUNK, 1), 0)
        return s0 + (c % halves) * ROW_CHUNK + r

    for h in range(NSA_KV_HEADS):
        q4 = q4_ref[h]

        ncp = kc_ref.shape[1]
        nsb = ncp // 4
        s_ref[:, 0:ncp] = lax.dot_general(q4, kc_ref[0], (((1,), (1,)), ((), ())), preferred_element_type=f32)
        psum_ref[...] = jnp.zeros_like(psum_ref)

        def cmp_chunk(c, carry):
            rows = chunk_rows(c)
            t = chunk_t(c)
            col = lax.broadcasted_iota(jnp.int32, (ROW_CHUNK, ncp), 1)
            cidx = (col % nsb) * 4 + col // nsb
            valid = cidx * CMP_STRIDE + (CMP_BLOCK - 1) <= t
            s = jnp.where(valid, s_ref[rows, 0:ncp], MASK_VALUE)
            mx = jnp.max(s, axis=1, keepdims=True)
            e = jnp.exp(s - mx)
            p = jnp.where(valid, e / jnp.sum(e, axis=1, keepdims=True), 0.0)
            p_ref[rows, 0:ncp] = p.astype(bf16)
            hrows = pl.ds(pl.multiple_of((c % halves) * ROW_CHUNK, ROW_CHUNK), ROW_CHUNK)
            psum_ref[hrows, :] += p
            return carry

        lax.fori_loop(0, n_chunks, cmp_chunk, 0)
        oc_ref[h] = jnp.dot(p_ref[:, 0:ncp], vc_ref[0], preferred_element_type=f32)

        ps = psum_ref[...]
        a3 = ps[:, 3 * nsb:4 * nsb]
        blk = lax.broadcasted_iota(jnp.int32, (Q_BLOCK, nsb), 1)
        tq = s0 + lax.broadcasted_iota(jnp.int32, (Q_BLOCK, nsb), 0)
        imp = ps[:, 0:nsb] + ps[:, nsb:2 * nsb] + ps[:, 2 * nsb:3 * nsb] + a3
        imp = imp + jnp.where(blk == 0, 0.0, pltpu.roll(a3, 1, axis=1))
        visible = blk * SEL_BLOCK <= tq
        forced = (blk == 0) | (blk == tq // SEL_BLOCK)
        score = jnp.where(forced, FORCE_SCORE, jnp.where(visible, imp, -1.0))
        blkf = blk.astype(f32)

        def pick(_, carry):
            sc, selm = carry
            mx = jnp.max(sc, axis=1, keepdims=True)
            first = jnp.min(jnp.where(sc == mx, blkf, float(nsb)), axis=1, keepdims=True)
            hit = blkf == first
            return jnp.where(hit, NEG_BIG, sc), jnp.where(hit, 1.0, selm)

        _, selm = lax.fori_loop(0, min(SEL_TOPK, n_sel_blocks), pick, (score, jnp.zeros_like(score)))
        bq_ref[...] = jnp.where(selm > 0.0, 0.0, MASK_VALUE).astype(bf16)

        m_ref[...] = jnp.full_like(m_ref, NEG_BIG)
        l_ref[...] = jnp.zeros_like(l_ref)
        os_ref[h] = jnp.zeros((NSA_GROUP * Q_BLOCK, hw), f32)

        def sel_tile(kt, carry):
            k0 = pl.multiple_of(kt * SEL_TILE, SEL_TILE)
            e_off = pl.multiple_of(nsb - kt * SEL_PER_TILE, SEL_PER_TILE)
            bias_ref[...] = jnp.dot(bq_ref[...], ebig_ref[pl.ds(e_off, nsb), :], preferred_element_type=f32)
            s_ref[...] = lax.dot_general(q4, ksvs_ref[pl.ds(k0, SEL_TILE), 0:hw],
                                         (((1,), (1,)), ((), ())), preferred_element_type=f32)

            def sel_chunk(c, carry2):
                rows = chunk_rows(c)
                hrows = pl.ds(pl.multiple_of((c % halves) * ROW_CHUNK, ROW_CHUNK), ROW_CHUNK)
                t = chunk_t(c)
                key = k0 + lax.broadcasted_iota(jnp.int32, (ROW_CHUNK, SEL_TILE), 1)
                s = jnp.where(key <= t, s_ref[rows, :] + bias_ref[hrows, :], MASK_VALUE)
                m_old = m_ref[rows, :]
                m_new = jnp.maximum(m_old, jnp.max(s, axis=1, keepdims=True))
                p = jnp.exp(s - _lane_tile(m_new, SEL_TILE // 128))
                alpha = jnp.exp(m_old - m_new)
                l_ref[rows, :] = alpha * l_ref[rows, :] + jnp.sum(p, axis=1, keepdims=True)
                m_ref[rows, :] = m_new
                al_ref[rows, :] = alpha
                p_ref[rows, :] = p.astype(bf16)
                return carry2

            lax.fori_loop(0, n_chunks, sel_chunk, 0)
            pv = jnp.dot(p_ref[...], ksvs_ref[pl.ds(k0, SEL_TILE), hw:2 * hw], preferred_element_type=f32)
            os_ref[h] = os_ref[h] * al_ref[...] + pv
            return carry

        lax.fori_loop(0, (s0 + Q_BLOCK - 1) // SEL_TILE + 1, sel_tile, 0)
        os_ref[h] = os_ref[h] / l_ref[...]

        s_ref[:, 0:WIN_KEYS] = lax.dot_general(q4, kw_ref[...], (((1,), (1,)), ((), ())),
                                               preferred_element_type=f32)

        def win_chunk(c, carry):
            rows = chunk_rows(c)
            t = chunk_t(c)
            pos = s0 - WINDOW + lax.broadcasted_iota(jnp.int32, (ROW_CHUNK, WIN_KEYS), 1)
            d = t - pos
            valid = (d >= 0) & (d <= WINDOW) & (pos >= 0)
            s = jnp.where(valid, s_ref[rows, 0:WIN_KEYS], MASK_VALUE)
            mx = jnp.max(s, axis=1, keepdims=True)
            e = jnp.exp(s - mx)
            p_ref[rows, 0:WIN_KEYS] = (e / jnp.sum(e, axis=1, keepdims=True)).astype(bf16)
            return carry

        lax.fori_loop(0, n_chunks, win_chunk, 0)
        ow_ref[h] = jnp.dot(p_ref[:, 0:WIN_KEYS], vw_ref[...], preferred_element_type=f32)

    gsig = jax.nn.sigmoid(misc_ref[...])
    out = jnp.zeros((Q_BLOCK, Q_W), f32)
    for h in range(NSA_KV_HEADS):
        for g in range(NSA_GROUP):
            hg = h * NSA_GROUP + g
            r = slice(g * Q_BLOCK, (g + 1) * Q_BLOCK)
            mix = (gsig[:, 3 * hg:3 * hg + 1] * oc_ref[h, r, :]
                   + gsig[:, 3 * hg + 1:3 * hg + 2] * os_ref[h, r, :]
                   + gsig[:, 3 * hg + 2:3 * hg + 3] * ow_ref[h, r, :])
            out = out + jnp.dot(mix.astype(bf16), pselt_ref[hg], preferred_element_type=f32)
    o_ref[...] = out.astype(o_ref.dtype)


def _nsa_constants(n_sel_blocks):
    hw = NSA_KV_HEADS * HEAD_DIM
    psel = np.zeros((NSA_HEADS, Q_W, hw), np.float32)
    for h in range(NSA_KV_HEADS):
        for g in range(NSA_GROUP):
            hg = h * NSA_GROUP + g
            for d in range(HEAD_DIM):
                psel[hg, hg * HEAD_DIM + d, h * HEAD_DIM + d] = 1.0
    pselt = np.transpose(psel, (0, 2, 1))
    r = np.arange(2 * n_sel_blocks)[:, None] - n_sel_blocks
    ebig = (r == (np.arange(SEL_TILE)[None, :] // SEL_BLOCK)).astype(np.float32)
    return (jnp.asarray(psel, jnp.bfloat16), jnp.asarray(pselt, jnp.bfloat16), jnp.asarray(ebig, jnp.bfloat16))


def _nsa_prompt_call(q, misc, kcp, vcp, kvb, batch, seq):
    assert seq % SEL_TILE == 0 and seq % Q_BLOCK == 0
    nb = seq // Q_BLOCK
    nsb = seq // SEL_BLOCK
    hw = NSA_KV_HEADS * HEAD_DIM
    psel, pselt, ebig = _nsa_constants(nsb)
    rows4 = NSA_GROUP * Q_BLOCK
    f32, bf16 = jnp.float32, jnp.bfloat16

    def win_spec(j):
        return pl.BlockSpec((Q_BLOCK, 2 * hw),
                            lambda b, i: (b * nb + jnp.maximum(i - (WIN_BLOCKS - 1) + j, 0), 1))

    return pl.pallas_call(
        functools.partial(_nsa_prompt_kernel, n_sel_blocks=nsb),
        grid=(batch, nb),
        in_specs=[
            pl.BlockSpec((Q_BLOCK, Q_W), lambda b, i: (b * nb + i, 0)),
            pl.BlockSpec((Q_BLOCK, MISC_W), lambda b, i: (b * nb + i, 0)),
            pl.BlockSpec((1, seq // CMP_STRIDE, hw), lambda b, i: (b, 0, 0)),
            pl.BlockSpec((1, seq // CMP_STRIDE, hw), lambda b, i: (b, 0, 0)),
            pl.BlockSpec((seq, 2 * hw), lambda b, i: (b, 0)),
        ] + [win_spec(j) for j in range(WIN_BLOCKS)] + [
            pl.BlockSpec(psel.shape, lambda b, i: (0, 0, 0)),
            pl.BlockSpec(pselt.shape, lambda b, i: (0, 0, 0)),
            pl.BlockSpec(ebig.shape, lambda b, i: (0, 0)),
        ],
        out_specs=pl.BlockSpec((Q_BLOCK, Q_W), lambda b, i: (b * nb + i, 0)),
        out_shape=jax.ShapeDtypeStruct((batch * seq, Q_W), bf16),
        scratch_shapes=[
            pltpu.VMEM((NSA_KV_HEADS, rows4, hw), bf16),
            pltpu.VMEM((rows4, SEL_TILE), f32),
            pltpu.VMEM((rows4, SEL_TILE), bf16),
            pltpu.VMEM((Q_BLOCK, SEL_TILE), f32),
            pltpu.VMEM((Q_BLOCK, nsb), bf16),
            pltpu.VMEM((Q_BLOCK, seq // CMP_STRIDE), f32),
            pltpu.VMEM((rows4, hw), f32),
            pltpu.VMEM((rows4, hw), f32),
            pltpu.VMEM((rows4, hw), f32),
            pltpu.VMEM((WIN_KEYS, hw), bf16),
            pltpu.VMEM((WIN_KEYS, hw), bf16),
            pltpu.VMEM((NSA_KV_HEADS, rows4, hw), f32),
            pltpu.VMEM((NSA_KV_HEADS, rows4, hw), f32),
            pltpu.VMEM((NSA_KV_HEADS, rows4, hw), f32),
        ],
        compiler_params=pltpu.CompilerParams(
            dimension_semantics=("arbitrary", "arbitrary"), vmem_limit_bytes=V7X_VMEM_LIMIT),
        name="nsa_prompt",
    )(q, misc, kcp, vcp, kvb, kvb, kvb, kvb, kvb, kvb, psel, pselt, ebig)


def _nsa_prompt_pallas(q2d, misc, kvr2d, kvb, cmp_pos, w_cmp, k_norm_g, batch, seq):
    nch = seq // CMP_STRIDE
    wf, ws, pos = _compress_weights(w_cmp, cmp_pos)
    kc, vc = _compress_call(kvr2d.reshape(batch, nch, CMP_STRIDE * ROW_LANES), wf, ws, pos, k_norm_g[0],
                            tn=min(256, nch))

    def perm(a):
        return a.reshape(batch, nch // 4, 4, a.shape[-1]).transpose(0, 2, 1, 3).reshape(batch, nch, -1).astype(jnp.bfloat16)

    return _nsa_prompt_call(q2d, misc, perm(kc), perm(vc), kvb, batch, seq)


PEER_GROUPS = 2 * PEER_HEADS
PEER_HALF = PEER_KEY_DIM // 2
PEER_SLOTS = PEER_HEADS * PEER_TOPK
PEER_WORDS = D_MODEL // 2


def _tail_kernel(x_ref, a_ref, g_ref, wo_ref, n2_ref, wq_ref, sk_ref, x1_ref, h2_ref, st_ref):
    f32, bf16 = jnp.float32, jnp.bfloat16
    half = wo_ref.shape[0] // 2
    mix = jnp.dot(a_ref[...], wo_ref[0:half, :], preferred_element_type=f32)
    mix = mix + jnp.dot(g_ref[...], wo_ref[half:2 * half, :], preferred_element_type=f32)
    x1 = x_ref[...] + mix
    x1_ref[...] = x1
    h2 = x1 * lax.rsqrt(jnp.mean(x1 * x1, axis=-1, keepdims=True) + RMS_EPS) * n2_ref[...]
    h2_ref[...] = h2
    qh = jnp.dot(h2.astype(bf16), wq_ref[...], preferred_element_type=f32).astype(bf16)
    for c in range(PEER_GROUPS):
        st_ref[c] = lax.dot_general(sk_ref[c], qh[:, c * PEER_HALF:(c + 1) * PEER_HALF],
                                    (((1,), (1,)), ((), ())), preferred_element_type=f32)


def _tail_call(x2d, a, g, w_out, norm2_g, w_query, subkeys, *, tm):
    n = x2d.shape[0]
    assert n % tm == 0
    f32, bf16 = jnp.float32, jnp.bfloat16
    row = lambda i: (i, 0)
    const = lambda i: (0, 0)
    sk = subkeys.reshape(PEER_GROUPS, PEER_NKEYS, PEER_HALF).astype(bf16)
    return pl.pallas_call(
        _tail_kernel,
        grid=(n // tm,),
        in_specs=[
            pl.BlockSpec((tm, D_MODEL), row),
            pl.BlockSpec((tm, Q_W), row),
            pl.BlockSpec((tm, GV_W), row),
            pl.BlockSpec((Q_W + GV_W, D_MODEL), const),
            pl.BlockSpec((1, D_MODEL), const),
            pl.BlockSpec((D_MODEL, PEER_HEADS * PEER_KEY_DIM), const),
            pl.BlockSpec((PEER_GROUPS, PEER_NKEYS, PEER_HALF), lambda i: (0, 0, 0)),
        ],
        out_specs=[pl.BlockSpec((tm, D_MODEL), row), pl.BlockSpec((tm, D_MODEL), row),
                   pl.BlockSpec((PEER_GROUPS, PEER_NKEYS, tm), lambda i: (0, 0, i))],
        out_shape=[jax.ShapeDtypeStruct((n, D_MODEL), f32), jax.ShapeDtypeStruct((n, D_MODEL), f32),
                   jax.ShapeDtypeStruct((PEER_GROUPS, PEER_NKEYS, n), f32)],
        compiler_params=pltpu.CompilerParams(
            dimension_semantics=("arbitrary",), vmem_limit_bytes=V7X_VMEM_LIMIT),
        name="tail_proj",
    )(x2d, a, g, w_out.astype(bf16), norm2_g.reshape(1, -1), w_query.astype(bf16), sk)


def _extract_topk(x, ids, k):
    r = x.shape[0]
    rows = lax.broadcasted_iota(jnp.int32, x.shape, 0).astype(jnp.float32)
    vals, picked = [], []
    for _ in range(k):
        mx = jnp.max(x, axis=0, keepdims=True)
        first = jnp.min(jnp.where(x == mx, rows, float(r)), axis=0, keepdims=True)
        hit = rows == first
        vals.append(mx)
        picked.append(jnp.sum(jnp.where(hit, ids, 0.0), axis=0, keepdims=True))
        x = jnp.where(hit, NEG_BIG, x)
    return vals, picked


def _peer_topk_kernel(st_ref, e_ref, g_ref):
    f32 = jnp.float32
    t = st_ref.shape[2]
    key_ids = lax.broadcasted_iota(jnp.int32, (PEER_NKEYS, t), 0).astype(f32)

    def head(h, carry):
        v1, i1 = _extract_topk(st_ref[2 * h], key_ids, PEER_TOPK)
        v2, i2 = _extract_topk(st_ref[2 * h + 1], key_ids, PEER_TOPK)
        s2 = jnp.concatenate(v2, axis=0)
        j2 = jnp.concatenate(i2, axis=0)
        cand = jnp.concatenate([v1[a] + s2 for a in range(PEER_TOPK)], axis=0)
        cidx = jnp.concatenate([i1[a] * float(PEER_NKEYS) + j2 for a in range(PEER_TOPK)], axis=0)
        top, eid = _extract_topk(cand, cidx, PEER_TOPK)
        top = jnp.concatenate(top, axis=0)
        e = jnp.exp(top - top[0:1, :])
        rows = pl.ds(pl.multiple_of(h * PEER_TOPK, PEER_TOPK), PEER_TOPK)
        g_ref[rows, :] = e / jnp.sum(e, axis=0, keepdims=True)
        e_ref[rows, :] = jnp.concatenate(eid, axis=0).astype(jnp.int32)
        return carry

    lax.fori_loop(0, PEER_HEADS, head, 0)


def _peer_topk_call(st, *, tt):
    n = st.shape[2]
    assert n % tt == 0
    return pl.pallas_call(
        _peer_topk_kernel,
        grid=(n // tt,),
        in_specs=[pl.BlockSpec((PEER_GROUPS, PEER_NKEYS, tt), lambda i: (0, 0, i))],
        out_specs=[pl.BlockSpec((PEER_SLOTS, tt), lambda i: (0, i))] * 2,
        out_shape=[jax.ShapeDtypeStruct((PEER_SLOTS, n), jnp.int32),
                   jax.ShapeDtypeStruct((PEER_SLOTS, n), jnp.float32)],
        compiler_params=pltpu.CompilerParams(
            dimension_semantics=("arbitrary",), vmem_limit_bytes=V7X_VMEM_LIMIT),
        name="peer_topk",
    )(st)


SC_CORES = 2
SC_SUBCORES = 16
SC_WORKERS = SC_CORES * SC_SUBCORES
SC_CHUNK = 64
SC_IDX_BLOCK = 2048
SC_CHUNKS_PER_BLOCK = SC_IDX_BLOCK // SC_CHUNK


def _sc_gather_rows(table, idx):
    m = idx.shape[0]
    width = table.shape[1]
    assert m % (SC_WORKERS * SC_IDX_BLOCK) == 0 and SC_CHUNKS_PER_BLOCK % 2 == 0
    chunks_per_worker = m // SC_WORKERS // SC_CHUNK
    cpb = SC_CHUNKS_PER_BLOCK
    mesh = plsc.VectorSubcoreMesh(core_axis_name="c", subcore_axis_name="s",
                                  num_cores=SC_CORES, num_subcores=SC_SUBCORES)

    @functools.partial(
        pl.kernel, mesh=mesh,
        out_type=jax.ShapeDtypeStruct((m, width), table.dtype),
        scratch_types=[pltpu.VMEM((cpb, SC_CHUNK), jnp.int32),
                       pltpu.VMEM((SC_CHUNK, width), table.dtype),
                       pltpu.VMEM((SC_CHUNK, width), table.dtype),
                       pltpu.SemaphoreType.DMA, pltpu.SemaphoreType.DMA,
                       pltpu.SemaphoreType.DMA, pltpu.SemaphoreType.DMA],
        name="peer_gather",
    )
    def gather_kernel(table_hbm, idx_hbm, out_hbm, idx_v, buf0, buf1, gsem0, gsem1, wsem0, wsem1):
        wid = lax.axis_index("s") * SC_CORES + lax.axis_index("c")
        base_chunk = wid * chunks_per_worker

        def gather(j, buf, sem):
            return pltpu.make_async_copy(table_hbm.at[idx_v.at[j]], buf, sem)

        def write(chunk, buf, sem):
            rows = pl.ds(pl.multiple_of(chunk * SC_CHUNK, SC_CHUNK), SC_CHUNK)
            return pltpu.make_async_copy(buf, out_hbm.at[rows], sem)

        @pl.loop(0, chunks_per_worker // cpb)
        def _(blk):
            c0 = base_chunk + blk * cpb
            pltpu.sync_copy(idx_hbm.at[pl.ds(pl.multiple_of(c0, cpb), cpb)], idx_v)
            gather(0, buf0, gsem0).start()

            @pl.loop(0, cpb // 2)
            def _(p):
                j = p * 2
                gather(j, buf0, gsem0).wait()
                write(c0 + j, buf0, wsem0).start()

                @pl.when(p > 0)
                def _():
                    write(c0 + j - 1, buf1, wsem1).wait()

                gather(j + 1, buf1, gsem1).start()
                gather(j + 1, buf1, gsem1).wait()
                write(c0 + j + 1, buf1, wsem1).start()
                write(c0 + j, buf0, wsem0).wait()

                @pl.when(p < cpb // 2 - 1)
                def _():
                    gather(j + 2, buf0, gsem0).start()

            write(c0 + cpb - 1, buf1, wsem1).wait()

    return gather_kernel(table, idx.reshape(m // SC_CHUNK, SC_CHUNK))


def _pack_rows(w):
    b = lax.bitcast_convert_type(w.astype(jnp.bfloat16), jnp.uint16).astype(jnp.uint32)
    words = (b[:, :PEER_WORDS] << 16) | b[:, PEER_WORDS:]
    return lax.bitcast_convert_type(words, jnp.int32)


def _unpack_rows(words):
    hi = pltpu.bitcast(words & jnp.int32(-65536), jnp.float32)
    lo = pltpu.bitcast(words << 16, jnp.float32)
    return hi, lo


def _peer_combine_kernel(gu_ref, gv_ref, h2_ref, g_ref, x1_ref, y_ref):
    f32 = jnp.float32
    tb = h2_ref.shape[0]
    eye = (lax.broadcasted_iota(jnp.int32, (PEER_SLOTS, PEER_SLOTS), 0)
           == lax.broadcasted_iota(jnp.int32, (PEER_SLOTS, PEER_SLOTS), 1))

    def token(n, carry):
        rows = pl.ds(pl.multiple_of(n * PEER_SLOTS, PEER_SLOTS), PEER_SLOTS)
        x = h2_ref[pl.ds(n, 1), :]
        uh, ul = _unpack_rows(gu_ref[rows, :])
        d = jnp.sum(uh * x[:, 0:PEER_WORDS] + ul * x[:, PEER_WORDS:], axis=1, keepdims=True)
        gate = jnp.sum(jnp.where(eye, g_ref[pl.ds(n, 1), :], 0.0), axis=1, keepdims=True)
        w = gate * jax.nn.gelu(d)
        vh, vl = _unpack_rows(gv_ref[rows, :])
        x1 = x1_ref[pl.ds(n, 1), :]
        y_ref[pl.ds(n, 1), 0:PEER_WORDS] = x1[:, 0:PEER_WORDS] + jnp.sum(w * vh, axis=0, keepdims=True)
        y_ref[pl.ds(n, 1), PEER_WORDS:] = x1[:, PEER_WORDS:] + jnp.sum(w * vl, axis=0, keepdims=True)
        return carry

    lax.fori_loop(0, tb, token, 0)


def _peer_combine_call(gu, gv, h2, gates, x1, *, tb):
    n = h2.shape[0]
    assert n % tb == 0
    row = lambda i: (i, 0)
    return pl.pallas_call(
        _peer_combine_kernel,
        grid=(n // tb,),
        in_specs=[
            pl.BlockSpec((tb * PEER_SLOTS, PEER_WORDS), row),
            pl.BlockSpec((tb * PEER_SLOTS, PEER_WORDS), row),
            pl.BlockSpec((tb, D_MODEL), row),
            pl.BlockSpec((tb, PEER_SLOTS), row),
            pl.BlockSpec((tb, D_MODEL), row),
        ],
        out_specs=pl.BlockSpec((tb, D_MODEL), row),
        out_shape=jax.ShapeDtypeStruct((n, D_MODEL), jnp.float32),
        compiler_params=pltpu.CompilerParams(
            dimension_semantics=("arbitrary",), vmem_limit_bytes=V7X_VMEM_LIMIT),
        name="peer_combine",
    )(gu, gv, h2, gates, x1)


def _tail_pallas(x, a, g, w_out, norm2_g, w_query, subkeys, u_words, v_words):
    b, t, _ = x.shape
    n = b * t
    tm = _token_tile(n, (256, 128))
    x1, h2, st = _tail_call(x.reshape(n, D_MODEL), a, g, w_out, norm2_g, w_query, subkeys, tm=tm)
    eidx_t, gates_t = _peer_topk_call(st, tt=tm)
    eidx = eidx_t.T.reshape(n * PEER_SLOTS)
    gu = _sc_gather_rows(u_words, eidx)
    gv = _sc_gather_rows(v_words, eidx)
    y = _peer_combine_call(gu, gv, h2, gates_t.T, x1, tb=16)
    return y.reshape(b, t, D_MODEL)


def _compress(rows, pos, w):
    b, l = rows.shape[:2]
    nc = (l - CMP_BLOCK) // CMP_STRIDE + 1
    chunks = rows[:, :(nc + 1) * CMP_STRIDE].reshape(b, nc + 1, CMP_STRIDE, NSA_KV_HEADS, HEAD_DIM)
    first = jnp.einsum('bnphd,pde->bnhe', chunks, w[:CMP_STRIDE])
    second = jnp.einsum('bnphd,pde->bnhe', chunks, w[CMP_STRIDE:])
    bias = jnp.einsum('pd,pde->e', pos, w)
    return first[:, :-1] + second[:, 1:] + bias


def _nsa_keys(kv_all, cmp_pos, w_cmp, k_norm_g):
    kc = _rmsnorm(_compress(kv_all[:, :, 0], cmp_pos[0], w_cmp[0]), k_norm_g[0])
    vc = _compress(kv_all[:, :, 1], cmp_pos[1], w_cmp[1])
    return kc, vc, kv_all[:, :, 2], kv_all[:, :, 3]


def _nsa_block(q, t_pos, kc, vc, ks, vs, kw, vw, w_pos, gate):
    b, nq = q.shape[:2]
    nc = kc.shape[1]
    l = ks.shape[1]
    ns = -(-l // SEL_BLOCK)
    f32 = jnp.float32
    cmp_end = jnp.arange(nc) * CMP_STRIDE + (CMP_BLOCK - 1)
    m_c = cmp_end[None, :] <= t_pos[:, None]
    s_c = jnp.einsum('bqhgd,bnhd->bhgqn', q, kc).astype(f32)
    p_c = jax.nn.softmax(jnp.where(m_c, s_c, MASK_VALUE), axis=-1) * m_c
    o_c = jnp.einsum('bhgqn,bnhd->bqhgd', p_c.astype(vc.dtype), vc)
    ratio = SEL_BLOCK // CMP_STRIDE
    imp = jnp.pad(p_c.sum(axis=2), ((0, 0), (0, 0), (0, 0), (0, ns * ratio - nc)))
    imp = imp.reshape(b, NSA_KV_HEADS, nq, ns, ratio)
    imp = imp.sum(-1) + jnp.pad(imp[..., :-1, ratio - 1], ((0, 0), (0, 0), (0, 0), (1, 0)))
    blk = jnp.arange(ns)
    visible = blk[None, :] * SEL_BLOCK <= t_pos[:, None]
    forced = (blk[None, :] == 0) | (blk[None, :] == t_pos[:, None] // SEL_BLOCK)
    score = jnp.where(forced, FORCE_SCORE, jnp.where(visible, imp, -1.0))
    _, idx = lax.top_k(score, min(SEL_TOPK, ns))
    tok = idx[..., None] * SEL_BLOCK + jnp.arange(SEL_BLOCK)
    bi = jnp.arange(b)[:, None, None, None, None]
    hi = jnp.arange(NSA_KV_HEADS)[None, :, None, None, None]
    safe_tok = jnp.minimum(tok, l - 1)
    ks_g = ks[bi, safe_tok, hi]
    vs_g = vs[bi, safe_tok, hi]
    m_s = (tok <= t_pos[None, None, :, None, None])[:, :, None]
    s_s = jnp.einsum('bqhgd,bhqkpd->bhgqkp', q, ks_g).astype(f32)
    s_s = jnp.where(m_s, s_s, MASK_VALUE)
    p_s = jax.nn.softmax(s_s.reshape(s_s.shape[:4] + (-1,)), axis=-1).reshape(s_s.shape)
    o_s = jnp.einsum('bhgqkp,bhqkpd->bqhgd', p_s.astype(vs.dtype), vs_g)
    d = t_pos[:, None] - w_pos[None, :]
    m_w = (d >= 0) & (d <= WINDOW) & (w_pos[None, :] >= 0)
    s_w = jnp.einsum('bqhgd,bkhd->bhgqk', q, kw).astype(f32)
    p_w = jax.nn.softmax(jnp.where(m_w, s_w, MASK_VALUE), axis=-1)
    o_w = jnp.einsum('bhgqk,bkhd->bqhgd', p_w.astype(vw.dtype), vw)
    out = gate[..., 0:1] * o_c + gate[..., 1:2] * o_s + gate[..., 2:3] * o_w
    return out.reshape(b, nq, NSA_HEADS * HEAD_DIM).astype(q.dtype)


def _nsa_prompt(q, kv_rows, win_rows, gate, cmp_pos, w_cmp, k_norm_g):
    b, t = q.shape[:2]
    kc, vc, ks, vs = _nsa_keys(kv_rows, cmp_pos, w_cmp, k_norm_g)
    win = jnp.pad(win_rows, ((0, 0), (WINDOW, 0), (0, 0), (0, 0), (0, 0)))

    def one_block(blk):
        s = blk * Q_BLOCK
        qb = lax.dynamic_slice_in_dim(q, s, Q_BLOCK, axis=1)
        gb = lax.dynamic_slice_in_dim(gate, s, Q_BLOCK, axis=1)
        wb = lax.dynamic_slice_in_dim(win, s, WINDOW + Q_BLOCK, axis=1)
        t_pos = s + jnp.arange(Q_BLOCK)
        w_pos = s - WINDOW + jnp.arange(WINDOW + Q_BLOCK)
        return _nsa_block(qb, t_pos, kc, vc, ks, vs, wb[:, :, 0], wb[:, :, 1], w_pos, gb)

    out = lax.map(one_block, jnp.arange(t // Q_BLOCK))
    return out.transpose(1, 0, 2, 3).reshape(b, t, -1)


def _nsa_sample(q, kv_rows, win_rows, gate, cache_kv_l, cache_win, page_table, cmp_pos, w_cmp, k_norm_g):
    bd, t = q.shape[:2]
    past_len = page_table.shape[1] * PAGE_SIZE
    past = cache_kv_l[page_table].reshape(bd, past_len, KV_ROWS, NSA_KV_HEADS, HEAD_DIM)
    kc, vc, ks, vs = _nsa_keys(jnp.concatenate([past, kv_rows], axis=1), cmp_pos, w_cmp, k_norm_g)
    wbuf = cache_win.shape[1]
    win_all = jnp.concatenate([cache_win, win_rows], axis=1)
    t_pos = past_len + jnp.arange(t)
    w_pos = past_len - wbuf + jnp.arange(wbuf + t)
    out = _nsa_block(q, t_pos, kc, vc, ks, vs, win_all[:, :, 0], win_all[:, :, 1], w_pos, gate)
    return out, win_all[:, -min(WINDOW, past_len + t):]


def _gla_chunked(q, k, v, log_a, s0):
    b, t, h = q.shape[:3]
    c = math.gcd(t, GLA_CHUNK)
    n = t // c

    def to_chunks(a):
        return a.reshape(b, n, c, h, a.shape[-1]).transpose(1, 0, 3, 2, 4)

    causal = jnp.tril(jnp.ones((c, c), dtype=bool))

    def step(S, inp):
        qc, kc, vc, ac = inp
        cum = jnp.cumsum(ac, axis=2)
        diff = jnp.minimum(cum[:, :, :, None] - cum[:, :, None, :], 0.0)
        decay = jnp.where(causal[..., None], jnp.exp(diff), 0.0)
        att = jnp.einsum('bhid,bhjd,bhijd->bhij', qc, kc, decay)
        o = jnp.einsum('bhij,bhjv->bhiv', att, vc) + jnp.einsum('bhid,bhdv->bhiv', qc * jnp.exp(cum), S)
        last = cum[:, :, -1:]
        S = jnp.exp(last)[:, :, 0, :, None] * S + jnp.einsum('bhjd,bhjv->bhdv', kc * jnp.exp(last - cum), vc)
        return S, o

    S, o = lax.scan(step, s0, (to_chunks(q), to_chunks(k), to_chunks(v), to_chunks(log_a)))
    return o.transpose(1, 0, 3, 2, 4).reshape(b, t, h, -1), S


def _gla_mixer(gq, gk, gv, glr, gog, s0, w_gate, b_gate, norm_g):
    b, t = gq.shape[:2]
    f32 = jnp.float32
    q = gq.reshape(b, t, GLA_HEADS, GLA_DK).astype(f32) * (GLA_DK ** -0.5)
    k = gk.reshape(b, t, GLA_HEADS, GLA_DK).astype(f32)
    v = gv.reshape(b, t, GLA_HEADS, GLA_DV).astype(f32)
    log_a = jax.nn.log_sigmoid((glr @ w_gate + b_gate).astype(f32)).reshape(b, t, GLA_HEADS, GLA_DK) / GLA_GATE_TEMP
    o, S = _gla_chunked(q, k, v, log_a, s0.astype(f32))
    o = _rmsnorm(o, norm_g) * jax.nn.silu(gog.astype(f32)).reshape(b, t, GLA_HEADS, GLA_DV)
    return o.reshape(b, t, -1).astype(gq.dtype), S


def _peer_ffn(h, w_query, subkeys, u, v):
    b, t, d = h.shape
    n = b * t
    nb = -(-n // PEER_TOKEN_BLOCK)
    flat = jnp.pad(h.reshape(n, d), ((0, nb * PEER_TOKEN_BLOCK - n), (0, 0))).reshape(nb, PEER_TOKEN_BLOCK, d)

    def one_block(xb):
        qh = (xb @ w_query).reshape(-1, PEER_HEADS, 2, PEER_KEY_DIM // 2)
        s = jnp.einsum('nhcd,hckd->nhck', qh, subkeys).astype(jnp.float32)
        s1, i1 = lax.top_k(s[:, :, 0], PEER_TOPK)
        s2, i2 = lax.top_k(s[:, :, 1], PEER_TOPK)
        cand = (s1[..., :, None] + s2[..., None, :]).reshape(s1.shape[:-1] + (-1,))
        cidx = (i1[..., :, None] * PEER_NKEYS + i2[..., None, :]).reshape(i1.shape[:-1] + (-1,))
        top, pos = lax.top_k(cand, PEER_TOPK)
        eidx = jnp.take_along_axis(cidx, pos, axis=-1)
        g = jax.nn.softmax(top, axis=-1)
        act = jax.nn.gelu(jnp.einsum('nhkd,nd->nhk', jnp.take(u, eidx, axis=0), xb).astype(jnp.float32))
        return jnp.einsum('nhk,nhkd->nd', (g * act).astype(xb.dtype), jnp.take(v, eidx, axis=0))

    out = lax.map(one_block, flat)
    return out.reshape(-1, d)[:n].reshape(b, t, d)


def _residual_tail(x, mix, w_out, norm2_g, w_query, subkeys, u, v):
    x = x + (mix @ w_out).astype(x.dtype)
    return x + _peer_ffn(_rmsnorm(x, norm2_g), w_query, subkeys, u, v).astype(x.dtype)


def _mixer_inputs(x, norm1_g, w_in_r, q_norm_g, k_norm_g, *, tm):
    b, t = x.shape[:2]
    q2d, kvr, winr, misc, gq, gk, gv, gog, kvb = _inproj(
        x.reshape(b * t, D_MODEL), norm1_g, w_in_r, q_norm_g, k_norm_g, tm=tm)
    q = q2d.astype(jnp.float32).reshape(b, t, NSA_KV_HEADS, NSA_GROUP, HEAD_DIM)
    kv_rows = kvr.reshape(b, t, 4, NSA_KV_HEADS, HEAD_DIM)
    win_rows = winr.reshape(b, t, 2, NSA_KV_HEADS, HEAD_DIM)
    gate = jax.nn.sigmoid(misc[:, :GATE_W]).reshape(b, t, NSA_KV_HEADS, NSA_GROUP, 3)
    glr = misc[:, GATE_W:GATE_W + GLA_GATE_RANK].reshape(b, t, -1)
    rs = lambda a: a.reshape(b, t, -1)
    return q, kv_rows, win_rows, gate, (rs(gq), rs(gk), rs(gv), glr, rs(gog)), (q2d, misc, kvr, kvb)


def _token_tile(n, candidates=(512, 256, 128, 64, 32, 16, 8)):
    for tm in candidates:
        if n % tm == 0:
            return tm
    raise ValueError(n)


def kernel(x_prompt, x_sample, cache_kv, cache_win, state_gla, page_table, norm1_g, w_in, q_norm_g, k_norm_g, cmp_pos, w_cmp, gla_w_gate, gla_b_gate, gla_norm_g, w_out, norm2_g, peer_w_query, peer_subkeys, peer_u, peer_v):
    depth = w_in.shape[0]
    xp, xs = x_prompt, x_sample
    kv_p, win_p, gla_p, kv_s, win_s, gla_s = [], [], [], [], [], []
    for l in range(depth):
        w_in_r = _reorder_w_in(w_in[l])
        tmp = _token_tile(xp.shape[0] * xp.shape[1])
        q, kv_rows, win_rows, gate, gla_in, (q2d, misc, kvr, kvb) = _mixer_inputs(
            xp, norm1_g[l], w_in_r, q_norm_g[l], k_norm_g[l], tm=tmp)
        a = _nsa_prompt_pallas(q2d, misc, kvr, kvb, cmp_pos[l], w_cmp[l], k_norm_g[l], xp.shape[0], xp.shape[1])
        s0 = jnp.zeros((xp.shape[0], GLA_HEADS, GLA_DK, GLA_DV), jnp.float32)
        g, s_new = _gla_mixer(*gla_in, s0, gla_w_gate[l], gla_b_gate[l], gla_norm_g[l])
        kv_p.append(kv_rows)
        win_p.append(win_rows[:, -min(WINDOW, xp.shape[1]):])
        gla_p.append(s_new.astype(state_gla.dtype))
        u_words, v_words = _pack_rows(peer_u[l]), _pack_rows(peer_v[l])
        xp = _tail_pallas(xp, a, g.reshape(-1, GV_W).astype(jnp.bfloat16), w_out[l], norm2_g[l],
                          peer_w_query[l], peer_subkeys[l], u_words, v_words)
        tms = _token_tile(xs.shape[0] * xs.shape[1])
        q, kv_rows, win_rows, gate, gla_in, _ = _mixer_inputs(xs, norm1_g[l], w_in_r, q_norm_g[l], k_norm_g[l], tm=tms)
        a, win_new = _nsa_sample(q, kv_rows, win_rows, gate, cache_kv[l], cache_win[l], page_table,
                                 cmp_pos[l], w_cmp[l], k_norm_g[l])
        g, s_new = _gla_mixer(*gla_in, state_gla[l], gla_w_gate[l], gla_b_gate[l], gla_norm_g[l])
        kv_s.append(kv_rows)
        win_s.append(win_new)
        gla_s.append(s_new.astype(state_gla.dtype))
        xs = _tail_pallas(xs, a.reshape(-1, Q_W).astype(jnp.bfloat16), g.reshape(-1, GV_W).astype(jnp.bfloat16),
                          w_out[l], norm2_g[l], peer_w_query[l], peer_subkeys[l], u_words, v_words)
    return (xp, xs, jnp.stack(kv_p), jnp.stack(win_p), jnp.stack(gla_p),
            jnp.stack(kv_s), jnp.stack(win_s), jnp.stack(gla_s))
```

```python
import functools
import math

import jax
import jax.numpy as jnp
import numpy as np
from jax import lax
from jax.experimental import pallas as pl
from jax.experimental.pallas import tpu as pltpu
from jax.experimental.pallas import tpu_sc as plsc

D_MODEL = 1024
NSA_HEADS = 8
NSA_KV_HEADS = 2
NSA_GROUP = NSA_HEADS // NSA_KV_HEADS
HEAD_DIM = 64
CMP_STRIDE = 16
CMP_BLOCK = 32
SEL_BLOCK = 64
SEL_TOPK = 16
WINDOW = 512
Q_BLOCK = 128
PAGE_SIZE = 128
GLA_HEADS = 4
GLA_DV = 128
GLA_DK = 64
GLA_GATE_RANK = 16
GLA_GATE_TEMP = 16.0
GLA_CHUNK = 64
PEER_HEADS = 8
PEER_NKEYS = 128
PEER_KEY_DIM = 256
PEER_TOPK = 16
PEER_TOKEN_BLOCK = 128
KV_ROWS = 4
RMS_EPS = 1e-6
MASK_VALUE = -1e30
FORCE_SCORE = 1e4

Q_W = NSA_HEADS * HEAD_DIM
KV_W = 6 * NSA_KV_HEADS * HEAD_DIM
GATE_W = 3 * NSA_HEADS
GQ_W = GLA_HEADS * GLA_DK
GV_W = GLA_HEADS * GLA_DV
MISC_W = 128
IN_SIZES = (Q_W, KV_W, GATE_W, GQ_W, GQ_W, GV_W, GLA_GATE_RANK, GV_W)
P_W = Q_W + KV_W + GQ_W + GQ_W + GV_W + GV_W + MISC_W

V7X_VMEM_LIMIT = 56 * 1024 * 1024


def _rmsnorm(x, g):
    xf = x.astype(jnp.float32)
    y = xf * lax.rsqrt(jnp.mean(xf * xf, axis=-1, keepdims=True) + RMS_EPS)
    return (y * g.astype(jnp.float32)).astype(x.dtype)


def _head_group_ones(width, group, dtype):
    r = lax.broadcasted_iota(jnp.int32, (width, width), 0) // group
    c = lax.broadcasted_iota(jnp.int32, (width, width), 1) // group
    return jnp.where(r == c, 1.0, 0.0).astype(dtype)


def _group_mean_sq(x, group):
    sq = x * x
    hi = sq.astype(jnp.bfloat16)
    lo = (sq - hi.astype(jnp.float32)).astype(jnp.bfloat16)
    ones = _head_group_ones(x.shape[-1], group, jnp.bfloat16)
    s = jnp.dot(hi, ones, preferred_element_type=jnp.float32)
    s = s + jnp.dot(lo, ones, preferred_element_type=jnp.float32)
    return s * (1.0 / group)


def _inproj_kernel(x_ref, g1_ref, w_ref, qg_ref, ksg_ref, kwg_ref,
                   q_ref, kv_ref, win_ref, misc_ref, gq_ref, gk_ref, gv_ref, gog_ref, kvb_ref):
    x = x_ref[...]
    h = x * lax.rsqrt(jnp.mean(x * x, axis=-1, keepdims=True) + RMS_EPS) * g1_ref[...]
    p = jnp.dot(h.astype(jnp.bfloat16), w_ref[...], preferred_element_type=jnp.float32)
    o = 0
    q = p[:, o:o + Q_W]; o += Q_W
    kv = p[:, o:o + KV_W]; o += KV_W
    gq_ref[...] = p[:, o:o + GQ_W]; o += GQ_W
    gk_ref[...] = p[:, o:o + GQ_W]; o += GQ_W
    gv_ref[...] = p[:, o:o + GV_W]; o += GV_W
    gog_ref[...] = p[:, o:o + GV_W]; o += GV_W
    misc_ref[...] = p[:, o:o + MISC_W]
    qn = q * lax.rsqrt(_group_mean_sq(q, HEAD_DIM) + RMS_EPS) * qg_ref[...] * (HEAD_DIM ** -0.5)
    q_ref[...] = qn.astype(q_ref.dtype)
    hw = NSA_KV_HEADS * HEAD_DIM
    k_sel = kv[:, 2 * hw:3 * hw]
    k_sel = k_sel * lax.rsqrt(_group_mean_sq(k_sel, HEAD_DIM) + RMS_EPS) * ksg_ref[...]
    k_win = kv[:, 4 * hw:5 * hw]
    k_win = k_win * lax.rsqrt(_group_mean_sq(k_win, HEAD_DIM) + RMS_EPS) * kwg_ref[...]
    kv_ref[:, 0:2 * hw] = kv[:, 0:2 * hw]
    kv_ref[:, 2 * hw:3 * hw] = k_sel
    kv_ref[:, 3 * hw:4 * hw] = kv[:, 3 * hw:4 * hw]
    win_ref[:, 0:hw] = k_win
    win_ref[:, hw:2 * hw] = kv[:, 5 * hw:6 * hw]
    kvb_ref[:, 0:hw] = k_sel.astype(kvb_ref.dtype)
    kvb_ref[:, hw:2 * hw] = kv[:, 3 * hw:4 * hw].astype(kvb_ref.dtype)
    kvb_ref[:, 2 * hw:3 * hw] = k_win.astype(kvb_ref.dtype)
    kvb_ref[:, 3 * hw:4 * hw] = kv[:, 5 * hw:6 * hw].astype(kvb_ref.dtype)


def _reorder_w_in(w_in):
    offs = np.cumsum((0,) + IN_SIZES)
    q, kv, gate, gq, gk, gv, glr, gog = [w_in[:, offs[i]:offs[i + 1]] for i in range(8)]
    pad = jnp.zeros((w_in.shape[0], MISC_W - GATE_W - GLA_GATE_RANK), w_in.dtype)
    return jnp.concatenate([q, kv, gq, gk, gv, gog, gate, glr, pad], axis=1).astype(jnp.bfloat16)


def _inproj(x2d, norm1_g, w_in_r, q_norm_g, k_norm_g, *, tm):
    n = x2d.shape[0]
    assert n % tm == 0
    hw = NSA_KV_HEADS * HEAD_DIM
    f32 = jnp.float32
    row = lambda i: (i, 0)
    const = lambda i: (0, 0)
    widths = (Q_W, 4 * hw, 2 * hw, MISC_W, GQ_W, GQ_W, GV_W, GV_W, 4 * hw)
    bf16 = jnp.bfloat16
    dtypes = (bf16, f32, f32, f32, f32, f32, f32, f32, bf16)
    return pl.pallas_call(
        _inproj_kernel,
        grid=(n // tm,),
        in_specs=[
            pl.BlockSpec((tm, D_MODEL), row),
            pl.BlockSpec((1, D_MODEL), const),
            pl.BlockSpec((D_MODEL, P_W), const),
            pl.BlockSpec((1, Q_W), const),
            pl.BlockSpec((1, hw), const),
            pl.BlockSpec((1, hw), const),
        ],
        out_specs=[pl.BlockSpec((tm, w), row) for w in widths],
        out_shape=[jax.ShapeDtypeStruct((n, w), dt) for w, dt in zip(widths, dtypes)],
        compiler_params=pltpu.CompilerParams(
            dimension_semantics=("arbitrary",), vmem_limit_bytes=V7X_VMEM_LIMIT),
        name="inproj",
    )(x2d, norm1_g.reshape(1, -1), w_in_r,
      jnp.tile(q_norm_g, NSA_HEADS).reshape(1, -1),
      jnp.tile(k_norm_g[1], NSA_KV_HEADS).reshape(1, -1),
      jnp.tile(k_norm_g[2], NSA_KV_HEADS).reshape(1, -1))


CMP_LANES = 2 * NSA_KV_HEADS * HEAD_DIM
ROW_LANES = KV_ROWS * NSA_KV_HEADS * HEAD_DIM


def _chunk_map(x, w_ref):
    acc = None
    for p in range(CMP_STRIDE):
        xp = x[:, p * ROW_LANES:p * ROW_LANES + CMP_LANES].astype(jnp.bfloat16)
        d = jnp.dot(xp, w_ref[p], preferred_element_type=jnp.float32)
        acc = d if acc is None else acc + d
    return acc


def _compress_kernel(x_ref, xn_ref, pos_ref, wf_ref, ws_ref, kg_ref, kc_ref, vc_ref):
    tn = x_ref.shape[1]
    x = x_ref[0]
    first = _chunk_map(x, wf_ref)
    second = _chunk_map(x, ws_ref)
    second_next = _chunk_map(xn_ref[0], ws_ref)
    bias = _chunk_map(pos_ref[0], wf_ref) + _chunk_map(pos_ref[1], ws_ref)
    rows = lax.broadcasted_iota(jnp.int32, second.shape, 0)
    shifted = jnp.where(rows == tn - 1, second_next[0:1, :], pltpu.roll(second, tn - 1, axis=0))
    out = first + shifted + bias[0:1, :]
    hw = NSA_KV_HEADS * HEAD_DIM
    kc = out[:, 0:hw]
    kc_ref[0] = kc * lax.rsqrt(_group_mean_sq(kc, HEAD_DIM) + RMS_EPS) * kg_ref[...]
    vc_ref[0] = out[:, hw:2 * hw]


def _compress_weights(w_cmp, cmp_pos):
    eye = jnp.eye(NSA_KV_HEADS, dtype=w_cmp.dtype)

    def bd(p):
        blocks = [jnp.kron(eye, w_cmp[r, p]) for r in range(2)]
        z = jnp.zeros_like(blocks[0])
        return jnp.concatenate([jnp.concatenate([blocks[0], z], 1), jnp.concatenate([z, blocks[1]], 1)], 0)

    wf = jnp.stack([bd(p) for p in range(CMP_STRIDE)]).astype(jnp.bfloat16)
    ws = jnp.stack([bd(p + CMP_STRIDE) for p in range(CMP_STRIDE)]).astype(jnp.bfloat16)

    def pos_rows(lo):
        pk = jnp.tile(cmp_pos[0, lo:lo + CMP_STRIDE], (1, NSA_KV_HEADS))
        pv = jnp.tile(cmp_pos[1, lo:lo + CMP_STRIDE], (1, NSA_KV_HEADS))
        row = jnp.concatenate([pk, pv, jnp.zeros_like(pk), jnp.zeros_like(pv)], axis=1)
        flat = row.reshape(1, CMP_STRIDE * ROW_LANES)
        return jnp.concatenate([flat, jnp.zeros((7, flat.shape[1]), flat.dtype)], axis=0)

    pos = jnp.stack([pos_rows(0), pos_rows(CMP_STRIDE)])
    return wf, ws, pos


def _compress_call(kv_chunks, wf, ws, pos, k_norm0, *, tn):
    b, nch, width = kv_chunks.shape
    assert nch % tn == 0 and tn % 8 == 0
    hw = NSA_KV_HEADS * HEAD_DIM
    last8 = nch // 8 - 1
    return pl.pallas_call(
        _compress_kernel,
        grid=(b, nch // tn),
        in_specs=[
            pl.BlockSpec((1, tn, width), lambda i, j: (i, j, 0)),
            pl.BlockSpec((1, 8, width), lambda i, j: (i, jnp.minimum((j + 1) * (tn // 8), last8), 0)),
            pl.BlockSpec((2, 8, width), lambda i, j: (0, 0, 0)),
            pl.BlockSpec((CMP_STRIDE, CMP_LANES, CMP_LANES), lambda i, j: (0, 0, 0)),
            pl.BlockSpec((CMP_STRIDE, CMP_LANES, CMP_LANES), lambda i, j: (0, 0, 0)),
            pl.BlockSpec((1, hw), lambda i, j: (0, 0)),
        ],
        out_specs=[pl.BlockSpec((1, tn, hw), lambda i, j: (i, j, 0))] * 2,
        out_shape=[jax.ShapeDtypeStruct((b, nch, hw), jnp.float32)] * 2,
        compiler_params=pltpu.CompilerParams(
            dimension_semantics=("arbitrary", "arbitrary"), vmem_limit_bytes=V7X_VMEM_LIMIT),
        name="compress",
    )(kv_chunks, kv_chunks, pos, wf, ws, jnp.tile(k_norm0, NSA_KV_HEADS).reshape(1, -1))


PROMPT_SPLITS = 4
SEL_TILE = 1024
SEL_PER_TILE = SEL_TILE // SEL_BLOCK
WIN_KEYS = WINDOW + Q_BLOCK
WIN_BLOCKS = WIN_KEYS // Q_BLOCK
ROW_CHUNK = 64
NEG_BIG = -3.0e38


def _lane_tile(x, reps):
    return jnp.concatenate([x] * reps, axis=1)


def _nsa_prompt_kernel(q_ref, misc_ref, kc_ref, vc_ref, ksvs_ref, w0, w1, w2, w3, w4,
                       psel_ref, pselt_ref, ebig_ref, o_ref,
                       q4_ref, s_ref, p_ref, bias_ref, bq_ref, psum_ref, m_ref, l_ref, al_ref,
                       kw_ref, vw_ref, oc_ref, os_ref, ow_ref, *, n_sel_blocks, blk0):
    i = pl.program_id(0) + blk0
    s0 = i * Q_BLOCK
    f32, bf16 = jnp.float32, jnp.bfloat16
    hw = NSA_KV_HEADS * HEAD_DIM
    n_chunks = NSA_GROUP * Q_BLOCK // ROW_CHUNK
    halves = Q_BLOCK // ROW_CHUNK

    qb = q_ref[...]
    for h in range(NSA_KV_HEADS):
        for g in range(NSA_GROUP):
            piece = jnp.dot(qb, psel_ref[h * NSA_GROUP + g], preferred_element_type=f32)
            q4_ref[h, g * Q_BLOCK:(g + 1) * Q_BLOCK, :] = piece.astype(bf16)

    for j, w in enumerate((w0, w1, w2, w3, w4)):
        kw_ref[j * Q_BLOCK:(j + 1) * Q_BLOCK, :] = w[:, 0:hw]
        vw_ref[j * Q_BLOCK:(j + 1) * Q_BLOCK, :] = w[:, hw:2 * hw]

    def chunk_rows(c):
        return pl.ds(pl.multiple_of(c * ROW_CHUNK, ROW_CHUNK), ROW_CHUNK)

    def chunk_t(c):
        r = lax.broadcasted_iota(jnp.int32, (ROW_CHUNK, 1), 0)
        return s0 + (c % halves) * ROW_CHUNK + r

    for h in range(NSA_KV_HEADS):
        q4 = q4_ref[h]

        ncp = kc_ref.shape[1]
        nsb = ncp // 4
        s_ref[:, 0:ncp] = lax.dot_general(q4, kc_ref[0], (((1,), (1,)), ((), ())), preferred_element_type=f32)
        psum_ref[...] = jnp.zeros_like(psum_ref)

        def cmp_chunk(c, carry):
            rows = chunk_rows(c)
            t = chunk_t(c)
            col = lax.broadcasted_iota(jnp.int32, (ROW_CHUNK, ncp), 1)
            cidx = (col % nsb) * 4 + col // nsb
            valid = cidx * CMP_STRIDE + (CMP_BLOCK - 1) <= t
            s = jnp.where(valid, s_ref[rows, 0:ncp], MASK_VALUE)
            mx = jnp.max(s, axis=1, keepdims=True)
            e = jnp.exp(s - mx)
            p = jnp.where(valid, e / jnp.sum(e, axis=1, keepdims=True), 0.0)
            p_ref[rows, 0:ncp] = p.astype(bf16)
            hrows = pl.ds(pl.multiple_of((c % halves) * ROW_CHUNK, ROW_CHUNK), ROW_CHUNK)
            psum_ref[hrows, :] += p
            return carry

        lax.fori_loop(0, n_chunks, cmp_chunk, 0)
        oc_ref[h] = jnp.dot(p_ref[:, 0:ncp], vc_ref[0], preferred_element_type=f32)

        ps = psum_ref[...]
        a3 = ps[:, 3 * nsb:4 * nsb]
        blk = lax.broadcasted_iota(jnp.int32, (Q_BLOCK, nsb), 1)
        tq = s0 + lax.broadcasted_iota(jnp.int32, (Q_BLOCK, nsb), 0)
        imp = ps[:, 0:nsb] + ps[:, nsb:2 * nsb] + ps[:, 2 * nsb:3 * nsb] + a3
        imp = imp + jnp.where(blk == 0, 0.0, pltpu.roll(a3, 1, axis=1))
        visible = blk * SEL_BLOCK <= tq
        forced = (blk == 0) | (blk == tq // SEL_BLOCK)
        score = jnp.where(forced, FORCE_SCORE, jnp.where(visible, imp, -1.0))
        blkf = blk.astype(f32)

        def pick(_, carry):
            sc, selm = carry
            mx = jnp.max(sc, axis=1, keepdims=True)
            first = jnp.min(jnp.where(sc == mx, blkf, float(nsb)), axis=1, keepdims=True)
            hit = blkf == first
            return jnp.where(hit, NEG_BIG, sc), jnp.where(hit, 1.0, selm)

        _, selm = lax.fori_loop(0, min(SEL_TOPK, n_sel_blocks), pick, (score, jnp.zeros_like(score)))
        bq_ref[...] = jnp.where(selm > 0.0, 0.0, MASK_VALUE).astype(bf16)

        m_ref[...] = jnp.full_like(m_ref, NEG_BIG)
        l_ref[...] = jnp.zeros_like(l_ref)
        os_ref[h] = jnp.zeros((NSA_GROUP * Q_BLOCK, hw), f32)

        def sel_tile(kt, carry):
            k0 = pl.multiple_of(kt * SEL_TILE, SEL_TILE)
            e_off = pl.multiple_of(nsb - kt * SEL_PER_TILE, SEL_PER_TILE)
            bias_ref[...] = jnp.dot(bq_ref[...], ebig_ref[pl.ds(e_off, nsb), :], preferred_element_type=f32)
            s_ref[...] = lax.dot_general(q4, ksvs_ref[pl.ds(k0, SEL_TILE), 0:hw],
                                         (((1,), (1,)), ((), ())), preferred_element_type=f32)

            def sel_chunk(c, carry2):
                rows = chunk_rows(c)
                hrows = pl.ds(pl.multiple_of((c % halves) * ROW_CHUNK, ROW_CHUNK), ROW_CHUNK)
                t = chunk_t(c)
                key = k0 + lax.broadcasted_iota(jnp.int32, (ROW_CHUNK, SEL_TILE), 1)
                s = jnp.where(key <= t, s_ref[rows, :] + bias_ref[hrows, :], MASK_VALUE)
                m_old = m_ref[rows, :]
                m_new = jnp.maximum(m_old, jnp.max(s, axis=1, keepdims=True))
                p = jnp.exp(s - _lane_tile(m_new, SEL_TILE // 128))
                alpha = jnp.exp(m_old - m_new)
                l_ref[rows, :] = alpha * l_ref[rows, :] + jnp.sum(p, axis=1, keepdims=True)
                m_ref[rows, :] = m_new
                al_ref[rows, :] = alpha
                p_ref[rows, :] = p.astype(bf16)
                return carry2

            lax.fori_loop(0, n_chunks, sel_chunk, 0)
            pv = jnp.dot(p_ref[...], ksvs_ref[pl.ds(k0, SEL_TILE), hw:2 * hw], preferred_element_type=f32)
            os_ref[h] = os_ref[h] * al_ref[...] + pv
            return carry

        lax.fori_loop(0, (s0 + Q_BLOCK - 1) // SEL_TILE + 1, sel_tile, 0)
        os_ref[h] = os_ref[h] / l_ref[...]

        s_ref[:, 0:WIN_KEYS] = lax.dot_general(q4, kw_ref[...], (((1,), (1,)), ((), ())),
                                               preferred_element_type=f32)

        def win_chunk(c, carry):
            rows = chunk_rows(c)
            t = chunk_t(c)
            pos = s0 - WINDOW + lax.broadcasted_iota(jnp.int32, (ROW_CHUNK, WIN_KEYS), 1)
            d = t - pos
            valid = (d >= 0) & (d <= WINDOW) & (pos >= 0)
            s = jnp.where(valid, s_ref[rows, 0:WIN_KEYS], MASK_VALUE)
            mx = jnp.max(s, axis=1, keepdims=True)
            e = jnp.exp(s - mx)
            p_ref[rows, 0:WIN_KEYS] = (e / jnp.sum(e, axis=1, keepdims=True)).astype(bf16)
            return carry

        lax.fori_loop(0, n_chunks, win_chunk, 0)
        ow_ref[h] = jnp.dot(p_ref[:, 0:WIN_KEYS], vw_ref[...], preferred_element_type=f32)

    gsig = jax.nn.sigmoid(misc_ref[...])
    out = jnp.zeros((Q_BLOCK, Q_W), f32)
    for h in range(NSA_KV_HEADS):
        for g in range(NSA_GROUP):
            hg = h * NSA_GROUP + g
            r = slice(g * Q_BLOCK, (g + 1) * Q_BLOCK)
            mix = (gsig[:, 3 * hg:3 * hg + 1] * oc_ref[h, r, :]
                   + gsig[:, 3 * hg + 1:3 * hg + 2] * os_ref[h, r, :]
                   + gsig[:, 3 * hg + 2:3 * hg + 3] * ow_ref[h, r, :])
            out = out + jnp.dot(mix.astype(bf16), pselt_ref[hg], preferred_element_type=f32)
    o_ref[...] = out.astype(o_ref.dtype)


def _nsa_constants(n_sel_blocks):
    hw = NSA_KV_HEADS * HEAD_DIM
    psel = np.zeros((NSA_HEADS, Q_W, hw), np.float32)
    for h in range(NSA_KV_HEADS):
        for g in range(NSA_GROUP):
            hg = h * NSA_GROUP + g
            for d in range(HEAD_DIM):
                psel[hg, hg * HEAD_DIM + d, h * HEAD_DIM + d] = 1.0
    pselt = np.transpose(psel, (0, 2, 1))
    r = np.arange(2 * n_sel_blocks)[:, None] - n_sel_blocks
    ebig = (r == (np.arange(SEL_TILE)[None, :] // SEL_BLOCK)).astype(np.float32)
    return (jnp.asarray(psel, jnp.bfloat16), jnp.asarray(pselt, jnp.bfloat16), jnp.asarray(ebig, jnp.bfloat16))


def _nsa_prompt_call(q, misc, kcp, vcp, kvb, seq, b, blk0, nblk):
    assert seq % SEL_TILE == 0 and seq % Q_BLOCK == 0
    nb = seq // Q_BLOCK
    nsb = seq // SEL_BLOCK
    hw = NSA_KV_HEADS * HEAD_DIM
    psel, pselt, ebig = _nsa_constants(nsb)
    rows4 = NSA_GROUP * Q_BLOCK
    f32, bf16 = jnp.float32, jnp.bfloat16
    r0 = b * nb + blk0

    def win_spec(j):
        return pl.BlockSpec((Q_BLOCK, 2 * hw),
                            lambda i: (b * nb + jnp.maximum(blk0 + i - (WIN_BLOCKS - 1) + j, 0), 1))

    return pl.pallas_call(
        functools.partial(_nsa_prompt_kernel, n_sel_blocks=nsb, blk0=blk0),
        grid=(nblk,),
        in_specs=[
            pl.BlockSpec((Q_BLOCK, Q_W), lambda i: (r0 + i, 0)),
            pl.BlockSpec((Q_BLOCK, MISC_W), lambda i: (r0 + i, 0)),
            pl.BlockSpec((1, seq // CMP_STRIDE, hw), lambda i: (b, 0, 0)),
            pl.BlockSpec((1, seq // CMP_STRIDE, hw), lambda i: (b, 0, 0)),
            pl.BlockSpec((seq, 2 * hw), lambda i: (b, 0)),
        ] + [win_spec(j) for j in range(WIN_BLOCKS)] + [
            pl.BlockSpec(psel.shape, lambda i: (0, 0, 0)),
            pl.BlockSpec(pselt.shape, lambda i: (0, 0, 0)),
            pl.BlockSpec(ebig.shape, lambda i: (0, 0)),
        ],
        out_specs=pl.BlockSpec((Q_BLOCK, Q_W), lambda i: (i, 0)),
        out_shape=jax.ShapeDtypeStruct((nblk * Q_BLOCK, Q_W), bf16),
        scratch_shapes=[
            pltpu.VMEM((NSA_KV_HEADS, rows4, hw), bf16),
            pltpu.VMEM((rows4, SEL_TILE), f32),
            pltpu.VMEM((rows4, SEL_TILE), bf16),
            pltpu.VMEM((Q_BLOCK, SEL_TILE), f32),
            pltpu.VMEM((Q_BLOCK, nsb), bf16),
            pltpu.VMEM((Q_BLOCK, seq // CMP_STRIDE), f32),
            pltpu.VMEM((rows4, hw), f32),
            pltpu.VMEM((rows4, hw), f32),
            pltpu.VMEM((rows4, hw), f32),
            pltpu.VMEM((WIN_KEYS, hw), bf16),
            pltpu.VMEM((WIN_KEYS, hw), bf16),
            pltpu.VMEM((NSA_KV_HEADS, rows4, hw), f32),
            pltpu.VMEM((NSA_KV_HEADS, rows4, hw), f32),
            pltpu.VMEM((NSA_KV_HEADS, rows4, hw), f32),
        ],
        compiler_params=pltpu.CompilerParams(
            dimension_semantics=("arbitrary",), vmem_limit_bytes=V7X_VMEM_LIMIT),
        name="nsa_prompt",
    )(q, misc, kcp, vcp, kvb, kvb, kvb, kvb, kvb, kvb, psel, pselt, ebig)


def _prompt_compressed_kv(kvr2d, cmp_pos, w_cmp, k_norm_g, batch, seq):
    nch = seq // CMP_STRIDE
    wf, ws, pos = _compress_weights(w_cmp, cmp_pos)
    kc, vc = _compress_call(kvr2d.reshape(batch, nch, CMP_STRIDE * ROW_LANES), wf, ws, pos, k_norm_g[0],
                            tn=min(256, nch))

    def perm(a):
        return a.reshape(batch, nch // 4, 4, a.shape[-1]).transpose(0, 2, 1, 3).reshape(batch, nch, -1).astype(jnp.bfloat16)

    return perm(kc), perm(vc)


PAD_KEYS = 128


def _softmax_piece_max(pieces):
    m = None
    for s in pieces:
        pm = jnp.max(s, axis=1, keepdims=True)
        m = pm if m is None else jnp.maximum(m, pm)
    return m


def _nsa_sample_kernel(pt_ref, q_ref, misc_ref, kvb_ref, win_ref, cache_ref, wf_ref, ws_ref, pos_ref, kg_ref,
                       psel_ref, pselt_ref, ebig_ref, o_ref, pages_ref, sem_ref, *, past_len, n_new):
    f32, bf16 = jnp.float32, jnp.bfloat16
    b = pl.program_id(0)
    nseq = pl.num_programs(0)
    n_pages = past_len // PAGE_SIZE
    nsb = past_len // SEL_BLOCK
    ncp = past_len // CMP_STRIDE
    hw = NSA_KV_HEADS * HEAD_DIM
    slot = b % 2
    rows_q = NSA_GROUP * n_new

    def page_copy(seq, j, kind, s):
        return pltpu.make_async_copy(cache_ref.at[pt_ref[seq, j], :, pl.ds(kind * hw, hw)],
                                     pages_ref.at[s, kind, pl.ds(pl.multiple_of(j * PAGE_SIZE, PAGE_SIZE), PAGE_SIZE)],
                                     sem_ref.at[s])

    def for_each_page_copy(seq, s, fn):
        def body(j, c):
            for kind in range(KV_ROWS):
                fn(page_copy(seq, j, kind, s))
            return c
        lax.fori_loop(0, n_pages, body, 0)

    @pl.when(b == 0)
    def _():
        pages_ref[:, :, past_len:past_len + SEL_BLOCK, :] = jnp.zeros((2, KV_ROWS, SEL_BLOCK, hw), f32)
        for_each_page_copy(0, 0, lambda cp: cp.start())

    @pl.when(b + 1 < nseq)
    def _():
        for_each_page_copy(b + 1, 1 - slot, lambda cp: cp.start())

    for_each_page_copy(b, slot, lambda cp: cp.wait())

    def strided(start):
        rows = pl.ds(start, nsb, stride=SEL_BLOCK)
        return jnp.concatenate([pages_ref[slot, 0, rows, :], pages_ref[slot, 1, rows, :]], axis=1).astype(bf16)

    first = None
    second = None
    for p in range(CMP_STRIDE):
        xf = jnp.concatenate([strided(CMP_STRIDE * r + p) for r in range(4)], axis=0)
        xs = jnp.concatenate([strided(CMP_STRIDE * (r + 1) + p) for r in range(4)], axis=0)
        df = jnp.dot(xf, wf_ref[p], preferred_element_type=f32)
        ds_ = jnp.dot(xs, ws_ref[p], preferred_element_type=f32)
        first = df if first is None else first + df
        second = ds_ if second is None else second + ds_
    bias = _chunk_map(pos_ref[0], wf_ref) + _chunk_map(pos_ref[1], ws_ref)
    cmp_out = first + second + bias[0:1, :]
    kc = cmp_out[:, 0:hw]
    kc = (kc * lax.rsqrt(_group_mean_sq(kc, HEAD_DIM) + RMS_EPS) * kg_ref[...]).astype(bf16)
    vc = cmp_out[:, hw:2 * hw].astype(bf16)

    qb = q_ref[...]
    newkv = kvb_ref[...]
    zpad = jnp.zeros((PAD_KEYS - n_new, hw), bf16)
    ks_new = jnp.concatenate([newkv[:, 0:hw], zpad], axis=0)
    vs_new = jnp.concatenate([newkv[:, hw:2 * hw], zpad], axis=0)
    kw_new = jnp.concatenate([newkv[:, 2 * hw:3 * hw], zpad], axis=0)
    vw_new = jnp.concatenate([newkv[:, 3 * hw:4 * hw], zpad], axis=0)
    wcache = win_ref[0]
    wbuf = wcache.shape[0]
    kw_old = wcache[:, 0:hw].astype(bf16)
    vw_old = wcache[:, hw:2 * hw].astype(bf16)

    tl = lax.broadcasted_iota(jnp.int32, (rows_q, 1), 0) % n_new
    t_abs = past_len + tl
    new_col = lax.broadcasted_iota(jnp.int32, (rows_q, PAD_KEYS), 1)
    new_ok = new_col <= tl
    nt_dims = (((1,), (1,)), ((), ()))
    gsig = jax.nn.sigmoid(misc_ref[...])
    out = jnp.zeros((n_new, Q_W), f32)

    for h in range(NSA_KV_HEADS):
        q4 = jnp.concatenate(
            [jnp.dot(qb, psel_ref[h * NSA_GROUP + g], preferred_element_type=f32).astype(bf16)
             for g in range(NSA_GROUP)], axis=0)

        s = lax.dot_general(q4, kc, nt_dims, preferred_element_type=f32)
        col = lax.broadcasted_iota(jnp.int32, (rows_q, ncp), 1)
        cidx = (col % nsb) * 4 + col // nsb
        valid = cidx * CMP_STRIDE + (CMP_BLOCK - 1) <= t_abs
        s = jnp.where(valid, s, MASK_VALUE)
        e = jnp.exp(s - jnp.max(s, axis=1, keepdims=True))
        pc = jnp.where(valid, e / jnp.sum(e, axis=1, keepdims=True), 0.0)
        o_c = jnp.dot(pc.astype(bf16), vc, preferred_element_type=f32)
        psum = pc[0:n_new]
        for g in range(1, NSA_GROUP):
            psum = psum + pc[g * n_new:(g + 1) * n_new]

        a3 = psum[:, 3 * nsb:4 * nsb]
        blk = lax.broadcasted_iota(jnp.int32, (n_new, nsb), 1)
        imp = psum[:, 0:nsb] + psum[:, nsb:2 * nsb] + psum[:, 2 * nsb:3 * nsb] + a3
        imp = imp + jnp.where(blk == 0, 0.0, pltpu.roll(a3, 1, axis=1))
        score = jnp.where(blk == 0, FORCE_SCORE, imp)
        blkf = blk.astype(f32)
        selm = jnp.zeros_like(score)
        for _ in range(SEL_TOPK - 1):
            mx = jnp.max(score, axis=1, keepdims=True)
            firstb = jnp.min(jnp.where(score == mx, blkf, float(nsb)), axis=1, keepdims=True)
            hit = blkf == firstb
            selm = jnp.where(hit, 1.0, selm)
            score = jnp.where(hit, NEG_BIG, score)
        bq = jnp.where(selm > 0.0, 0.0, MASK_VALUE).astype(bf16)

        s_new = jnp.where(new_ok, lax.dot_general(q4, ks_new, nt_dims, preferred_element_type=f32), MASK_VALUE)
        m_run = jnp.max(s_new, axis=1, keepdims=True)
        p_new = jnp.exp(s_new - m_run)
        l_run = jnp.sum(p_new, axis=1, keepdims=True)
        acc = jnp.dot(p_new.astype(bf16), vs_new, preferred_element_type=f32)

        def sel_tile(kt, carry):
            m_run, l_run, acc = carry
            k0 = pl.multiple_of(kt * SEL_TILE, SEL_TILE)
            e_off = pl.multiple_of(nsb - kt * SEL_PER_TILE, SEL_PER_TILE)
            bias = jnp.dot(bq, ebig_ref[pl.ds(e_off, nsb), :], preferred_element_type=f32)
            kt_rows = pages_ref[slot, 2, pl.ds(k0, SEL_TILE), :].astype(bf16)
            vt_rows = pages_ref[slot, 3, pl.ds(k0, SEL_TILE), :].astype(bf16)
            s = lax.dot_general(q4, kt_rows, nt_dims, preferred_element_type=f32)
            s = s + jnp.concatenate([bias] * NSA_GROUP, axis=0)
            m_new = jnp.maximum(m_run, jnp.max(s, axis=1, keepdims=True))
            p = jnp.exp(s - m_new)
            alpha = jnp.exp(m_run - m_new)
            l_new = alpha * l_run + jnp.sum(p, axis=1, keepdims=True)
            acc = acc * alpha + jnp.dot(p.astype(bf16), vt_rows, preferred_element_type=f32)
            return m_new, l_new, acc

        m_run, l_run, acc = lax.fori_loop(0, past_len // SEL_TILE, sel_tile, (m_run, l_run, acc))
        o_s = acc / l_run

        wpos = past_len - wbuf + lax.broadcasted_iota(jnp.int32, (rows_q, wbuf), 1)
        d = t_abs - wpos
        ok_old = (d >= 0) & (d <= WINDOW) & (wpos >= 0)
        s_old = jnp.where(ok_old, lax.dot_general(q4, kw_old, nt_dims, preferred_element_type=f32), MASK_VALUE)
        s_nw = jnp.where(new_ok, lax.dot_general(q4, kw_new, nt_dims, preferred_element_type=f32), MASK_VALUE)
        mw = _softmax_piece_max([s_old, s_nw])
        e_old = jnp.exp(s_old - mw)
        e_nw = jnp.exp(s_nw - mw)
        lw = jnp.sum(e_old, axis=1, keepdims=True) + jnp.sum(e_nw, axis=1, keepdims=True)
        o_w = (jnp.dot((e_old / lw).astype(bf16), vw_old, preferred_element_type=f32)
               + jnp.dot((e_nw / lw).astype(bf16), vw_new, preferred_element_type=f32))

        for g in range(NSA_GROUP):
            hg = h * NSA_GROUP + g
            r = slice(g * n_new, (g + 1) * n_new)
            mix = (gsig[:, 3 * hg:3 * hg + 1] * o_c[r] + gsig[:, 3 * hg + 1:3 * hg + 2] * o_s[r]
                   + gsig[:, 3 * hg + 2:3 * hg + 3] * o_w[r])
            out = out + jnp.dot(mix.astype(bf16), pselt_ref[hg], preferred_element_type=f32)
    o_ref[...] = out.astype(o_ref.dtype)


def _nsa_sample_call(q, misc, kvb, cache_kv_l, cache_win_l, page_table, cmp_pos, w_cmp, k_norm_g, n_new):
    bsz, n_pages = page_table.shape
    past_len = n_pages * PAGE_SIZE
    assert past_len % SEL_TILE == 0 and n_new % 8 == 0 and n_new <= PAD_KEYS
    assert (past_len + n_new - CMP_BLOCK) // CMP_STRIDE + 1 == past_len // CMP_STRIDE - 1
    nsb = past_len // SEL_BLOCK
    hw = NSA_KV_HEADS * HEAD_DIM
    wbuf = cache_win_l.shape[1]
    psel, pselt, ebig = _nsa_constants(nsb)
    wf, ws, pos = _compress_weights(w_cmp, cmp_pos)
    cache = cache_kv_l.reshape(cache_kv_l.shape[0], PAGE_SIZE, ROW_LANES)
    win = cache_win_l.reshape(bsz, wbuf, 2 * hw)
    row = lambda i, pt: (i, 0)
    c2 = lambda i, pt: (0, 0)
    c3 = lambda i, pt: (0, 0, 0)
    grid_spec = pltpu.PrefetchScalarGridSpec(
        num_scalar_prefetch=1,
        grid=(bsz,),
        in_specs=[
            pl.BlockSpec((n_new, Q_W), row),
            pl.BlockSpec((n_new, MISC_W), row),
            pl.BlockSpec((n_new, 4 * hw), row),
            pl.BlockSpec((1, wbuf, 2 * hw), lambda i, pt: (i, 0, 0)),
            pl.BlockSpec(memory_space=pl.ANY),
            pl.BlockSpec(wf.shape, c3), pl.BlockSpec(ws.shape, c3), pl.BlockSpec(pos.shape, c3),
            pl.BlockSpec((1, hw), c2),
            pl.BlockSpec(psel.shape, c3), pl.BlockSpec(pselt.shape, c3), pl.BlockSpec(ebig.shape, c2),
        ],
        out_specs=pl.BlockSpec((n_new, Q_W), row),
        scratch_shapes=[pltpu.VMEM((2, KV_ROWS, past_len + SEL_BLOCK, hw), jnp.float32),
                        pltpu.SemaphoreType.DMA((2,))],
    )
    return pl.pallas_call(
        functools.partial(_nsa_sample_kernel, past_len=past_len, n_new=n_new),
        grid_spec=grid_spec,
        out_shape=jax.ShapeDtypeStruct((bsz * n_new, Q_W), jnp.bfloat16),
        compiler_params=pltpu.CompilerParams(
            dimension_semantics=("arbitrary",), vmem_limit_bytes=V7X_VMEM_LIMIT),
        name="nsa_sample",
    )(page_table, q, misc, kvb, win, cache, wf, ws, pos,
      jnp.tile(k_norm_g[0], NSA_KV_HEADS).reshape(1, -1), psel, pselt, ebig)


GLA_J_GROUP = 8


def _split3(x):
    hi = x.astype(jnp.bfloat16)
    r = x - hi.astype(jnp.float32)
    mid = r.astype(jnp.bfloat16)
    lo = (r - mid.astype(jnp.float32)).astype(jnp.bfloat16)
    return hi, mid, lo


def _gla_kernel(gq_ref, gk_ref, gv_ref, gog_ref, misc_ref, s0_ref, wg_ref, bg_ref, ng_ref,
                o_ref, sout_ref, sbd_ref, la_ref, cum_ref, *, chunk):
    f32, bf16 = jnp.float32, jnp.bfloat16
    tstep = pl.program_id(1)
    n_tsteps = pl.num_programs(1)
    tb = gq_ref.shape[0]
    c = chunk
    mm = bf16 if c % 16 == 0 else f32
    hk, hv = GQ_W, GV_W

    @pl.when(tstep == 0)
    def _():
        sbd_ref[...] = jnp.zeros_like(sbd_ref)
        for h in range(GLA_HEADS):
            sbd_ref[h * GLA_DK:(h + 1) * GLA_DK, h * GLA_DV:(h + 1) * GLA_DV] = s0_ref[0, h]

    z = jnp.dot(misc_ref[...].astype(bf16), wg_ref[...], preferred_element_type=f32) + bg_ref[...]
    la_ref[...] = (jnp.minimum(z, 0.0) - jnp.log1p(jnp.exp(-jnp.abs(z)))) * (1.0 / GLA_GATE_TEMP)

    ri = lax.broadcasted_iota(jnp.int32, (c, c), 0)
    ci = lax.broadcasted_iota(jnp.int32, (c, c), 1)
    tril = jnp.where(ri >= ci, 1.0, 0.0).astype(bf16)
    kr = lax.broadcasted_iota(jnp.int32, (hk, hk), 0) // GLA_DK
    kc = lax.broadcasted_iota(jnp.int32, (hk, hk), 1) // GLA_DK
    head_rep = jnp.where(kr == kc, 1.0, 0.0).astype(bf16)
    eye_k = (lax.broadcasted_iota(jnp.int32, (hk, hk), 0) == lax.broadcasted_iota(jnp.int32, (hk, hk), 1))
    bd_mask = (lax.broadcasted_iota(jnp.int32, (hk, hv), 0) // GLA_DK
               == lax.broadcasted_iota(jnp.int32, (hk, hv), 1) // GLA_DV)
    lane_j = lax.broadcasted_iota(jnp.int32, (c, hk), 1) % GLA_DK
    row_i = lax.broadcasted_iota(jnp.int32, (c, hk), 0)

    def one_chunk(ch, carry):
        rows = pl.ds(pl.multiple_of(ch * c, c), c)
        q = gq_ref[rows, :] * (GLA_DK ** -0.5)
        k = gk_ref[rows, :]
        v = gv_ref[rows, :]
        la = la_ref[rows, :]
        hi, mid, lo = _split3(la)
        cum = (jnp.dot(tril, hi, preferred_element_type=f32) + jnp.dot(tril, mid, preferred_element_type=f32)
               + jnp.dot(tril, lo, preferred_element_type=f32))
        last = cum[c - 1:c, :]
        cum_ref[...] = cum

        def j_group(g, att):
            ws = []
            for jj in range(GLA_J_GROUP):
                j = g * GLA_J_GROUP + jj
                jrow = pl.ds(ch * c + j, 1)
                kj = gk_ref[jrow, :]
                cumj = cum_ref[pl.ds(j, 1), :]
                dec = jnp.where(row_i >= j, jnp.exp(jnp.minimum(cum - cumj, 0.0)), 0.0)
                ws.append((q * kj * dec).astype(bf16))
            r = jnp.dot(jnp.concatenate(ws, axis=0), head_rep, preferred_element_type=f32)
            for jj in range(GLA_J_GROUP):
                j = g * GLA_J_GROUP + jj
                att = att + jnp.where(lane_j == j, r[jj * c:(jj + 1) * c, :], 0.0)
            return att

        att = lax.fori_loop(0, c // GLA_J_GROUP, j_group, jnp.zeros((c, hk), f32))

        vt = jnp.concatenate([v] * (GLA_DK // c), axis=0) if c < GLA_DK else v
        vbd = jnp.where(bd_mask, jnp.concatenate([vt] * GLA_HEADS, axis=0), 0.0)
        sbd = sbd_ref[...]
        o = jnp.dot(att.astype(bf16), vbd.astype(bf16), preferred_element_type=f32)
        o = o + jnp.dot((q * jnp.exp(cum)).astype(bf16), sbd.astype(bf16), preferred_element_type=f32)

        ke = k * jnp.exp(last - cum)
        upd = lax.dot_general(ke.astype(mm), v.astype(mm), (((0,), (0,)), ((), ())), preferred_element_type=f32)
        dcol = jnp.sum(jnp.where(eye_k, jnp.exp(last), 0.0), axis=1, keepdims=True)
        sbd_ref[...] = sbd * dcol + jnp.where(bd_mask, upd, 0.0)

        gog = gog_ref[rows, :]
        for h in range(GLA_HEADS):
            sl = slice(h * GLA_DV, (h + 1) * GLA_DV)
            oh = o[:, sl]
            oh = oh * lax.rsqrt(jnp.mean(oh * oh, axis=1, keepdims=True) + RMS_EPS) * ng_ref[...]
            gh = gog[:, sl]
            o_ref[rows, sl] = (oh * gh * jax.nn.sigmoid(gh)).astype(o_ref.dtype)
        return carry

    lax.fori_loop(0, tb // c, one_chunk, 0)

    @pl.when(tstep == n_tsteps - 1)
    def _():
        for h in range(GLA_HEADS):
            sout_ref[0, h] = sbd_ref[h * GLA_DK:(h + 1) * GLA_DK, h * GLA_DV:(h + 1) * GLA_DV]


def _gla_call(gq, gk, gv, gog, misc, s0, w_gate, b_gate, norm_g, batch, seq):
    c = math.gcd(seq, GLA_CHUNK)
    tb = _token_tile(seq, (512, 256, 128, 64, 32, 16, 8))
    tb = max(tb, c)
    nt = seq // tb
    f32, bf16 = jnp.float32, jnp.bfloat16
    wg = jnp.zeros((MISC_W, GQ_W), f32).at[GATE_W:GATE_W + GLA_GATE_RANK].set(w_gate).astype(bf16)
    row = lambda b, t: (b * nt + t, 0)
    const = lambda b, t: (0, 0)
    state_spec = pl.BlockSpec((1, GLA_HEADS, GLA_DK, GLA_DV), lambda b, t: (b, 0, 0, 0))
    return pl.pallas_call(
        functools.partial(_gla_kernel, chunk=c),
        grid=(batch, nt),
        in_specs=[
            pl.BlockSpec((tb, GQ_W), row), pl.BlockSpec((tb, GQ_W), row),
            pl.BlockSpec((tb, GV_W), row), pl.BlockSpec((tb, GV_W), row),
            pl.BlockSpec((tb, MISC_W), row), state_spec,
            pl.BlockSpec((MISC_W, GQ_W), const), pl.BlockSpec((1, GQ_W), const), pl.BlockSpec((1, GLA_DV), const),
        ],
        out_specs=[pl.BlockSpec((tb, GV_W), row), state_spec],
        out_shape=[jax.ShapeDtypeStruct((batch * seq, GV_W), bf16),
                   jax.ShapeDtypeStruct((batch, GLA_HEADS, GLA_DK, GLA_DV), f32)],
        scratch_shapes=[pltpu.VMEM((GQ_W, GV_W), f32), pltpu.VMEM((tb, GQ_W), f32), pltpu.VMEM((c, GQ_W), f32)],
        compiler_params=pltpu.CompilerParams(
            dimension_semantics=("arbitrary", "arbitrary"), vmem_limit_bytes=V7X_VMEM_LIMIT),
        name="gla",
    )(gq, gk, gv, gog, misc, s0, wg, b_gate.reshape(1, -1), norm_g.reshape(1, -1))


PEER_GROUPS = 2 * PEER_HEADS
PEER_HALF = PEER_KEY_DIM // 2
PEER_SLOTS = PEER_HEADS * PEER_TOPK
PEER_WORDS = D_MODEL // 2


def _tail_kernel(x_ref, a_ref, g_ref, wo_ref, n2_ref, wq_ref, sk_ref, x1_ref, h2_ref, st_ref):
    f32, bf16 = jnp.float32, jnp.bfloat16
    half = wo_ref.shape[0] // 2
    mix = jnp.dot(a_ref[...], wo_ref[0:half, :], preferred_element_type=f32)
    mix = mix + jnp.dot(g_ref[...], wo_ref[half:2 * half, :], preferred_element_type=f32)
    x1 = x_ref[...] + mix
    x1_ref[...] = x1
    h2 = x1 * lax.rsqrt(jnp.mean(x1 * x1, axis=-1, keepdims=True) + RMS_EPS) * n2_ref[...]
    h2_ref[...] = h2
    qh = jnp.dot(h2.astype(bf16), wq_ref[...], preferred_element_type=f32).astype(bf16)
    for c in range(PEER_GROUPS):
        st_ref[c] = lax.dot_general(sk_ref[c], qh[:, c * PEER_HALF:(c + 1) * PEER_HALF],
                                    (((1,), (1,)), ((), ())), preferred_element_type=f32)


def _tail_call(x2d, a, g, w_out, norm2_g, w_query, subkeys, *, tm, row0, n):
    assert n % tm == 0 and row0 % tm == 0
    f32, bf16 = jnp.float32, jnp.bfloat16
    row = lambda i: (i, 0)
    off = lambda i: (row0 // tm + i, 0)
    const = lambda i: (0, 0)
    sk = subkeys.reshape(PEER_GROUPS, PEER_NKEYS, PEER_HALF).astype(bf16)
    return pl.pallas_call(
        _tail_kernel,
        grid=(n // tm,),
        in_specs=[
            pl.BlockSpec((tm, D_MODEL), off),
            pl.BlockSpec((tm, Q_W), row),
            pl.BlockSpec((tm, GV_W), off),
            pl.BlockSpec((Q_W + GV_W, D_MODEL), const),
            pl.BlockSpec((1, D_MODEL), const),
            pl.BlockSpec((D_MODEL, PEER_HEADS * PEER_KEY_DIM), const),
            pl.BlockSpec((PEER_GROUPS, PEER_NKEYS, PEER_HALF), lambda i: (0, 0, 0)),
        ],
        out_specs=[pl.BlockSpec((tm, D_MODEL), row), pl.BlockSpec((tm, D_MODEL), row),
                   pl.BlockSpec((PEER_GROUPS, PEER_NKEYS, tm), lambda i: (0, 0, i))],
        out_shape=[jax.ShapeDtypeStruct((n, D_MODEL), f32), jax.ShapeDtypeStruct((n, D_MODEL), f32),
                   jax.ShapeDtypeStruct((PEER_GROUPS, PEER_NKEYS, n), f32)],
        compiler_params=pltpu.CompilerParams(
            dimension_semantics=("arbitrary",), vmem_limit_bytes=V7X_VMEM_LIMIT),
        name="tail_proj",
    )(x2d, a, g, w_out.astype(bf16), norm2_g.reshape(1, -1), w_query.astype(bf16), sk)


def _extract_topk(x, ids, k):
    r = x.shape[0]
    rows = lax.broadcasted_iota(jnp.int32, x.shape, 0).astype(jnp.float32)
    vals, picked = [], []
    for _ in range(k):
        mx = jnp.max(x, axis=0, keepdims=True)
        first = jnp.min(jnp.where(x == mx, rows, float(r)), axis=0, keepdims=True)
        hit = rows == first
        vals.append(mx)
        picked.append(jnp.sum(jnp.where(hit, ids, 0.0), axis=0, keepdims=True))
        x = jnp.where(hit, NEG_BIG, x)
    return vals, picked


def _peer_topk_kernel(st_ref, e_ref, g_ref):
    f32 = jnp.float32
    t = st_ref.shape[2]
    key_ids = lax.broadcasted_iota(jnp.int32, (PEER_NKEYS, t), 0).astype(f32)

    def head(h, carry):
        v1, i1 = _extract_topk(st_ref[2 * h], key_ids, PEER_TOPK)
        v2, i2 = _extract_topk(st_ref[2 * h + 1], key_ids, PEER_TOPK)
        s2 = jnp.concatenate(v2, axis=0)
        j2 = jnp.concatenate(i2, axis=0)
        cand = jnp.concatenate([v1[a] + s2 for a in range(PEER_TOPK)], axis=0)
        cidx = jnp.concatenate([i1[a] * float(PEER_NKEYS) + j2 for a in range(PEER_TOPK)], axis=0)
        top, eid = _extract_topk(cand, cidx, PEER_TOPK)
        top = jnp.concatenate(top, axis=0)
        e = jnp.exp(top - top[0:1, :])
        rows = pl.ds(pl.multiple_of(h * PEER_TOPK, PEER_TOPK), PEER_TOPK)
        g_ref[rows, :] = e / jnp.sum(e, axis=0, keepdims=True)
        e_ref[rows, :] = jnp.concatenate(eid, axis=0).astype(jnp.int32)
        return carry

    lax.fori_loop(0, PEER_HEADS, head, 0)


def _peer_topk_call(st, *, tt):
    n = st.shape[2]
    assert n % tt == 0
    return pl.pallas_call(
        _peer_topk_kernel,
        grid=(n // tt,),
        in_specs=[pl.BlockSpec((PEER_GROUPS, PEER_NKEYS, tt), lambda i: (0, 0, i))],
        out_specs=[pl.BlockSpec((PEER_SLOTS, tt), lambda i: (0, i))] * 2,
        out_shape=[jax.ShapeDtypeStruct((PEER_SLOTS, n), jnp.int32),
                   jax.ShapeDtypeStruct((PEER_SLOTS, n), jnp.float32)],
        compiler_params=pltpu.CompilerParams(
            dimension_semantics=("arbitrary",), vmem_limit_bytes=V7X_VMEM_LIMIT),
        name="peer_topk",
    )(st)


SC_CORES = 2
SC_SUBCORES = 16
SC_WORKERS = SC_CORES * SC_SUBCORES
SC_CHUNK = 64
SC_IDX_BLOCK = 2048
SC_CHUNKS_PER_BLOCK = SC_IDX_BLOCK // SC_CHUNK


def _sc_gather_rows(table, idx):
    m = idx.shape[0]
    width = table.shape[1]
    assert m % (SC_WORKERS * SC_IDX_BLOCK) == 0 and SC_CHUNKS_PER_BLOCK % 2 == 0
    chunks_per_worker = m // SC_WORKERS // SC_CHUNK
    cpb = SC_CHUNKS_PER_BLOCK
    mesh = plsc.VectorSubcoreMesh(core_axis_name="c", subcore_axis_name="s",
                                  num_cores=SC_CORES, num_subcores=SC_SUBCORES)

    @functools.partial(
        pl.kernel, mesh=mesh,
        out_type=jax.ShapeDtypeStruct((m, width), table.dtype),
        scratch_types=[pltpu.VMEM((cpb, SC_CHUNK), jnp.int32),
                       pltpu.VMEM((SC_CHUNK, width), table.dtype),
                       pltpu.VMEM((SC_CHUNK, width), table.dtype),
                       pltpu.SemaphoreType.DMA, pltpu.SemaphoreType.DMA,
                       pltpu.SemaphoreType.DMA, pltpu.SemaphoreType.DMA],
        name="peer_gather",
    )
    def gather_kernel(table_hbm, idx_hbm, out_hbm, idx_v, buf0, buf1, gsem0, gsem1, wsem0, wsem1):
        wid = lax.axis_index("s") * SC_CORES + lax.axis_index("c")
        base_chunk = wid * chunks_per_worker

        def gather(j, buf, sem):
            return pltpu.make_async_copy(table_hbm.at[idx_v.at[j]], buf, sem)

        def write(chunk, buf, sem):
            rows = pl.ds(pl.multiple_of(chunk * SC_CHUNK, SC_CHUNK), SC_CHUNK)
            return pltpu.make_async_copy(buf, out_hbm.at[rows], sem)

        @pl.loop(0, chunks_per_worker // cpb)
        def _(blk):
            c0 = base_chunk + blk * cpb
            pltpu.sync_copy(idx_hbm.at[pl.ds(pl.multiple_of(c0, cpb), cpb)], idx_v)
            gather(0, buf0, gsem0).start()

            @pl.loop(0, cpb // 2)
            def _(p):
                j = p * 2
                gather(j, buf0, gsem0).wait()
                write(c0 + j, buf0, wsem0).start()

                @pl.when(p > 0)
                def _():
                    write(c0 + j - 1, buf1, wsem1).wait()

                gather(j + 1, buf1, gsem1).start()
                gather(j + 1, buf1, gsem1).wait()
                write(c0 + j + 1, buf1, wsem1).start()
                write(c0 + j, buf0, wsem0).wait()

                @pl.when(p < cpb // 2 - 1)
                def _():
                    gather(j + 2, buf0, gsem0).start()

            write(c0 + cpb - 1, buf1, wsem1).wait()

    return gather_kernel(table, idx.reshape(m // SC_CHUNK, SC_CHUNK))


def _pack_rows(w):
    b = lax.bitcast_convert_type(w.astype(jnp.bfloat16), jnp.uint16).astype(jnp.uint32)
    words = (b[:, :PEER_WORDS] << 16) | b[:, PEER_WORDS:]
    return lax.bitcast_convert_type(words, jnp.int32)


def _unpack_rows(words):
    hi = pltpu.bitcast(words & jnp.int32(-65536), jnp.float32)
    lo = pltpu.bitcast(words << 16, jnp.float32)
    return hi, lo


def _peer_combine_kernel(gu_ref, gv_ref, h2_ref, g_ref, x1_ref, y_ref):
    f32 = jnp.float32
    tb = h2_ref.shape[0]
    eye = (lax.broadcasted_iota(jnp.int32, (PEER_SLOTS, PEER_SLOTS), 0)
           == lax.broadcasted_iota(jnp.int32, (PEER_SLOTS, PEER_SLOTS), 1))

    def token(n, carry):
        rows = pl.ds(pl.multiple_of(n * PEER_SLOTS, PEER_SLOTS), PEER_SLOTS)
        x = h2_ref[pl.ds(n, 1), :]
        uh, ul = _unpack_rows(gu_ref[rows, :])
        d = jnp.sum(uh * x[:, 0:PEER_WORDS] + ul * x[:, PEER_WORDS:], axis=1, keepdims=True)
        gate = jnp.sum(jnp.where(eye, g_ref[pl.ds(n, 1), :], 0.0), axis=1, keepdims=True)
        w = gate * jax.nn.gelu(d)
        vh, vl = _unpack_rows(gv_ref[rows, :])
        x1 = x1_ref[pl.ds(n, 1), :]
        y_ref[pl.ds(n, 1), 0:PEER_WORDS] = x1[:, 0:PEER_WORDS] + jnp.sum(w * vh, axis=0, keepdims=True)
        y_ref[pl.ds(n, 1), PEER_WORDS:] = x1[:, PEER_WORDS:] + jnp.sum(w * vl, axis=0, keepdims=True)
        return carry

    lax.fori_loop(0, tb, token, 0)


def _peer_combine_call(gu, gv, h2, gates, x1, *, tb):
    n = h2.shape[0]
    assert n % tb == 0
    row = lambda i: (i, 0)
    return pl.pallas_call(
        _peer_combine_kernel,
        grid=(n // tb,),
        in_specs=[
            pl.BlockSpec((tb * PEER_SLOTS, PEER_WORDS), row),
            pl.BlockSpec((tb * PEER_SLOTS, PEER_WORDS), row),
            pl.BlockSpec((tb, D_MODEL), row),
            pl.BlockSpec((tb, PEER_SLOTS), row),
            pl.BlockSpec((tb, D_MODEL), row),
        ],
        out_specs=pl.BlockSpec((tb, D_MODEL), row),
        out_shape=jax.ShapeDtypeStruct((n, D_MODEL), jnp.float32),
        compiler_params=pltpu.CompilerParams(
            dimension_semantics=("arbitrary",), vmem_limit_bytes=V7X_VMEM_LIMIT),
        name="peer_combine",
    )(gu, gv, h2, gates, x1)


def _tail_pallas(x2d, a, g, w_out, norm2_g, w_query, subkeys, u_words, v_words, row0, n):
    tm = _token_tile(n, (256, 128))
    x1, h2, st = _tail_call(x2d, a, g, w_out, norm2_g, w_query, subkeys, tm=tm, row0=row0, n=n)
    eidx_t, gates_t = _peer_topk_call(st, tt=tm)
    eidx = eidx_t.T.reshape(n * PEER_SLOTS)
    gu = _sc_gather_rows(u_words, eidx)
    gv = _sc_gather_rows(v_words, eidx)
    return _peer_combine_call(gu, gv, h2, gates_t.T, x1, tb=16)


def _compress(rows, pos, w):
    b, l = rows.shape[:2]
    nc = (l - CMP_BLOCK) // CMP_STRIDE + 1
    chunks = rows[:, :(nc + 1) * CMP_STRIDE].reshape(b, nc + 1, CMP_STRIDE, NSA_KV_HEADS, HEAD_DIM)
    first = jnp.einsum('bnphd,pde->bnhe', chunks, w[:CMP_STRIDE])
    second = jnp.einsum('bnphd,pde->bnhe', chunks, w[CMP_STRIDE:])
    bias = jnp.einsum('pd,pde->e', pos, w)
    return first[:, :-1] + second[:, 1:] + bias


def _nsa_keys(kv_all, cmp_pos, w_cmp, k_norm_g):
    kc = _rmsnorm(_compress(kv_all[:, :, 0], cmp_pos[0], w_cmp[0]), k_norm_g[0])
    vc = _compress(kv_all[:, :, 1], cmp_pos[1], w_cmp[1])
    return kc, vc, kv_all[:, :, 2], kv_all[:, :, 3]


def _nsa_block(q, t_pos, kc, vc, ks, vs, kw, vw, w_pos, gate):
    b, nq = q.shape[:2]
    nc = kc.shape[1]
    l = ks.shape[1]
    ns = -(-l // SEL_BLOCK)
    f32 = jnp.float32
    cmp_end = jnp.arange(nc) * CMP_STRIDE + (CMP_BLOCK - 1)
    m_c = cmp_end[None, :] <= t_pos[:, None]
    s_c = jnp.einsum('bqhgd,bnhd->bhgqn', q, kc).astype(f32)
    p_c = jax.nn.softmax(jnp.where(m_c, s_c, MASK_VALUE), axis=-1) * m_c
    o_c = jnp.einsum('bhgqn,bnhd->bqhgd', p_c.astype(vc.dtype), vc)
    ratio = SEL_BLOCK // CMP_STRIDE
    imp = jnp.pad(p_c.sum(axis=2), ((0, 0), (0, 0), (0, 0), (0, ns * ratio - nc)))
    imp = imp.reshape(b, NSA_KV_HEADS, nq, ns, ratio)
    imp = imp.sum(-1) + jnp.pad(imp[..., :-1, ratio - 1], ((0, 0), (0, 0), (0, 0), (1, 0)))
    blk = jnp.arange(ns)
    visible = blk[None, :] * SEL_BLOCK <= t_pos[:, None]
    forced = (blk[None, :] == 0) | (blk[None, :] == t_pos[:, None] // SEL_BLOCK)
    score = jnp.where(forced, FORCE_SCORE, jnp.where(visible, imp, -1.0))
    _, idx = lax.top_k(score, min(SEL_TOPK, ns))
    tok = idx[..., None] * SEL_BLOCK + jnp.arange(SEL_BLOCK)
    bi = jnp.arange(b)[:, None, None, None, None]
    hi = jnp.arange(NSA_KV_HEADS)[None, :, None, None, None]
    safe_tok = jnp.minimum(tok, l - 1)
    ks_g = ks[bi, safe_tok, hi]
    vs_g = vs[bi, safe_tok, hi]
    m_s = (tok <= t_pos[None, None, :, None, None])[:, :, None]
    s_s = jnp.einsum('bqhgd,bhqkpd->bhgqkp', q, ks_g).astype(f32)
    s_s = jnp.where(m_s, s_s, MASK_VALUE)
    p_s = jax.nn.softmax(s_s.reshape(s_s.shape[:4] + (-1,)), axis=-1).reshape(s_s.shape)
    o_s = jnp.einsum('bhgqkp,bhqkpd->bqhgd', p_s.astype(vs.dtype), vs_g)
    d = t_pos[:, None] - w_pos[None, :]
    m_w = (d >= 0) & (d <= WINDOW) & (w_pos[None, :] >= 0)
    s_w = jnp.einsum('bqhgd,bkhd->bhgqk', q, kw).astype(f32)
    p_w = jax.nn.softmax(jnp.where(m_w, s_w, MASK_VALUE), axis=-1)
    o_w = jnp.einsum('bhgqk,bkhd->bqhgd', p_w.astype(vw.dtype), vw)
    out = gate[..., 0:1] * o_c + gate[..., 1:2] * o_s + gate[..., 2:3] * o_w
    return out.reshape(b, nq, NSA_HEADS * HEAD_DIM).astype(q.dtype)


def _nsa_prompt(q, kv_rows, win_rows, gate, cmp_pos, w_cmp, k_norm_g):
    b, t = q.shape[:2]
    kc, vc, ks, vs = _nsa_keys(kv_rows, cmp_pos, w_cmp, k_norm_g)
    win = jnp.pad(win_rows, ((0, 0), (WINDOW, 0), (0, 0), (0, 0), (0, 0)))

    def one_block(blk):
        s = blk * Q_BLOCK
        qb = lax.dynamic_slice_in_dim(q, s, Q_BLOCK, axis=1)
        gb = lax.dynamic_slice_in_dim(gate, s, Q_BLOCK, axis=1)
        wb = lax.dynamic_slice_in_dim(win, s, WINDOW + Q_BLOCK, axis=1)
        t_pos = s + jnp.arange(Q_BLOCK)
        w_pos = s - WINDOW + jnp.arange(WINDOW + Q_BLOCK)
        return _nsa_block(qb, t_pos, kc, vc, ks, vs, wb[:, :, 0], wb[:, :, 1], w_pos, gb)

    out = lax.map(one_block, jnp.arange(t // Q_BLOCK))
    return out.transpose(1, 0, 2, 3).reshape(b, t, -1)


def _nsa_sample(q, kv_rows, win_rows, gate, cache_kv_l, cache_win, page_table, cmp_pos, w_cmp, k_norm_g):
    bd, t = q.shape[:2]
    past_len = page_table.shape[1] * PAGE_SIZE
    past = cache_kv_l[page_table].reshape(bd, past_len, KV_ROWS, NSA_KV_HEADS, HEAD_DIM)
    kc, vc, ks, vs = _nsa_keys(jnp.concatenate([past, kv_rows], axis=1), cmp_pos, w_cmp, k_norm_g)
    wbuf = cache_win.shape[1]
    win_all = jnp.concatenate([cache_win, win_rows], axis=1)
    t_pos = past_len + jnp.arange(t)
    w_pos = past_len - wbuf + jnp.arange(wbuf + t)
    out = _nsa_block(q, t_pos, kc, vc, ks, vs, win_all[:, :, 0], win_all[:, :, 1], w_pos, gate)
    return out, win_all[:, -min(WINDOW, past_len + t):]


def _gla_chunked(q, k, v, log_a, s0):
    b, t, h = q.shape[:3]
    c = math.gcd(t, GLA_CHUNK)
    n = t // c

    def to_chunks(a):
        return a.reshape(b, n, c, h, a.shape[-1]).transpose(1, 0, 3, 2, 4)

    causal = jnp.tril(jnp.ones((c, c), dtype=bool))

    def step(S, inp):
        qc, kc, vc, ac = inp
        cum = jnp.cumsum(ac, axis=2)
        diff = jnp.minimum(cum[:, :, :, None] - cum[:, :, None, :], 0.0)
        decay = jnp.where(causal[..., None], jnp.exp(diff), 0.0)
        att = jnp.einsum('bhid,bhjd,bhijd->bhij', qc, kc, decay)
        o = jnp.einsum('bhij,bhjv->bhiv', att, vc) + jnp.einsum('bhid,bhdv->bhiv', qc * jnp.exp(cum), S)
        last = cum[:, :, -1:]
        S = jnp.exp(last)[:, :, 0, :, None] * S + jnp.einsum('bhjd,bhjv->bhdv', kc * jnp.exp(last - cum), vc)
        return S, o

    S, o = lax.scan(step, s0, (to_chunks(q), to_chunks(k), to_chunks(v), to_chunks(log_a)))
    return o.transpose(1, 0, 3, 2, 4).reshape(b, t, h, -1), S


def _gla_mixer(gq, gk, gv, glr, gog, s0, w_gate, b_gate, norm_g):
    b, t = gq.shape[:2]
    f32 = jnp.float32
    q = gq.reshape(b, t, GLA_HEADS, GLA_DK).astype(f32) * (GLA_DK ** -0.5)
    k = gk.reshape(b, t, GLA_HEADS, GLA_DK).astype(f32)
    v = gv.reshape(b, t, GLA_HEADS, GLA_DV).astype(f32)
    log_a = jax.nn.log_sigmoid((glr @ w_gate + b_gate).astype(f32)).reshape(b, t, GLA_HEADS, GLA_DK) / GLA_GATE_TEMP
    o, S = _gla_chunked(q, k, v, log_a, s0.astype(f32))
    o = _rmsnorm(o, norm_g) * jax.nn.silu(gog.astype(f32)).reshape(b, t, GLA_HEADS, GLA_DV)
    return o.reshape(b, t, -1).astype(gq.dtype), S


def _peer_ffn(h, w_query, subkeys, u, v):
    b, t, d = h.shape
    n = b * t
    nb = -(-n // PEER_TOKEN_BLOCK)
    flat = jnp.pad(h.reshape(n, d), ((0, nb * PEER_TOKEN_BLOCK - n), (0, 0))).reshape(nb, PEER_TOKEN_BLOCK, d)

    def one_block(xb):
        qh = (xb @ w_query).reshape(-1, PEER_HEADS, 2, PEER_KEY_DIM // 2)
        s = jnp.einsum('nhcd,hckd->nhck', qh, subkeys).astype(jnp.float32)
        s1, i1 = lax.top_k(s[:, :, 0], PEER_TOPK)
        s2, i2 = lax.top_k(s[:, :, 1], PEER_TOPK)
        cand = (s1[..., :, None] + s2[..., None, :]).reshape(s1.shape[:-1] + (-1,))
        cidx = (i1[..., :, None] * PEER_NKEYS + i2[..., None, :]).reshape(i1.shape[:-1] + (-1,))
        top, pos = lax.top_k(cand, PEER_TOPK)
        eidx = jnp.take_along_axis(cidx, pos, axis=-1)
        g = jax.nn.softmax(top, axis=-1)
        act = jax.nn.gelu(jnp.einsum('nhkd,nd->nhk', jnp.take(u, eidx, axis=0), xb).astype(jnp.float32))
        return jnp.einsum('nhk,nhkd->nd', (g * act).astype(xb.dtype), jnp.take(v, eidx, axis=0))

    out = lax.map(one_block, flat)
    return out.reshape(-1, d)[:n].reshape(b, t, d)


def _residual_tail(x, mix, w_out, norm2_g, w_query, subkeys, u, v):
    x = x + (mix @ w_out).astype(x.dtype)
    return x + _peer_ffn(_rmsnorm(x, norm2_g), w_query, subkeys, u, v).astype(x.dtype)


def _mixer_inputs(x, norm1_g, w_in_r, q_norm_g, k_norm_g, *, tm):
    b, t = x.shape[:2]
    q2d, kvr, winr, misc, gq, gk, gv, gog, kvb = _inproj(
        x.reshape(b * t, D_MODEL), norm1_g, w_in_r, q_norm_g, k_norm_g, tm=tm)
    q = q2d.astype(jnp.float32).reshape(b, t, NSA_KV_HEADS, NSA_GROUP, HEAD_DIM)
    kv_rows = kvr.reshape(b, t, 4, NSA_KV_HEADS, HEAD_DIM)
    win_rows = winr.reshape(b, t, 2, NSA_KV_HEADS, HEAD_DIM)
    gate = jax.nn.sigmoid(misc[:, :GATE_W]).reshape(b, t, NSA_KV_HEADS, NSA_GROUP, 3)
    glr = misc[:, GATE_W:GATE_W + GLA_GATE_RANK].reshape(b, t, -1)
    rs = lambda a: a.reshape(b, t, -1)
    return q, kv_rows, win_rows, gate, (rs(gq), rs(gk), rs(gv), glr, rs(gog)), (q2d, misc, kvr, kvb, gq, gk, gv, gog)


def _token_tile(n, candidates=(512, 256, 128, 64, 32, 16, 8)):
    for tm in candidates:
        if n % tm == 0:
            return tm
    raise ValueError(n)


def kernel(x_prompt, x_sample, cache_kv, cache_win, state_gla, page_table, norm1_g, w_in, q_norm_g, k_norm_g, cmp_pos, w_cmp, gla_w_gate, gla_b_gate, gla_norm_g, w_out, norm2_g, peer_w_query, peer_subkeys, peer_u, peer_v):
    depth = w_in.shape[0]
    xp, xs = x_prompt, x_sample
    kv_p, win_p, gla_p, kv_s, win_s, gla_s = [], [], [], [], [], []
    for l in range(depth):
        w_in_r = _reorder_w_in(w_in[l])
        u_words, v_words = _pack_rows(peer_u[l]), _pack_rows(peer_v[l])
        kv_shape = (KV_ROWS, NSA_KV_HEADS, HEAD_DIM)
        win_shape = (2, NSA_KV_HEADS, HEAD_DIM)

        bs, ts = xs.shape[0], xs.shape[1]
        ns = bs * ts
        xs2d = xs.reshape(ns, D_MODEL)
        q2d, kvr, winr, misc, gq, gk, gv, gog, kvb = _inproj(
            xs2d, norm1_g[l], w_in_r, q_norm_g[l], k_norm_g[l], tm=_token_tile(ns))
        a = _nsa_sample_call(q2d, misc, kvb, cache_kv[l], cache_win[l], page_table,
                             cmp_pos[l], w_cmp[l], k_norm_g[l], ts)
        g, s_new = _gla_call(gq, gk, gv, gog, misc, state_gla[l].astype(jnp.float32), gla_w_gate[l], gla_b_gate[l],
                             gla_norm_g[l], bs, ts)
        win_all = jnp.concatenate([cache_win[l], winr.reshape((bs, ts) + win_shape)], axis=1)
        kv_s.append(kvr.reshape((bs, ts) + kv_shape))
        win_s.append(win_all[:, -min(WINDOW, page_table.shape[1] * PAGE_SIZE + ts):])
        gla_s.append(s_new.astype(state_gla.dtype))
        xs = _tail_pallas(xs2d, a, g, w_out[l], norm2_g[l], peer_w_query[l], peer_subkeys[l],
                          u_words, v_words, 0, ns).reshape(xs.shape)

        bp, tp = xp.shape[0], xp.shape[1]
        x2d = xp.reshape(bp * tp, D_MODEL)
        q2d, kvr, winr, misc, gq, gk, gv, gog, kvb = _inproj(
            x2d, norm1_g[l], w_in_r, q_norm_g[l], k_norm_g[l], tm=_token_tile(bp * tp))
        kcp, vcp = _prompt_compressed_kv(kvr, cmp_pos[l], w_cmp[l], k_norm_g[l], bp, tp)
        s0 = jnp.zeros((bp, GLA_HEADS, GLA_DK, GLA_DV), jnp.float32)
        g, s_new = _gla_call(gq, gk, gv, gog, misc, s0, gla_w_gate[l], gla_b_gate[l], gla_norm_g[l], bp, tp)
        kv_p.append(kvr.reshape((bp, tp) + kv_shape))
        win_p.append(winr.reshape((bp, tp) + win_shape)[:, -min(WINDOW, tp):])
        gla_p.append(s_new.astype(state_gla.dtype))
        nb = tp // Q_BLOCK
        nblk = _token_tile(nb, (nb // PROMPT_SPLITS, nb))
        ys = []
        for b in range(bp):
            for blk0 in range(0, nb, nblk):
                a = _nsa_prompt_call(q2d, misc, kcp, vcp, kvb, tp, b, blk0, nblk)
                ys.append(_tail_pallas(x2d, a, g, w_out[l], norm2_g[l], peer_w_query[l], peer_subkeys[l],
                                       u_words, v_words, (b * nb + blk0) * Q_BLOCK, nblk * Q_BLOCK))
        xp = jnp.concatenate(ys, axis=0).reshape(bp, tp, D_MODEL)
    return (xp, xs, jnp.stack(kv_p), jnp.stack(win_p), jnp.stack(gla_p),
            jnp.stack(kv_s), jnp.stack(win_s), jnp.stack(gla_s))
```

```python
import functools
import math

import jax
import jax.numpy as jnp
import numpy as np
from jax import lax
from jax.experimental import pallas as pl
from jax.experimental.pallas import tpu as pltpu
from jax.experimental.pallas import tpu_sc as plsc

D_MODEL = 1024
NSA_HEADS = 8
NSA_KV_HEADS = 2
NSA_GROUP = NSA_HEADS // NSA_KV_HEADS
HEAD_DIM = 64
CMP_STRIDE = 16
CMP_BLOCK = 32
SEL_BLOCK = 64
SEL_TOPK = 16
WINDOW = 512
Q_BLOCK = 128
PAGE_SIZE = 128
GLA_HEADS = 4
GLA_DV = 128
GLA_DK = 64
GLA_GATE_RANK = 16
GLA_GATE_TEMP = 16.0
GLA_CHUNK = 64
PEER_HEADS = 8
PEER_NKEYS = 128
PEER_KEY_DIM = 256
PEER_TOPK = 16
PEER_TOKEN_BLOCK = 128
KV_ROWS = 4
RMS_EPS = 1e-6
MASK_VALUE = -1e30
FORCE_SCORE = 1e4

Q_W = NSA_HEADS * HEAD_DIM
KV_W = 6 * NSA_KV_HEADS * HEAD_DIM
GATE_W = 3 * NSA_HEADS
GQ_W = GLA_HEADS * GLA_DK
GV_W = GLA_HEADS * GLA_DV
MISC_W = 128
IN_SIZES = (Q_W, KV_W, GATE_W, GQ_W, GQ_W, GV_W, GLA_GATE_RANK, GV_W)
P_W = Q_W + KV_W + GQ_W + GQ_W + GV_W + GV_W + MISC_W

V7X_VMEM_LIMIT = 56 * 1024 * 1024


def _rmsnorm(x, g):
    xf = x.astype(jnp.float32)
    y = xf * lax.rsqrt(jnp.mean(xf * xf, axis=-1, keepdims=True) + RMS_EPS)
    return (y * g.astype(jnp.float32)).astype(x.dtype)


def _head_group_ones(width, group, dtype):
    r = lax.broadcasted_iota(jnp.int32, (width, width), 0) // group
    c = lax.broadcasted_iota(jnp.int32, (width, width), 1) // group
    return jnp.where(r == c, 1.0, 0.0).astype(dtype)


def _group_mean_sq(x, group):
    sq = x * x
    hi = sq.astype(jnp.bfloat16)
    lo = (sq - hi.astype(jnp.float32)).astype(jnp.bfloat16)
    ones = _head_group_ones(x.shape[-1], group, jnp.bfloat16)
    s = jnp.dot(hi, ones, preferred_element_type=jnp.float32)
    s = s + jnp.dot(lo, ones, preferred_element_type=jnp.float32)
    return s * (1.0 / group)


def _inproj_kernel(x_ref, g1_ref, w_ref, qg_ref, ksg_ref, kwg_ref,
                   q_ref, kv_ref, win_ref, misc_ref, gq_ref, gk_ref, gv_ref, gog_ref, kvb_ref):
    x = x_ref[...]
    h = x * lax.rsqrt(jnp.mean(x * x, axis=-1, keepdims=True) + RMS_EPS) * g1_ref[...]
    p = jnp.dot(h.astype(jnp.bfloat16), w_ref[...], preferred_element_type=jnp.float32)
    o = 0
    q = p[:, o:o + Q_W]; o += Q_W
    kv = p[:, o:o + KV_W]; o += KV_W
    gq_ref[...] = p[:, o:o + GQ_W]; o += GQ_W
    gk_ref[...] = p[:, o:o + GQ_W]; o += GQ_W
    gv_ref[...] = p[:, o:o + GV_W]; o += GV_W
    gog_ref[...] = p[:, o:o + GV_W]; o += GV_W
    misc_ref[...] = p[:, o:o + MISC_W]
    qn = q * lax.rsqrt(_group_mean_sq(q, HEAD_DIM) + RMS_EPS) * qg_ref[...] * (HEAD_DIM ** -0.5)
    q_ref[...] = qn.astype(q_ref.dtype)
    hw = NSA_KV_HEADS * HEAD_DIM
    k_sel = kv[:, 2 * hw:3 * hw]
    k_sel = k_sel * lax.rsqrt(_group_mean_sq(k_sel, HEAD_DIM) + RMS_EPS) * ksg_ref[...]
    k_win = kv[:, 4 * hw:5 * hw]
    k_win = k_win * lax.rsqrt(_group_mean_sq(k_win, HEAD_DIM) + RMS_EPS) * kwg_ref[...]
    kv_ref[:, 0:2 * hw] = kv[:, 0:2 * hw]
    kv_ref[:, 2 * hw:3 * hw] = k_sel
    kv_ref[:, 3 * hw:4 * hw] = kv[:, 3 * hw:4 * hw]
    win_ref[:, 0:hw] = k_win
    win_ref[:, hw:2 * hw] = kv[:, 5 * hw:6 * hw]
    kvb_ref[:, 0:hw] = k_sel.astype(kvb_ref.dtype)
    kvb_ref[:, hw:2 * hw] = kv[:, 3 * hw:4 * hw].astype(kvb_ref.dtype)
    kvb_ref[:, 2 * hw:3 * hw] = k_win.astype(kvb_ref.dtype)
    kvb_ref[:, 3 * hw:4 * hw] = kv[:, 5 * hw:6 * hw].astype(kvb_ref.dtype)


def _reorder_w_in(w_in):
    offs = np.cumsum((0,) + IN_SIZES)
    q, kv, gate, gq, gk, gv, glr, gog = [w_in[:, offs[i]:offs[i + 1]] for i in range(8)]
    pad = jnp.zeros((w_in.shape[0], MISC_W - GATE_W - GLA_GATE_RANK), w_in.dtype)
    return jnp.concatenate([q, kv, gq, gk, gv, gog, gate, glr, pad], axis=1).astype(jnp.bfloat16)


def _inproj(x2d, norm1_g, w_in_r, q_norm_g, k_norm_g, *, tm):
    n = x2d.shape[0]
    assert n % tm == 0
    hw = NSA_KV_HEADS * HEAD_DIM
    f32 = jnp.float32
    row = lambda i: (i, 0)
    const = lambda i: (0, 0)
    widths = (Q_W, 4 * hw, 2 * hw, MISC_W, GQ_W, GQ_W, GV_W, GV_W, 4 * hw)
    bf16 = jnp.bfloat16
    dtypes = (bf16, f32, f32, f32, f32, f32, f32, f32, bf16)
    return pl.pallas_call(
        _inproj_kernel,
        grid=(n // tm,),
        in_specs=[
            pl.BlockSpec((tm, D_MODEL), row),
            pl.BlockSpec((1, D_MODEL), const),
            pl.BlockSpec((D_MODEL, P_W), const),
            pl.BlockSpec((1, Q_W), const),
            pl.BlockSpec((1, hw), const),
            pl.BlockSpec((1, hw), const),
        ],
        out_specs=[pl.BlockSpec((tm, w), row) for w in widths],
        out_shape=[jax.ShapeDtypeStruct((n, w), dt) for w, dt in zip(widths, dtypes)],
        compiler_params=pltpu.CompilerParams(
            dimension_semantics=("arbitrary",), vmem_limit_bytes=V7X_VMEM_LIMIT),
        name="inproj",
    )(x2d, norm1_g.reshape(1, -1), w_in_r,
      jnp.tile(q_norm_g, NSA_HEADS).reshape(1, -1),
      jnp.tile(k_norm_g[1], NSA_KV_HEADS).reshape(1, -1),
      jnp.tile(k_norm_g[2], NSA_KV_HEADS).reshape(1, -1))


CMP_LANES = 2 * NSA_KV_HEADS * HEAD_DIM
ROW_LANES = KV_ROWS * NSA_KV_HEADS * HEAD_DIM


def _chunk_map(x, w_ref):
    acc = None
    for p in range(CMP_STRIDE):
        xp = x[:, p * ROW_LANES:p * ROW_LANES + CMP_LANES].astype(jnp.bfloat16)
        d = jnp.dot(xp, w_ref[p], preferred_element_type=jnp.float32)
        acc = d if acc is None else acc + d
    return acc


def _compress_kernel(x_ref, xn_ref, pos_ref, wf_ref, ws_ref, kg_ref, kc_ref, vc_ref):
    tn = x_ref.shape[1]
    x = x_ref[0]
    first = _chunk_map(x, wf_ref)
    second = _chunk_map(x, ws_ref)
    second_next = _chunk_map(xn_ref[0], ws_ref)
    bias = _chunk_map(pos_ref[0], wf_ref) + _chunk_map(pos_ref[1], ws_ref)
    rows = lax.broadcasted_iota(jnp.int32, second.shape, 0)
    shifted = jnp.where(rows == tn - 1, second_next[0:1, :], pltpu.roll(second, tn - 1, axis=0))
    out = first + shifted + bias[0:1, :]
    hw = NSA_KV_HEADS * HEAD_DIM
    kc = out[:, 0:hw]
    kc_ref[0] = kc * lax.rsqrt(_group_mean_sq(kc, HEAD_DIM) + RMS_EPS) * kg_ref[...]
    vc_ref[0] = out[:, hw:2 * hw]


def _compress_weights(w_cmp, cmp_pos):
    eye = jnp.eye(NSA_KV_HEADS, dtype=w_cmp.dtype)

    def bd(p):
        blocks = [jnp.kron(eye, w_cmp[r, p]) for r in range(2)]
        z = jnp.zeros_like(blocks[0])
        return jnp.concatenate([jnp.concatenate([blocks[0], z], 1), jnp.concatenate([z, blocks[1]], 1)], 0)

    wf = jnp.stack([bd(p) for p in range(CMP_STRIDE)]).astype(jnp.bfloat16)
    ws = jnp.stack([bd(p + CMP_STRIDE) for p in range(CMP_STRIDE)]).astype(jnp.bfloat16)

    def pos_rows(lo):
        pk = jnp.tile(cmp_pos[0, lo:lo + CMP_STRIDE], (1, NSA_KV_HEADS))
        pv = jnp.tile(cmp_pos[1, lo:lo + CMP_STRIDE], (1, NSA_KV_HEADS))
        row = jnp.concatenate([pk, pv, jnp.zeros_like(pk), jnp.zeros_like(pv)], axis=1)
        flat = row.reshape(1, CMP_STRIDE * ROW_LANES)
        return jnp.concatenate([flat, jnp.zeros((7, flat.shape[1]), flat.dtype)], axis=0)

    pos = jnp.stack([pos_rows(0), pos_rows(CMP_STRIDE)])
    return wf, ws, pos


def _compress_call(kv_chunks, wf, ws, pos, k_norm0, *, tn):
    b, nch, width = kv_chunks.shape
    assert nch % tn == 0 and tn % 8 == 0
    hw = NSA_KV_HEADS * HEAD_DIM
    last8 = nch // 8 - 1
    return pl.pallas_call(
        _compress_kernel,
        grid=(b, nch // tn),
        in_specs=[
            pl.BlockSpec((1, tn, width), lambda i, j: (i, j, 0)),
            pl.BlockSpec((1, 8, width), lambda i, j: (i, jnp.minimum((j + 1) * (tn // 8), last8), 0)),
            pl.BlockSpec((2, 8, width), lambda i, j: (0, 0, 0)),
            pl.BlockSpec((CMP_STRIDE, CMP_LANES, CMP_LANES), lambda i, j: (0, 0, 0)),
            pl.BlockSpec((CMP_STRIDE, CMP_LANES, CMP_LANES), lambda i, j: (0, 0, 0)),
            pl.BlockSpec((1, hw), lambda i, j: (0, 0)),
        ],
        out_specs=[pl.BlockSpec((1, tn, hw), lambda i, j: (i, j, 0))] * 2,
        out_shape=[jax.ShapeDtypeStruct((b, nch, hw), jnp.float32)] * 2,
        compiler_params=pltpu.CompilerParams(
            dimension_semantics=("arbitrary", "arbitrary"), vmem_limit_bytes=V7X_VMEM_LIMIT),
        name="compress",
    )(kv_chunks, kv_chunks, pos, wf, ws, jnp.tile(k_norm0, NSA_KV_HEADS).reshape(1, -1))


PROMPT_SPLITS = 4
SEL_TILE = 1024
SEL_PER_TILE = SEL_TILE // SEL_BLOCK
WIN_KEYS = WINDOW + Q_BLOCK
WIN_BLOCKS = WIN_KEYS // Q_BLOCK
ROW_CHUNK = 64
NEG_BIG = -3.0e38


def _lane_tile(x, reps):
    return jnp.concatenate([x] * reps, axis=1)


def _nsa_prompt_kernel(q_ref, misc_ref, kc_ref, vc_ref, ksvs_ref, w0, w1, w2, w3, w4,
                       psel_ref, pselt_ref, ebig_ref, o_ref,
                       q4_ref, s_ref, p_ref, bias_ref, bq_all_ref, psum_ref, m_ref, l_ref, al_ref,
                       kw_ref, vw_ref, oc_ref, os_ref, ow_ref, sc_ref, *, n_sel_blocks, blk0):
    i = pl.program_id(0) + blk0
    s0 = i * Q_BLOCK
    f32, bf16 = jnp.float32, jnp.bfloat16
    hw = NSA_KV_HEADS * HEAD_DIM
    n_chunks = NSA_GROUP * Q_BLOCK // ROW_CHUNK
    halves = Q_BLOCK // ROW_CHUNK

    qb = q_ref[...]
    for h in range(NSA_KV_HEADS):
        for g in range(NSA_GROUP):
            piece = jnp.dot(qb, psel_ref[h * NSA_GROUP + g], preferred_element_type=f32)
            q4_ref[h, g * Q_BLOCK:(g + 1) * Q_BLOCK, :] = piece.astype(bf16)

    for j, w in enumerate((w0, w1, w2, w3, w4)):
        kw_ref[j * Q_BLOCK:(j + 1) * Q_BLOCK, :] = w[:, 0:hw]
        vw_ref[j * Q_BLOCK:(j + 1) * Q_BLOCK, :] = w[:, hw:2 * hw]

    def chunk_rows(c):
        return pl.ds(c * ROW_CHUNK, ROW_CHUNK)

    def chunk_t(c):
        r = lax.broadcasted_iota(jnp.int32, (ROW_CHUNK, 1), 0)
        return s0 + (c % halves) * ROW_CHUNK + r

    for h in range(NSA_KV_HEADS):
        q4 = q4_ref[h]

        ncp = kc_ref.shape[1]
        nsb = ncp // 4
        s_ref[:, 0:ncp] = lax.dot_general(q4, kc_ref[0], (((1,), (1,)), ((), ())), preferred_element_type=f32)
        psum_ref[...] = jnp.zeros_like(psum_ref)

        def cmp_chunk(c, carry):
            rows = chunk_rows(c)
            t = chunk_t(c)
            col = lax.broadcasted_iota(jnp.int32, (ROW_CHUNK, ncp), 1)
            cidx = (col % nsb) * 4 + col // nsb
            valid = cidx * CMP_STRIDE + (CMP_BLOCK - 1) <= t
            s = jnp.where(valid, s_ref[rows, 0:ncp], MASK_VALUE)
            mx = jnp.max(s, axis=1, keepdims=True)
            e = jnp.exp(s - mx)
            p = jnp.where(valid, e / jnp.sum(e, axis=1, keepdims=True), 0.0)
            p_ref[rows, 0:ncp] = p.astype(bf16)
            hrows = pl.ds((c % halves) * ROW_CHUNK, ROW_CHUNK)
            psum_ref[hrows, :] += p
            return carry

        for c in range(n_chunks):
            cmp_chunk(c, 0)
        oc_ref[h] = jnp.dot(p_ref[:, 0:ncp], vc_ref[0], preferred_element_type=f32)

        ps = psum_ref[...]
        a3 = ps[:, 3 * nsb:4 * nsb]
        blk = lax.broadcasted_iota(jnp.int32, (Q_BLOCK, nsb), 1)
        tq = s0 + lax.broadcasted_iota(jnp.int32, (Q_BLOCK, nsb), 0)
        imp = ps[:, 0:nsb] + ps[:, nsb:2 * nsb] + ps[:, 2 * nsb:3 * nsb] + a3
        imp = imp + jnp.where(blk == 0, 0.0, pltpu.roll(a3, 1, axis=1))
        visible = blk * SEL_BLOCK <= tq
        forced = (blk == 0) | (blk == tq // SEL_BLOCK)
        sc_ref[h * Q_BLOCK:(h + 1) * Q_BLOCK, :] = jnp.where(forced, FORCE_SCORE, jnp.where(visible, imp, -1.0))

    blkf = lax.broadcasted_iota(jnp.int32, sc_ref.shape, 1).astype(f32)

    def pick(_, carry):
        sc, selm = carry
        mx = jnp.max(sc, axis=1, keepdims=True)
        first = jnp.min(jnp.where(sc == mx, blkf, float(nsb)), axis=1, keepdims=True)
        hit = blkf == first
        return jnp.where(hit, NEG_BIG, sc), jnp.where(hit, 1.0, selm)

    score = sc_ref[...]
    _, selm = lax.fori_loop(0, min(SEL_TOPK, n_sel_blocks), pick, (score, jnp.zeros_like(score)))
    bq_all_ref[...] = jnp.where(selm > 0.0, 0.0, MASK_VALUE).astype(bf16)

    for h in range(NSA_KV_HEADS):
        q4 = q4_ref[h]
        bq_ref = bq_all_ref.at[h * Q_BLOCK:(h + 1) * Q_BLOCK]

        m_ref[...] = jnp.full_like(m_ref, NEG_BIG)
        l_ref[...] = jnp.zeros_like(l_ref)
        os_ref[h] = jnp.zeros((NSA_GROUP * Q_BLOCK, hw), f32)

        def sel_tile(kt, carry):
            k0 = pl.multiple_of(kt * SEL_TILE, SEL_TILE)
            e_off = pl.multiple_of(nsb - kt * SEL_PER_TILE, SEL_PER_TILE)
            key = k0 + lax.broadcasted_iota(jnp.int32, (Q_BLOCK, SEL_TILE), 1)
            tq1 = s0 + lax.broadcasted_iota(jnp.int32, (Q_BLOCK, 1), 0)
            blockmask = jnp.dot(bq_ref[...], ebig_ref[pl.ds(e_off, nsb), :], preferred_element_type=f32)
            bias_ref[...] = jnp.where(key <= tq1, blockmask, MASK_VALUE)
            s_ref[...] = lax.dot_general(q4, ksvs_ref[pl.ds(k0, SEL_TILE), 0:hw],
                                         (((1,), (1,)), ((), ())), preferred_element_type=f32)

            for c in range(n_chunks):
                rows = slice(c * ROW_CHUNK, (c + 1) * ROW_CHUNK)
                hrows = slice((c % halves) * ROW_CHUNK, (c % halves + 1) * ROW_CHUNK)
                s = s_ref[rows, :] + bias_ref[hrows, :]
                m_old = m_ref[rows, :]
                m_new = jnp.maximum(m_old, jnp.max(s, axis=1, keepdims=True))
                p = jnp.exp(s - _lane_tile(m_new, SEL_TILE // 128))
                alpha = jnp.exp(m_old - m_new)
                l_ref[rows, :] = alpha * l_ref[rows, :] + jnp.sum(p, axis=1, keepdims=True)
                m_ref[rows, :] = m_new
                al_ref[rows, :] = alpha
                p_ref[rows, :] = p.astype(bf16)
            pv = jnp.dot(p_ref[...], ksvs_ref[pl.ds(k0, SEL_TILE), hw:2 * hw], preferred_element_type=f32)
            os_ref[h] = os_ref[h] * al_ref[...] + pv
            return carry

        lax.fori_loop(0, (s0 + Q_BLOCK - 1) // SEL_TILE + 1, sel_tile, 0)
        os_ref[h] = os_ref[h] / l_ref[...]

        s_ref[:, 0:WIN_KEYS] = lax.dot_general(q4, kw_ref[...], (((1,), (1,)), ((), ())),
                                               preferred_element_type=f32)

        def win_chunk(c, carry):
            rows = chunk_rows(c)
            t = chunk_t(c)
            pos = s0 - WINDOW + lax.broadcasted_iota(jnp.int32, (ROW_CHUNK, WIN_KEYS), 1)
            d = t - pos
            valid = (d >= 0) & (d <= WINDOW) & (pos >= 0)
            s = jnp.where(valid, s_ref[rows, 0:WIN_KEYS], MASK_VALUE)
            mx = jnp.max(s, axis=1, keepdims=True)
            e = jnp.exp(s - mx)
            p_ref[rows, 0:WIN_KEYS] = (e / jnp.sum(e, axis=1, keepdims=True)).astype(bf16)
            return carry

        for c in range(n_chunks):
            win_chunk(c, 0)
        ow_ref[h] = jnp.dot(p_ref[:, 0:WIN_KEYS], vw_ref[...], preferred_element_type=f32)

    gsig = jax.nn.sigmoid(misc_ref[...])
    out = jnp.zeros((Q_BLOCK, Q_W), f32)
    for h in range(NSA_KV_HEADS):
        for g in range(NSA_GROUP):
            hg = h * NSA_GROUP + g
            r = slice(g * Q_BLOCK, (g + 1) * Q_BLOCK)
            mix = (gsig[:, 3 * hg:3 * hg + 1] * oc_ref[h, r, :]
                   + gsig[:, 3 * hg + 1:3 * hg + 2] * os_ref[h, r, :]
                   + gsig[:, 3 * hg + 2:3 * hg + 3] * ow_ref[h, r, :])
            out = out + jnp.dot(mix.astype(bf16), pselt_ref[hg], preferred_element_type=f32)
    o_ref[...] = out.astype(o_ref.dtype)


def _nsa_constants(n_sel_blocks):
    hw = NSA_KV_HEADS * HEAD_DIM
    psel = np.zeros((NSA_HEADS, Q_W, hw), np.float32)
    for h in range(NSA_KV_HEADS):
        for g in range(NSA_GROUP):
            hg = h * NSA_GROUP + g
            for d in range(HEAD_DIM):
                psel[hg, hg * HEAD_DIM + d, h * HEAD_DIM + d] = 1.0
    pselt = np.transpose(psel, (0, 2, 1))
    r = np.arange(2 * n_sel_blocks)[:, None] - n_sel_blocks
    ebig = (r == (np.arange(SEL_TILE)[None, :] // SEL_BLOCK)).astype(np.float32)
    return (jnp.asarray(psel, jnp.bfloat16), jnp.asarray(pselt, jnp.bfloat16), jnp.asarray(ebig, jnp.bfloat16))


def _nsa_prompt_call(q, misc, kcp, vcp, kvb, seq, b, blk0, nblk):
    assert seq % SEL_TILE == 0 and seq % Q_BLOCK == 0
    nb = seq // Q_BLOCK
    nsb = seq // SEL_BLOCK
    hw = NSA_KV_HEADS * HEAD_DIM
    psel, pselt, ebig = _nsa_constants(nsb)
    rows4 = NSA_GROUP * Q_BLOCK
    f32, bf16 = jnp.float32, jnp.bfloat16
    r0 = b * nb + blk0

    def win_spec(j):
        return pl.BlockSpec((Q_BLOCK, 2 * hw),
                            lambda i: (b * nb + jnp.maximum(blk0 + i - (WIN_BLOCKS - 1) + j, 0), 1))

    return pl.pallas_call(
        functools.partial(_nsa_prompt_kernel, n_sel_blocks=nsb, blk0=blk0),
        grid=(nblk,),
        in_specs=[
            pl.BlockSpec((Q_BLOCK, Q_W), lambda i: (r0 + i, 0)),
            pl.BlockSpec((Q_BLOCK, MISC_W), lambda i: (r0 + i, 0)),
            pl.BlockSpec((1, seq // CMP_STRIDE, hw), lambda i: (b, 0, 0)),
            pl.BlockSpec((1, seq // CMP_STRIDE, hw), lambda i: (b, 0, 0)),
            pl.BlockSpec((seq, 2 * hw), lambda i: (b, 0)),
        ] + [win_spec(j) for j in range(WIN_BLOCKS)] + [
            pl.BlockSpec(psel.shape, lambda i: (0, 0, 0)),
            pl.BlockSpec(pselt.shape, lambda i: (0, 0, 0)),
            pl.BlockSpec(ebig.shape, lambda i: (0, 0)),
        ],
        out_specs=pl.BlockSpec((Q_BLOCK, Q_W), lambda i: (i, 0)),
        out_shape=jax.ShapeDtypeStruct((nblk * Q_BLOCK, Q_W), bf16),
        scratch_shapes=[
            pltpu.VMEM((NSA_KV_HEADS, rows4, hw), bf16),
            pltpu.VMEM((rows4, SEL_TILE), f32),
            pltpu.VMEM((rows4, SEL_TILE), bf16),
            pltpu.VMEM((Q_BLOCK, SEL_TILE), f32),
            pltpu.VMEM((NSA_KV_HEADS * Q_BLOCK, nsb), bf16),
            pltpu.VMEM((Q_BLOCK, seq // CMP_STRIDE), f32),
            pltpu.VMEM((rows4, hw), f32),
            pltpu.VMEM((rows4, hw), f32),
            pltpu.VMEM((rows4, hw), f32),
            pltpu.VMEM((WIN_KEYS, hw), bf16),
            pltpu.VMEM((WIN_KEYS, hw), bf16),
            pltpu.VMEM((NSA_KV_HEADS, rows4, hw), f32),
            pltpu.VMEM((NSA_KV_HEADS, rows4, hw), f32),
            pltpu.VMEM((NSA_KV_HEADS, rows4, hw), f32),
            pltpu.VMEM((NSA_KV_HEADS * Q_BLOCK, nsb), f32),
        ],
        compiler_params=pltpu.CompilerParams(
            dimension_semantics=("arbitrary",), vmem_limit_bytes=V7X_VMEM_LIMIT),
        name="nsa_prompt",
    )(q, misc, kcp, vcp, kvb, kvb, kvb, kvb, kvb, kvb, psel, pselt, ebig)


def _prompt_compressed_kv(kvr2d, cmp_pos, w_cmp, k_norm_g, batch, seq):
    nch = seq // CMP_STRIDE
    wf, ws, pos = _compress_weights(w_cmp, cmp_pos)
    kc, vc = _compress_call(kvr2d.reshape(batch, nch, CMP_STRIDE * ROW_LANES), wf, ws, pos, k_norm_g[0],
                            tn=min(256, nch))

    def perm(a):
        return a.reshape(batch, nch // 4, 4, a.shape[-1]).transpose(0, 2, 1, 3).reshape(batch, nch, -1).astype(jnp.bfloat16)

    return perm(kc), perm(vc)


PAD_KEYS = 128


def _softmax_piece_max(pieces):
    m = None
    for s in pieces:
        pm = jnp.max(s, axis=1, keepdims=True)
        m = pm if m is None else jnp.maximum(m, pm)
    return m


def _nsa_sample_kernel(pt_ref, q_ref, misc_ref, kvb_ref, win_ref, cache_ref, wf_ref, ws_ref, pos_ref, kg_ref,
                       psel_ref, pselt_ref, ebig_ref, o_ref, pages_ref, sem_ref, *, past_len, n_new):
    f32, bf16 = jnp.float32, jnp.bfloat16
    b = pl.program_id(0)
    nseq = pl.num_programs(0)
    n_pages = past_len // PAGE_SIZE
    nsb = past_len // SEL_BLOCK
    ncp = past_len // CMP_STRIDE
    hw = NSA_KV_HEADS * HEAD_DIM
    slot = b % 2
    rows_q = NSA_GROUP * n_new

    def page_copy(seq, j, kind, s):
        return pltpu.make_async_copy(cache_ref.at[pt_ref[seq, j], :, pl.ds(kind * hw, hw)],
                                     pages_ref.at[s, kind, pl.ds(pl.multiple_of(j * PAGE_SIZE, PAGE_SIZE), PAGE_SIZE)],
                                     sem_ref.at[s])

    def for_each_page_copy(seq, s, fn):
        def body(j, c):
            for kind in range(KV_ROWS):
                fn(page_copy(seq, j, kind, s))
            return c
        lax.fori_loop(0, n_pages, body, 0)

    @pl.when(b == 0)
    def _():
        pages_ref[:, :, past_len:past_len + SEL_BLOCK, :] = jnp.zeros((2, KV_ROWS, SEL_BLOCK, hw), f32)
        for_each_page_copy(0, 0, lambda cp: cp.start())

    @pl.when(b + 1 < nseq)
    def _():
        for_each_page_copy(b + 1, 1 - slot, lambda cp: cp.start())

    for_each_page_copy(b, slot, lambda cp: cp.wait())

    def strided(start):
        rows = pl.ds(start, nsb, stride=SEL_BLOCK)
        return jnp.concatenate([pages_ref[slot, 0, rows, :], pages_ref[slot, 1, rows, :]], axis=1).astype(bf16)

    first = None
    second = None
    for p in range(CMP_STRIDE):
        xf = jnp.concatenate([strided(CMP_STRIDE * r + p) for r in range(4)], axis=0)
        xs = jnp.concatenate([strided(CMP_STRIDE * (r + 1) + p) for r in range(4)], axis=0)
        df = jnp.dot(xf, wf_ref[p], preferred_element_type=f32)
        ds_ = jnp.dot(xs, ws_ref[p], preferred_element_type=f32)
        first = df if first is None else first + df
        second = ds_ if second is None else second + ds_
    bias = _chunk_map(pos_ref[0], wf_ref) + _chunk_map(pos_ref[1], ws_ref)
    cmp_out = first + second + bias[0:1, :]
    kc = cmp_out[:, 0:hw]
    kc = (kc * lax.rsqrt(_group_mean_sq(kc, HEAD_DIM) + RMS_EPS) * kg_ref[...]).astype(bf16)
    vc = cmp_out[:, hw:2 * hw].astype(bf16)

    qb = q_ref[...]
    newkv = kvb_ref[...]
    zpad = jnp.zeros((PAD_KEYS - n_new, hw), bf16)
    ks_new = jnp.concatenate([newkv[:, 0:hw], zpad], axis=0)
    vs_new = jnp.concatenate([newkv[:, hw:2 * hw], zpad], axis=0)
    kw_new = jnp.concatenate([newkv[:, 2 * hw:3 * hw], zpad], axis=0)
    vw_new = jnp.concatenate([newkv[:, 3 * hw:4 * hw], zpad], axis=0)
    wcache = win_ref[0]
    wbuf = wcache.shape[0]
    kw_old = wcache[:, 0:hw].astype(bf16)
    vw_old = wcache[:, hw:2 * hw].astype(bf16)

    tl = lax.broadcasted_iota(jnp.int32, (rows_q, 1), 0) % n_new
    t_abs = past_len + tl
    new_col = lax.broadcasted_iota(jnp.int32, (rows_q, PAD_KEYS), 1)
    new_ok = new_col <= tl
    nt_dims = (((1,), (1,)), ((), ()))
    gsig = jax.nn.sigmoid(misc_ref[...])
    out = jnp.zeros((n_new, Q_W), f32)

    for h in range(NSA_KV_HEADS):
        q4 = jnp.concatenate(
            [jnp.dot(qb, psel_ref[h * NSA_GROUP + g], preferred_element_type=f32).astype(bf16)
             for g in range(NSA_GROUP)], axis=0)

        s = lax.dot_general(q4, kc, nt_dims, preferred_element_type=f32)
        col = lax.broadcasted_iota(jnp.int32, (rows_q, ncp), 1)
        cidx = (col % nsb) * 4 + col // nsb
        valid = cidx * CMP_STRIDE + (CMP_BLOCK - 1) <= t_abs
        s = jnp.where(valid, s, MASK_VALUE)
        e = jnp.exp(s - jnp.max(s, axis=1, keepdims=True))
        pc = jnp.where(valid, e / jnp.sum(e, axis=1, keepdims=True), 0.0)
        o_c = jnp.dot(pc.astype(bf16), vc, preferred_element_type=f32)
        psum = pc[0:n_new]
        for g in range(1, NSA_GROUP):
            psum = psum + pc[g * n_new:(g + 1) * n_new]

        a3 = psum[:, 3 * nsb:4 * nsb]
        blk = lax.broadcasted_iota(jnp.int32, (n_new, nsb), 1)
        imp = psum[:, 0:nsb] + psum[:, nsb:2 * nsb] + psum[:, 2 * nsb:3 * nsb] + a3
        imp = imp + jnp.where(blk == 0, 0.0, pltpu.roll(a3, 1, axis=1))
        score = jnp.where(blk == 0, FORCE_SCORE, imp)
        blkf = blk.astype(f32)
        selm = jnp.zeros_like(score)
        for _ in range(SEL_TOPK - 1):
            mx = jnp.max(score, axis=1, keepdims=True)
            firstb = jnp.min(jnp.where(score == mx, blkf, float(nsb)), axis=1, keepdims=True)
            hit = blkf == firstb
            selm = jnp.where(hit, 1.0, selm)
            score = jnp.where(hit, NEG_BIG, score)
        bq = jnp.where(selm > 0.0, 0.0, MASK_VALUE).astype(bf16)

        s_new = jnp.where(new_ok, lax.dot_general(q4, ks_new, nt_dims, preferred_element_type=f32), MASK_VALUE)
        m_run = jnp.max(s_new, axis=1, keepdims=True)
        p_new = jnp.exp(s_new - m_run)
        l_run = jnp.sum(p_new, axis=1, keepdims=True)
        acc = jnp.dot(p_new.astype(bf16), vs_new, preferred_element_type=f32)

        def sel_tile(kt, carry):
            m_run, l_run, acc = carry
            k0 = pl.multiple_of(kt * SEL_TILE, SEL_TILE)
            e_off = pl.multiple_of(nsb - kt * SEL_PER_TILE, SEL_PER_TILE)
            bias = jnp.dot(bq, ebig_ref[pl.ds(e_off, nsb), :], preferred_element_type=f32)
            kt_rows = pages_ref[slot, 2, pl.ds(k0, SEL_TILE), :].astype(bf16)
            vt_rows = pages_ref[slot, 3, pl.ds(k0, SEL_TILE), :].astype(bf16)
            s = lax.dot_general(q4, kt_rows, nt_dims, preferred_element_type=f32)
            s = s + jnp.concatenate([bias] * NSA_GROUP, axis=0)
            m_new = jnp.maximum(m_run, jnp.max(s, axis=1, keepdims=True))
            p = jnp.exp(s - m_new)
            alpha = jnp.exp(m_run - m_new)
            l_new = alpha * l_run + jnp.sum(p, axis=1, keepdims=True)
            acc = acc * alpha + jnp.dot(p.astype(bf16), vt_rows, preferred_element_type=f32)
            return m_new, l_new, acc

        m_run, l_run, acc = lax.fori_loop(0, past_len // SEL_TILE, sel_tile, (m_run, l_run, acc))
        o_s = acc / l_run

        wpos = past_len - wbuf + lax.broadcasted_iota(jnp.int32, (rows_q, wbuf), 1)
        d = t_abs - wpos
        ok_old = (d >= 0) & (d <= WINDOW) & (wpos >= 0)
        s_old = jnp.where(ok_old, lax.dot_general(q4, kw_old, nt_dims, preferred_element_type=f32), MASK_VALUE)
        s_nw = jnp.where(new_ok, lax.dot_general(q4, kw_new, nt_dims, preferred_element_type=f32), MASK_VALUE)
        mw = _softmax_piece_max([s_old, s_nw])
        e_old = jnp.exp(s_old - mw)
        e_nw = jnp.exp(s_nw - mw)
        lw = jnp.sum(e_old, axis=1, keepdims=True) + jnp.sum(e_nw, axis=1, keepdims=True)
        o_w = (jnp.dot((e_old / lw).astype(bf16), vw_old, preferred_element_type=f32)
               + jnp.dot((e_nw / lw).astype(bf16), vw_new, preferred_element_type=f32))

        for g in range(NSA_GROUP):
            hg = h * NSA_GROUP + g
            r = slice(g * n_new, (g + 1) * n_new)
            mix = (gsig[:, 3 * hg:3 * hg + 1] * o_c[r] + gsig[:, 3 * hg + 1:3 * hg + 2] * o_s[r]
                   + gsig[:, 3 * hg + 2:3 * hg + 3] * o_w[r])
            out = out + jnp.dot(mix.astype(bf16), pselt_ref[hg], preferred_element_type=f32)
    o_ref[...] = out.astype(o_ref.dtype)


def _nsa_sample_call(q, misc, kvb, cache_kv_l, cache_win_l, page_table, cmp_pos, w_cmp, k_norm_g, n_new):
    bsz, n_pages = page_table.shape
    past_len = n_pages * PAGE_SIZE
    assert past_len % SEL_TILE == 0 and n_new % 8 == 0 and n_new <= PAD_KEYS
    assert (past_len + n_new - CMP_BLOCK) // CMP_STRIDE + 1 == past_len // CMP_STRIDE - 1
    nsb = past_len // SEL_BLOCK
    hw = NSA_KV_HEADS * HEAD_DIM
    wbuf = cache_win_l.shape[1]
    psel, pselt, ebig = _nsa_constants(nsb)
    wf, ws, pos = _compress_weights(w_cmp, cmp_pos)
    cache = cache_kv_l.reshape(cache_kv_l.shape[0], PAGE_SIZE, ROW_LANES)
    win = cache_win_l.reshape(bsz, wbuf, 2 * hw)
    row = lambda i, pt: (i, 0)
    c2 = lambda i, pt: (0, 0)
    c3 = lambda i, pt: (0, 0, 0)
    grid_spec = pltpu.PrefetchScalarGridSpec(
        num_scalar_prefetch=1,
        grid=(bsz,),
        in_specs=[
            pl.BlockSpec((n_new, Q_W), row),
            pl.BlockSpec((n_new, MISC_W), row),
            pl.BlockSpec((n_new, 4 * hw), row),
            pl.BlockSpec((1, wbuf, 2 * hw), lambda i, pt: (i, 0, 0)),
            pl.BlockSpec(memory_space=pl.ANY),
            pl.BlockSpec(wf.shape, c3), pl.BlockSpec(ws.shape, c3), pl.BlockSpec(pos.shape, c3),
            pl.BlockSpec((1, hw), c2),
            pl.BlockSpec(psel.shape, c3), pl.BlockSpec(pselt.shape, c3), pl.BlockSpec(ebig.shape, c2),
        ],
        out_specs=pl.BlockSpec((n_new, Q_W), row),
        scratch_shapes=[pltpu.VMEM((2, KV_ROWS, past_len + SEL_BLOCK, hw), jnp.float32),
                        pltpu.SemaphoreType.DMA((2,))],
    )
    return pl.pallas_call(
        functools.partial(_nsa_sample_kernel, past_len=past_len, n_new=n_new),
        grid_spec=grid_spec,
        out_shape=jax.ShapeDtypeStruct((bsz * n_new, Q_W), jnp.bfloat16),
        compiler_params=pltpu.CompilerParams(
            dimension_semantics=("arbitrary",), vmem_limit_bytes=V7X_VMEM_LIMIT),
        name="nsa_sample",
    )(page_table, q, misc, kvb, win, cache, wf, ws, pos,
      jnp.tile(k_norm_g[0], NSA_KV_HEADS).reshape(1, -1), psel, pselt, ebig)


GLA_J_GROUP = 8


def _split3(x):
    hi = x.astype(jnp.bfloat16)
    r = x - hi.astype(jnp.float32)
    mid = r.astype(jnp.bfloat16)
    lo = (r - mid.astype(jnp.float32)).astype(jnp.bfloat16)
    return hi, mid, lo


def _gla_kernel(gq_ref, gk_ref, gv_ref, gog_ref, misc_ref, s0_ref, wg_ref, bg_ref, ng_ref,
                o_ref, sout_ref, sbd_ref, la_ref, cum_ref, *, chunk):
    f32, bf16 = jnp.float32, jnp.bfloat16
    tstep = pl.program_id(1)
    n_tsteps = pl.num_programs(1)
    tb = gq_ref.shape[0]
    c = chunk
    mm = bf16 if c % 16 == 0 else f32
    hk, hv = GQ_W, GV_W

    @pl.when(tstep == 0)
    def _():
        sbd_ref[...] = jnp.zeros_like(sbd_ref)
        for h in range(GLA_HEADS):
            sbd_ref[h * GLA_DK:(h + 1) * GLA_DK, h * GLA_DV:(h + 1) * GLA_DV] = s0_ref[0, h]

    z = jnp.dot(misc_ref[...].astype(bf16), wg_ref[...], preferred_element_type=f32) + bg_ref[...]
    la_ref[...] = (jnp.minimum(z, 0.0) - jnp.log1p(jnp.exp(-jnp.abs(z)))) * (1.0 / GLA_GATE_TEMP)

    ri = lax.broadcasted_iota(jnp.int32, (c, c), 0)
    ci = lax.broadcasted_iota(jnp.int32, (c, c), 1)
    tril = jnp.where(ri >= ci, 1.0, 0.0).astype(bf16)
    kr = lax.broadcasted_iota(jnp.int32, (hk, hk), 0) // GLA_DK
    kc = lax.broadcasted_iota(jnp.int32, (hk, hk), 1) // GLA_DK
    head_rep = jnp.where(kr == kc, 1.0, 0.0).astype(bf16)
    eye_k = (lax.broadcasted_iota(jnp.int32, (hk, hk), 0) == lax.broadcasted_iota(jnp.int32, (hk, hk), 1))
    bd_mask = (lax.broadcasted_iota(jnp.int32, (hk, hv), 0) // GLA_DK
               == lax.broadcasted_iota(jnp.int32, (hk, hv), 1) // GLA_DV)
    lane_j = lax.broadcasted_iota(jnp.int32, (c, hk), 1) % GLA_DK
    row_i = lax.broadcasted_iota(jnp.int32, (c, hk), 0)

    def one_chunk(ch, carry):
        rows = pl.ds(pl.multiple_of(ch * c, c), c)
        q = gq_ref[rows, :] * (GLA_DK ** -0.5)
        k = gk_ref[rows, :]
        v = gv_ref[rows, :]
        la = la_ref[rows, :]
        hi, mid, lo = _split3(la)
        cum = (jnp.dot(tril, hi, preferred_element_type=f32) + jnp.dot(tril, mid, preferred_element_type=f32)
               + jnp.dot(tril, lo, preferred_element_type=f32))
        last = cum[c - 1:c, :]
        cum_ref[...] = cum

        def j_group(g, att):
            ws = []
            for jj in range(GLA_J_GROUP):
                j = g * GLA_J_GROUP + jj
                jrow = pl.ds(ch * c + j, 1)
                kj = gk_ref[jrow, :]
                cumj = cum_ref[pl.ds(j, 1), :]
                dec = jnp.where(row_i >= j, jnp.exp(jnp.minimum(cum - cumj, 0.0)), 0.0)
                ws.append((q * kj * dec).astype(bf16))
            r = jnp.dot(jnp.concatenate(ws, axis=0), head_rep, preferred_element_type=f32)
            for jj in range(GLA_J_GROUP):
                j = g * GLA_J_GROUP + jj
                att = att + jnp.where(lane_j == j, r[jj * c:(jj + 1) * c, :], 0.0)
            return att

        att = lax.fori_loop(0, c // GLA_J_GROUP, j_group, jnp.zeros((c, hk), f32))

        vt = jnp.concatenate([v] * (GLA_DK // c), axis=0) if c < GLA_DK else v
        vbd = jnp.where(bd_mask, jnp.concatenate([vt] * GLA_HEADS, axis=0), 0.0)
        sbd = sbd_ref[...]
        o = jnp.dot(att.astype(bf16), vbd.astype(bf16), preferred_element_type=f32)
        o = o + jnp.dot((q * jnp.exp(cum)).astype(bf16), sbd.astype(bf16), preferred_element_type=f32)

        ke = k * jnp.exp(last - cum)
        upd = lax.dot_general(ke.astype(mm), v.astype(mm), (((0,), (0,)), ((), ())), preferred_element_type=f32)
        dcol = jnp.sum(jnp.where(eye_k, jnp.exp(last), 0.0), axis=1, keepdims=True)
        sbd_ref[...] = sbd * dcol + jnp.where(bd_mask, upd, 0.0)

        gog = gog_ref[rows, :]
        for h in range(GLA_HEADS):
            sl = slice(h * GLA_DV, (h + 1) * GLA_DV)
            oh = o[:, sl]
            oh = oh * lax.rsqrt(jnp.mean(oh * oh, axis=1, keepdims=True) + RMS_EPS) * ng_ref[...]
            gh = gog[:, sl]
            o_ref[rows, sl] = (oh * gh * jax.nn.sigmoid(gh)).astype(o_ref.dtype)
        return carry

    lax.fori_loop(0, tb // c, one_chunk, 0)

    @pl.when(tstep == n_tsteps - 1)
    def _():
        for h in range(GLA_HEADS):
            sout_ref[0, h] = sbd_ref[h * GLA_DK:(h + 1) * GLA_DK, h * GLA_DV:(h + 1) * GLA_DV]


def _gla_call(gq, gk, gv, gog, misc, s0, w_gate, b_gate, norm_g, batch, seq):
    c = math.gcd(seq, GLA_CHUNK)
    tb = _token_tile(seq, (512, 256, 128, 64, 32, 16, 8))
    tb = max(tb, c)
    nt = seq // tb
    f32, bf16 = jnp.float32, jnp.bfloat16
    wg = jnp.zeros((MISC_W, GQ_W), f32).at[GATE_W:GATE_W + GLA_GATE_RANK].set(w_gate).astype(bf16)
    row = lambda b, t: (b * nt + t, 0)
    const = lambda b, t: (0, 0)
    state_spec = pl.BlockSpec((1, GLA_HEADS, GLA_DK, GLA_DV), lambda b, t: (b, 0, 0, 0))
    return pl.pallas_call(
        functools.partial(_gla_kernel, chunk=c),
        grid=(batch, nt),
        in_specs=[
            pl.BlockSpec((tb, GQ_W), row), pl.BlockSpec((tb, GQ_W), row),
            pl.BlockSpec((tb, GV_W), row), pl.BlockSpec((tb, GV_W), row),
            pl.BlockSpec((tb, MISC_W), row), state_spec,
            pl.BlockSpec((MISC_W, GQ_W), const), pl.BlockSpec((1, GQ_W), const), pl.BlockSpec((1, GLA_DV), const),
        ],
        out_specs=[pl.BlockSpec((tb, GV_W), row), state_spec],
        out_shape=[jax.ShapeDtypeStruct((batch * seq, GV_W), bf16),
                   jax.ShapeDtypeStruct((batch, GLA_HEADS, GLA_DK, GLA_DV), f32)],
        scratch_shapes=[pltpu.VMEM((GQ_W, GV_W), f32), pltpu.VMEM((tb, GQ_W), f32), pltpu.VMEM((c, GQ_W), f32)],
        compiler_params=pltpu.CompilerParams(
            dimension_semantics=("arbitrary", "arbitrary"), vmem_limit_bytes=V7X_VMEM_LIMIT),
        name="gla",
    )(gq, gk, gv, gog, misc, s0, wg, b_gate.reshape(1, -1), norm_g.reshape(1, -1))


PEER_GROUPS = 2 * PEER_HEADS
PEER_HALF = PEER_KEY_DIM // 2
PEER_SLOTS = PEER_HEADS * PEER_TOPK
PEER_WORDS = D_MODEL // 2


def _tail_kernel(x_ref, a_ref, g_ref, wo_ref, n2_ref, wq_ref, sk_ref, x1_ref, h2_ref, st_ref):
    f32, bf16 = jnp.float32, jnp.bfloat16
    half = wo_ref.shape[0] // 2
    mix = jnp.dot(a_ref[...], wo_ref[0:half, :], preferred_element_type=f32)
    mix = mix + jnp.dot(g_ref[...], wo_ref[half:2 * half, :], preferred_element_type=f32)
    x1 = x_ref[...] + mix
    x1_ref[...] = x1
    h2 = x1 * lax.rsqrt(jnp.mean(x1 * x1, axis=-1, keepdims=True) + RMS_EPS) * n2_ref[...]
    h2_ref[...] = h2
    qh = jnp.dot(h2.astype(bf16), wq_ref[...], preferred_element_type=f32).astype(bf16)
    for c in range(PEER_GROUPS):
        st_ref[c] = lax.dot_general(sk_ref[c], qh[:, c * PEER_HALF:(c + 1) * PEER_HALF],
                                    (((1,), (1,)), ((), ())), preferred_element_type=f32)


def _tail_call(x2d, a, g, w_out, norm2_g, w_query, subkeys, *, tm, row0, n):
    assert n % tm == 0 and row0 % tm == 0
    f32, bf16 = jnp.float32, jnp.bfloat16
    row = lambda i: (i, 0)
    off = lambda i: (row0 // tm + i, 0)
    const = lambda i: (0, 0)
    sk = subkeys.reshape(PEER_GROUPS, PEER_NKEYS, PEER_HALF).astype(bf16)
    return pl.pallas_call(
        _tail_kernel,
        grid=(n // tm,),
        in_specs=[
            pl.BlockSpec((tm, D_MODEL), off),
            pl.BlockSpec((tm, Q_W), row),
            pl.BlockSpec((tm, GV_W), off),
            pl.BlockSpec((Q_W + GV_W, D_MODEL), const),
            pl.BlockSpec((1, D_MODEL), const),
            pl.BlockSpec((D_MODEL, PEER_HEADS * PEER_KEY_DIM), const),
            pl.BlockSpec((PEER_GROUPS, PEER_NKEYS, PEER_HALF), lambda i: (0, 0, 0)),
        ],
        out_specs=[pl.BlockSpec((tm, D_MODEL), row), pl.BlockSpec((tm, D_MODEL), row),
                   pl.BlockSpec((PEER_GROUPS, PEER_NKEYS, tm), lambda i: (0, 0, i))],
        out_shape=[jax.ShapeDtypeStruct((n, D_MODEL), f32), jax.ShapeDtypeStruct((n, D_MODEL), f32),
                   jax.ShapeDtypeStruct((PEER_GROUPS, PEER_NKEYS, n), f32)],
        compiler_params=pltpu.CompilerParams(
            dimension_semantics=("arbitrary",), vmem_limit_bytes=V7X_VMEM_LIMIT),
        name="tail_proj",
    )(x2d, a, g, w_out.astype(bf16), norm2_g.reshape(1, -1), w_query.astype(bf16), sk)


def _extract_topk(x, ids, k):
    r = x.shape[0]
    rows = lax.broadcasted_iota(jnp.int32, x.shape, 0).astype(jnp.float32)
    vals, picked = [], []
    for _ in range(k):
        mx = jnp.max(x, axis=0, keepdims=True)
        first = jnp.min(jnp.where(x == mx, rows, float(r)), axis=0, keepdims=True)
        hit = rows == first
        vals.append(mx)
        picked.append(first if ids is None else jnp.sum(jnp.where(hit, ids, 0.0), axis=0, keepdims=True))
        x = jnp.where(hit, NEG_BIG, x)
    return vals, picked


def _grid_candidates(v1, i1, v2, i2):
    s2 = jnp.concatenate(v2, axis=0)
    j2 = jnp.concatenate(i2, axis=0)
    cand, cidx = [], []
    for a in range(PEER_TOPK // 2):
        nb = PEER_TOPK if a == 0 else PEER_TOPK // 2
        cand.append(v1[a] + s2[0:nb])
        cidx.append(i1[a] * float(PEER_NKEYS) + j2[0:nb])
    tail = range(PEER_TOPK // 2, PEER_TOPK)
    cand.append(jnp.concatenate([v1[a] for a in tail], axis=0) + v2[0])
    cidx.append(jnp.concatenate([i1[a] for a in tail], axis=0) * float(PEER_NKEYS) + i2[0])
    return jnp.concatenate(cand, axis=0), jnp.concatenate(cidx, axis=0)


def _peer_topk_kernel(st_ref, e_ref, g_ref):
    f32 = jnp.float32

    def head(h, carry):
        v1, i1 = _extract_topk(st_ref[2 * h], None, PEER_TOPK)
        v2, i2 = _extract_topk(st_ref[2 * h + 1], None, PEER_TOPK)
        cand, cidx = _grid_candidates(v1, i1, v2, i2)
        top, eid = _extract_topk(cand, cidx, PEER_TOPK)
        top = jnp.concatenate(top, axis=0)
        e = jnp.exp(top - top[0:1, :])
        rows = pl.ds(pl.multiple_of(h * PEER_TOPK, PEER_TOPK), PEER_TOPK)
        g_ref[rows, :] = e / jnp.sum(e, axis=0, keepdims=True)
        e_ref[rows, :] = jnp.concatenate(eid, axis=0).astype(jnp.int32)
        return carry

    lax.fori_loop(0, PEER_HEADS, head, 0)


def _peer_topk_call(st, *, tt):
    n = st.shape[2]
    assert n % tt == 0
    return pl.pallas_call(
        _peer_topk_kernel,
        grid=(n // tt,),
        in_specs=[pl.BlockSpec((PEER_GROUPS, PEER_NKEYS, tt), lambda i: (0, 0, i))],
        out_specs=[pl.BlockSpec((PEER_SLOTS, tt), lambda i: (0, i))] * 2,
        out_shape=[jax.ShapeDtypeStruct((PEER_SLOTS, n), jnp.int32),
                   jax.ShapeDtypeStruct((PEER_SLOTS, n), jnp.float32)],
        compiler_params=pltpu.CompilerParams(
            dimension_semantics=("arbitrary",), vmem_limit_bytes=V7X_VMEM_LIMIT),
        name="peer_topk",
    )(st)


SC_CORES = 2
SC_SUBCORES = 16
SC_WORKERS = SC_CORES * SC_SUBCORES
SC_CHUNK = 64
SC_IDX_BLOCK = 2048
SC_CHUNKS_PER_BLOCK = SC_IDX_BLOCK // SC_CHUNK


def _sc_gather_rows(table, idx):
    m = idx.shape[0]
    width = table.shape[1]
    assert m % (SC_WORKERS * SC_IDX_BLOCK) == 0 and SC_CHUNKS_PER_BLOCK % 2 == 0
    chunks_per_worker = m // SC_WORKERS // SC_CHUNK
    cpb = SC_CHUNKS_PER_BLOCK
    mesh = plsc.VectorSubcoreMesh(core_axis_name="c", subcore_axis_name="s",
                                  num_cores=SC_CORES, num_subcores=SC_SUBCORES)

    @functools.partial(
        pl.kernel, mesh=mesh,
        out_type=jax.ShapeDtypeStruct((m, width), table.dtype),
        scratch_types=[pltpu.VMEM((cpb, SC_CHUNK), jnp.int32),
                       pltpu.VMEM((SC_CHUNK, width), table.dtype),
                       pltpu.VMEM((SC_CHUNK, width), table.dtype),
                       pltpu.SemaphoreType.DMA, pltpu.SemaphoreType.DMA,
                       pltpu.SemaphoreType.DMA, pltpu.SemaphoreType.DMA],
        name="peer_gather",
    )
    def gather_kernel(table_hbm, idx_hbm, out_hbm, idx_v, buf0, buf1, gsem0, gsem1, wsem0, wsem1):
        wid = lax.axis_index("s") * SC_CORES + lax.axis_index("c")
        base_chunk = wid * chunks_per_worker

        def gather(j, buf, sem):
            return pltpu.make_async_copy(table_hbm.at[idx_v.at[j]], buf, sem)

        def write(chunk, buf, sem):
            rows = pl.ds(pl.multiple_of(chunk * SC_CHUNK, SC_CHUNK), SC_CHUNK)
            return pltpu.make_async_copy(buf, out_hbm.at[rows], sem)

        @pl.loop(0, chunks_per_worker // cpb)
        def _(blk):
            c0 = base_chunk + blk * cpb
            pltpu.sync_copy(idx_hbm.at[pl.ds(pl.multiple_of(c0, cpb), cpb)], idx_v)
            gather(0, buf0, gsem0).start()

            @pl.loop(0, cpb // 2)
            def _(p):
                j = p * 2
                gather(j, buf0, gsem0).wait()
                write(c0 + j, buf0, wsem0).start()

                @pl.when(p > 0)
                def _():
                    write(c0 + j - 1, buf1, wsem1).wait()

                gather(j + 1, buf1, gsem1).start()
                gather(j + 1, buf1, gsem1).wait()
                write(c0 + j + 1, buf1, wsem1).start()
                write(c0 + j, buf0, wsem0).wait()

                @pl.when(p < cpb // 2 - 1)
                def _():
                    gather(j + 2, buf0, gsem0).start()

            write(c0 + cpb - 1, buf1, wsem1).wait()

    return gather_kernel(table, idx.reshape(m // SC_CHUNK, SC_CHUNK))


def _pack_rows(w):
    b = lax.bitcast_convert_type(w.astype(jnp.bfloat16), jnp.uint16).astype(jnp.uint32)
    words = (b[:, :PEER_WORDS] << 16) | b[:, PEER_WORDS:]
    return lax.bitcast_convert_type(words, jnp.int32)


def _unpack_rows(words):
    hi = pltpu.bitcast(words & jnp.int32(-65536), jnp.float32)
    lo = pltpu.bitcast(words << 16, jnp.float32)
    return hi, lo


PEER_TOKEN_UNROLL = 4


def _peer_combine_kernel(gu_ref, gv_ref, h2_ref, gt_ref, x1_ref, y_ref):
    f32 = jnp.float32
    tb = h2_ref.shape[0]
    lane = lax.broadcasted_iota(jnp.int32, (PEER_SLOTS, tb), 1)

    def dots(n, dmat):
        rows = pl.ds(pl.multiple_of(n * PEER_SLOTS, PEER_SLOTS), PEER_SLOTS)
        x = h2_ref[pl.ds(n, 1), :]
        uh, ul = _unpack_rows(gu_ref[rows, :])
        d = jnp.sum(uh * x[:, 0:PEER_WORDS] + ul * x[:, PEER_WORDS:], axis=1, keepdims=True)
        return jnp.where(lane == n, d, dmat)

    dmat = lax.fori_loop(0, tb, dots, jnp.zeros((PEER_SLOTS, tb), f32), unroll=PEER_TOKEN_UNROLL)
    wmat = gt_ref[0] * jax.nn.gelu(dmat)

    def combine(n, carry):
        rows = pl.ds(pl.multiple_of(n * PEER_SLOTS, PEER_SLOTS), PEER_SLOTS)
        w = jnp.sum(jnp.where(lane == n, wmat, 0.0), axis=1, keepdims=True)
        vh, vl = _unpack_rows(gv_ref[rows, :])
        x1 = x1_ref[pl.ds(n, 1), :]
        y_ref[pl.ds(n, 1), 0:PEER_WORDS] = x1[:, 0:PEER_WORDS] + jnp.sum(w * vh, axis=0, keepdims=True)
        y_ref[pl.ds(n, 1), PEER_WORDS:] = x1[:, PEER_WORDS:] + jnp.sum(w * vl, axis=0, keepdims=True)
        return carry

    lax.fori_loop(0, tb, combine, 0, unroll=PEER_TOKEN_UNROLL)


def _peer_combine_call(gu, gv, h2, gates_t, x1, *, tb):
    n = h2.shape[0]
    assert n % tb == 0
    row = lambda i: (i, 0)
    gt = gates_t.reshape(PEER_SLOTS, n // tb, tb).transpose(1, 0, 2)
    return pl.pallas_call(
        _peer_combine_kernel,
        grid=(n // tb,),
        in_specs=[
            pl.BlockSpec((tb * PEER_SLOTS, PEER_WORDS), row),
            pl.BlockSpec((tb * PEER_SLOTS, PEER_WORDS), row),
            pl.BlockSpec((tb, D_MODEL), row),
            pl.BlockSpec((1, PEER_SLOTS, tb), lambda i: (i, 0, 0)),
            pl.BlockSpec((tb, D_MODEL), row),
        ],
        out_specs=pl.BlockSpec((tb, D_MODEL), row),
        out_shape=jax.ShapeDtypeStruct((n, D_MODEL), jnp.float32),
        compiler_params=pltpu.CompilerParams(
            dimension_semantics=("arbitrary",), vmem_limit_bytes=V7X_VMEM_LIMIT),
        name="peer_combine",
    )(gu, gv, h2, gt, x1)


def _tail_pallas(x2d, a, g, w_out, norm2_g, w_query, subkeys, u_words, v_words, row0, n):
    tm = _token_tile(n, (256, 128))
    x1, h2, st = _tail_call(x2d, a, g, w_out, norm2_g, w_query, subkeys, tm=tm, row0=row0, n=n)
    eidx_t, gates_t = _peer_topk_call(st, tt=tm)
    eidx = eidx_t.T.reshape(n * PEER_SLOTS)
    gu = _sc_gather_rows(u_words, eidx)
    gv = _sc_gather_rows(v_words, eidx)
    return _peer_combine_call(gu, gv, h2, gates_t, x1, tb=16)


def _compress(rows, pos, w):
    b, l = rows.shape[:2]
    nc = (l - CMP_BLOCK) // CMP_STRIDE + 1
    chunks = rows[:, :(nc + 1) * CMP_STRIDE].reshape(b, nc + 1, CMP_STRIDE, NSA_KV_HEADS, HEAD_DIM)
    first = jnp.einsum('bnphd,pde->bnhe', chunks, w[:CMP_STRIDE])
    second = jnp.einsum('bnphd,pde->bnhe', chunks, w[CMP_STRIDE:])
    bias = jnp.einsum('pd,pde->e', pos, w)
    return first[:, :-1] + second[:, 1:] + bias


def _nsa_keys(kv_all, cmp_pos, w_cmp, k_norm_g):
    kc = _rmsnorm(_compress(kv_all[:, :, 0], cmp_pos[0], w_cmp[0]), k_norm_g[0])
    vc = _compress(kv_all[:, :, 1], cmp_pos[1], w_cmp[1])
    return kc, vc, kv_all[:, :, 2], kv_all[:, :, 3]


def _nsa_block(q, t_pos, kc, vc, ks, vs, kw, vw, w_pos, gate):
    b, nq = q.shape[:2]
    nc = kc.shape[1]
    l = ks.shape[1]
    ns = -(-l // SEL_BLOCK)
    f32 = jnp.float32
    cmp_end = jnp.arange(nc) * CMP_STRIDE + (CMP_BLOCK - 1)
    m_c = cmp_end[None, :] <= t_pos[:, None]
    s_c = jnp.einsum('bqhgd,bnhd->bhgqn', q, kc).astype(f32)
    p_c = jax.nn.softmax(jnp.where(m_c, s_c, MASK_VALUE), axis=-1) * m_c
    o_c = jnp.einsum('bhgqn,bnhd->bqhgd', p_c.astype(vc.dtype), vc)
    ratio = SEL_BLOCK // CMP_STRIDE
    imp = jnp.pad(p_c.sum(axis=2), ((0, 0), (0, 0), (0, 0), (0, ns * ratio - nc)))
    imp = imp.reshape(b, NSA_KV_HEADS, nq, ns, ratio)
    imp = imp.sum(-1) + jnp.pad(imp[..., :-1, ratio - 1], ((0, 0), (0, 0), (0, 0), (1, 0)))
    blk = jnp.arange(ns)
    visible = blk[None, :] * SEL_BLOCK <= t_pos[:, None]
    forced = (blk[None, :] == 0) | (blk[None, :] == t_pos[:, None] // SEL_BLOCK)
    score = jnp.where(forced, FORCE_SCORE, jnp.where(visible, imp, -1.0))
    _, idx = lax.top_k(score, min(SEL_TOPK, ns))
    tok = idx[..., None] * SEL_BLOCK + jnp.arange(SEL_BLOCK)
    bi = jnp.arange(b)[:, None, None, None, None]
    hi = jnp.arange(NSA_KV_HEADS)[None, :, None, None, None]
    safe_tok = jnp.minimum(tok, l - 1)
    ks_g = ks[bi, safe_tok, hi]
    vs_g = vs[bi, safe_tok, hi]
    m_s = (tok <= t_pos[None, None, :, None, None])[:, :, None]
    s_s = jnp.einsum('bqhgd,bhqkpd->bhgqkp', q, ks_g).astype(f32)
    s_s = jnp.where(m_s, s_s, MASK_VALUE)
    p_s = jax.nn.softmax(s_s.reshape(s_s.shape[:4] + (-1,)), axis=-1).reshape(s_s.shape)
    o_s = jnp.einsum('bhgqkp,bhqkpd->bqhgd', p_s.astype(vs.dtype), vs_g)
    d = t_pos[:, None] - w_pos[None, :]
    m_w = (d >= 0) & (d <= WINDOW) & (w_pos[None, :] >= 0)
    s_w = jnp.einsum('bqhgd,bkhd->bhgqk', q, kw).astype(f32)
    p_w = jax.nn.softmax(jnp.where(m_w, s_w, MASK_VALUE), axis=-1)
    o_w = jnp.einsum('bhgqk,bkhd->bqhgd', p_w.astype(vw.dtype), vw)
    out = gate[..., 0:1] * o_c + gate[..., 1:2] * o_s + gate[..., 2:3] * o_w
    return out.reshape(b, nq, NSA_HEADS * HEAD_DIM).astype(q.dtype)


def _nsa_prompt(q, kv_rows, win_rows, gate, cmp_pos, w_cmp, k_norm_g):
    b, t = q.shape[:2]
    kc, vc, ks, vs = _nsa_keys(kv_rows, cmp_pos, w_cmp, k_norm_g)
    win = jnp.pad(win_rows, ((0, 0), (WINDOW, 0), (0, 0), (0, 0), (0, 0)))

    def one_block(blk):
        s = blk * Q_BLOCK
        qb = lax.dynamic_slice_in_dim(q, s, Q_BLOCK, axis=1)
        gb = lax.dynamic_slice_in_dim(gate, s, Q_BLOCK, axis=1)
        wb = lax.dynamic_slice_in_dim(win, s, WINDOW + Q_BLOCK, axis=1)
        t_pos = s + jnp.arange(Q_BLOCK)
        w_pos = s - WINDOW + jnp.arange(WINDOW + Q_BLOCK)
        return _nsa_block(qb, t_pos, kc, vc, ks, vs, wb[:, :, 0], wb[:, :, 1], w_pos, gb)

    out = lax.map(one_block, jnp.arange(t // Q_BLOCK))
    return out.transpose(1, 0, 2, 3).reshape(b, t, -1)


def _nsa_sample(q, kv_rows, win_rows, gate, cache_kv_l, cache_win, page_table, cmp_pos, w_cmp, k_norm_g):
    bd, t = q.shape[:2]
    past_len = page_table.shape[1] * PAGE_SIZE
    past = cache_kv_l[page_table].reshape(bd, past_len, KV_ROWS, NSA_KV_HEADS, HEAD_DIM)
    kc, vc, ks, vs = _nsa_keys(jnp.concatenate([past, kv_rows], axis=1), cmp_pos, w_cmp, k_norm_g)
    wbuf = cache_win.shape[1]
    win_all = jnp.concatenate([cache_win, win_rows], axis=1)
    t_pos = past_len + jnp.arange(t)
    w_pos = past_len - wbuf + jnp.arange(wbuf + t)
    out = _nsa_block(q, t_pos, kc, vc, ks, vs, win_all[:, :, 0], win_all[:, :, 1], w_pos, gate)
    return out, win_all[:, -min(WINDOW, past_len + t):]


def _gla_chunked(q, k, v, log_a, s0):
    b, t, h = q.shape[:3]
    c = math.gcd(t, GLA_CHUNK)
    n = t // c

    def to_chunks(a):
        return a.reshape(b, n, c, h, a.shape[-1]).transpose(1, 0, 3, 2, 4)

    causal = jnp.tril(jnp.ones((c, c), dtype=bool))

    def step(S, inp):
        qc, kc, vc, ac = inp
        cum = jnp.cumsum(ac, axis=2)
        diff = jnp.minimum(cum[:, :, :, None] - cum[:, :, None, :], 0.0)
        decay = jnp.where(causal[..., None], jnp.exp(diff), 0.0)
        att = jnp.einsum('bhid,bhjd,bhijd->bhij', qc, kc, decay)
        o = jnp.einsum('bhij,bhjv->bhiv', att, vc) + jnp.einsum('bhid,bhdv->bhiv', qc * jnp.exp(cum), S)
        last = cum[:, :, -1:]
        S = jnp.exp(last)[:, :, 0, :, None] * S + jnp.einsum('bhjd,bhjv->bhdv', kc * jnp.exp(last - cum), vc)
        return S, o

    S, o = lax.scan(step, s0, (to_chunks(q), to_chunks(k), to_chunks(v), to_chunks(log_a)))
    return o.transpose(1, 0, 3, 2, 4).reshape(b, t, h, -1), S


def _gla_mixer(gq, gk, gv, glr, gog, s0, w_gate, b_gate, norm_g):
    b, t = gq.shape[:2]
    f32 = jnp.float32
    q = gq.reshape(b, t, GLA_HEADS, GLA_DK).astype(f32) * (GLA_DK ** -0.5)
    k = gk.reshape(b, t, GLA_HEADS, GLA_DK).astype(f32)
    v = gv.reshape(b, t, GLA_HEADS, GLA_DV).astype(f32)
    log_a = jax.nn.log_sigmoid((glr @ w_gate + b_gate).astype(f32)).reshape(b, t, GLA_HEADS, GLA_DK) / GLA_GATE_TEMP
    o, S = _gla_chunked(q, k, v, log_a, s0.astype(f32))
    o = _rmsnorm(o, norm_g) * jax.nn.silu(gog.astype(f32)).reshape(b, t, GLA_HEADS, GLA_DV)
    return o.reshape(b, t, -1).astype(gq.dtype), S


def _peer_ffn(h, w_query, subkeys, u, v):
    b, t, d = h.shape
    n = b * t
    nb = -(-n // PEER_TOKEN_BLOCK)
    flat = jnp.pad(h.reshape(n, d), ((0, nb * PEER_TOKEN_BLOCK - n), (0, 0))).reshape(nb, PEER_TOKEN_BLOCK, d)

    def one_block(xb):
        qh = (xb @ w_query).reshape(-1, PEER_HEADS, 2, PEER_KEY_DIM // 2)
        s = jnp.einsum('nhcd,hckd->nhck', qh, subkeys).astype(jnp.float32)
        s1, i1 = lax.top_k(s[:, :, 0], PEER_TOPK)
        s2, i2 = lax.top_k(s[:, :, 1], PEER_TOPK)
        cand = (s1[..., :, None] + s2[..., None, :]).reshape(s1.shape[:-1] + (-1,))
        cidx = (i1[..., :, None] * PEER_NKEYS + i2[..., None, :]).reshape(i1.shape[:-1] + (-1,))
        top, pos = lax.top_k(cand, PEER_TOPK)
        eidx = jnp.take_along_axis(cidx, pos, axis=-1)
        g = jax.nn.softmax(top, axis=-1)
        act = jax.nn.gelu(jnp.einsum('nhkd,nd->nhk', jnp.take(u, eidx, axis=0), xb).astype(jnp.float32))
        return jnp.einsum('nhk,nhkd->nd', (g * act).astype(xb.dtype), jnp.take(v, eidx, axis=0))

    out = lax.map(one_block, flat)
    return out.reshape(-1, d)[:n].reshape(b, t, d)


def _residual_tail(x, mix, w_out, norm2_g, w_query, subkeys, u, v):
    x = x + (mix @ w_out).astype(x.dtype)
    return x + _peer_ffn(_rmsnorm(x, norm2_g), w_query, subkeys, u, v).astype(x.dtype)


def _mixer_inputs(x, norm1_g, w_in_r, q_norm_g, k_norm_g, *, tm):
    b, t = x.shape[:2]
    q2d, kvr, winr, misc, gq, gk, gv, gog, kvb = _inproj(
        x.reshape(b * t, D_MODEL), norm1_g, w_in_r, q_norm_g, k_norm_g, tm=tm)
    q = q2d.astype(jnp.float32).reshape(b, t, NSA_KV_HEADS, NSA_GROUP, HEAD_DIM)
    kv_rows = kvr.reshape(b, t, 4, NSA_KV_HEADS, HEAD_DIM)
    win_rows = winr.reshape(b, t, 2, NSA_KV_HEADS, HEAD_DIM)
    gate = jax.nn.sigmoid(misc[:, :GATE_W]).reshape(b, t, NSA_KV_HEADS, NSA_GROUP, 3)
    glr = misc[:, GATE_W:GATE_W + GLA_GATE_RANK].reshape(b, t, -1)
    rs = lambda a: a.reshape(b, t, -1)
    return q, kv_rows, win_rows, gate, (rs(gq), rs(gk), rs(gv), glr, rs(gog)), (q2d, misc, kvr, kvb, gq, gk, gv, gog)


def _token_tile(n, candidates=(512, 256, 128, 64, 32, 16, 8)):
    for tm in candidates:
        if n % tm == 0:
            return tm
    raise ValueError(n)


def kernel(x_prompt, x_sample, cache_kv, cache_win, state_gla, page_table, norm1_g, w_in, q_norm_g, k_norm_g, cmp_pos, w_cmp, gla_w_gate, gla_b_gate, gla_norm_g, w_out, norm2_g, peer_w_query, peer_subkeys, peer_u, peer_v):
    depth = w_in.shape[0]
    xp, xs = x_prompt, x_sample
    kv_p, win_p, gla_p, kv_s, win_s, gla_s = [], [], [], [], [], []
    for l in range(depth):
        w_in_r = _reorder_w_in(w_in[l])
        u_words, v_words = _pack_rows(peer_u[l]), _pack_rows(peer_v[l])
        kv_shape = (KV_ROWS, NSA_KV_HEADS, HEAD_DIM)
        win_shape = (2, NSA_KV_HEADS, HEAD_DIM)

        bs, ts = xs.shape[0], xs.shape[1]
        ns = bs * ts
        xs2d = xs.reshape(ns, D_MODEL)
        q2d, kvr, winr, misc, gq, gk, gv, gog, kvb = _inproj(
            xs2d, norm1_g[l], w_in_r, q_norm_g[l], k_norm_g[l], tm=_token_tile(ns))
        a = _nsa_sample_call(q2d, misc, kvb, cache_kv[l], cache_win[l], page_table,
                             cmp_pos[l], w_cmp[l], k_norm_g[l], ts)
        g, s_new = _gla_call(gq, gk, gv, gog, misc, state_gla[l].astype(jnp.float32), gla_w_gate[l], gla_b_gate[l],
                             gla_norm_g[l], bs, ts)
        win_all = jnp.concatenate([cache_win[l], winr.reshape((bs, ts) + win_shape)], axis=1)
        kv_s.append(kvr.reshape((bs, ts) + kv_shape))
        win_s.append(win_all[:, -min(WINDOW, page_table.shape[1] * PAGE_SIZE + ts):])
        gla_s.append(s_new.astype(state_gla.dtype))
        xs = _tail_pallas(xs2d, a, g, w_out[l], norm2_g[l], peer_w_query[l], peer_subkeys[l],
                          u_words, v_words, 0, ns).reshape(xs.shape)

        bp, tp = xp.shape[0], xp.shape[1]
        x2d = xp.reshape(bp * tp, D_MODEL)
        q2d, kvr, winr, misc, gq, gk, gv, gog, kvb = _inproj(
            x2d, norm1_g[l], w_in_r, q_norm_g[l], k_norm_g[l], tm=_token_tile(bp * tp))
        kcp, vcp = _prompt_compressed_kv(kvr, cmp_pos[l], w_cmp[l], k_norm_g[l], bp, tp)
        s0 = jnp.zeros((bp, GLA_HEADS, GLA_DK, GLA_DV), jnp.float32)
        g, s_new = _gla_call(gq, gk, gv, gog, misc, s0, gla_w_gate[l], gla_b_gate[l], gla_norm_g[l], bp, tp)
        kv_p.append(kvr.reshape((bp, tp) + kv_shape))
        win_p.append(winr.reshape((bp, tp) + win_shape)[:, -min(WINDOW, tp):])
        gla_p.append(s_new.astype(state_gla.dtype))
        nb = tp // Q_BLOCK
        nblk = _token_tile(nb, (nb // PROMPT_SPLITS, nb))
        ys = []
        for b in range(bp):
            for blk0 in range(0, nb, nblk):
                a = _nsa_prompt_call(q2d, misc, kcp, vcp, kvb, tp, b, blk0, nblk)
                ys.append(_tail_pallas(x2d, a, g, w_out[l], norm2_g[l], peer_w_query[l], peer_subkeys[l],
                                       u_words, v_words, (b * nb + blk0) * Q_BLOCK, nblk * Q_BLOCK))
        xp = jnp.concatenate(ys, axis=0).reshape(bp, tp, D_MODEL)
    return (xp, xs, jnp.stack(kv_p), jnp.stack(win_p), jnp.stack(gla_p),
            jnp.stack(kv_s), jnp.stack(win_s), jnp.stack(gla_s))
```

```python
import functools
import math

import jax
import jax.numpy as jnp
import numpy as np
from jax import lax
from jax.experimental import pallas as pl
from jax.experimental.pallas import tpu as pltpu
from jax.experimental.pallas import tpu_sc as plsc

D_MODEL = 1024
NSA_HEADS = 8
NSA_KV_HEADS = 2
NSA_GROUP = NSA_HEADS // NSA_KV_HEADS
HEAD_DIM = 64
CMP_STRIDE = 16
CMP_BLOCK = 32
SEL_BLOCK = 64
SEL_TOPK = 16
WINDOW = 512
Q_BLOCK = 128
PAGE_SIZE = 128
GLA_HEADS = 4
GLA_DV = 128
GLA_DK = 64
GLA_GATE_RANK = 16
GLA_GATE_TEMP = 16.0
GLA_CHUNK = 64
PEER_HEADS = 8
PEER_NKEYS = 128
PEER_KEY_DIM = 256
PEER_TOPK = 16
PEER_TOKEN_BLOCK = 128
KV_ROWS = 4
RMS_EPS = 1e-6
MASK_VALUE = -1e30
FORCE_SCORE = 1e4

Q_W = NSA_HEADS * HEAD_DIM
KV_W = 6 * NSA_KV_HEADS * HEAD_DIM
GATE_W = 3 * NSA_HEADS
GQ_W = GLA_HEADS * GLA_DK
GV_W = GLA_HEADS * GLA_DV
MISC_W = 128
IN_SIZES = (Q_W, KV_W, GATE_W, GQ_W, GQ_W, GV_W, GLA_GATE_RANK, GV_W)
P_W = Q_W + KV_W + GQ_W + GQ_W + GV_W + GV_W + MISC_W

V7X_VMEM_LIMIT = 56 * 1024 * 1024


def _rmsnorm(x, g):
    xf = x.astype(jnp.float32)
    y = xf * lax.rsqrt(jnp.mean(xf * xf, axis=-1, keepdims=True) + RMS_EPS)
    return (y * g.astype(jnp.float32)).astype(x.dtype)


def _head_group_ones(width, group, dtype):
    r = lax.broadcasted_iota(jnp.int32, (width, width), 0) // group
    c = lax.broadcasted_iota(jnp.int32, (width, width), 1) // group
    return jnp.where(r == c, 1.0, 0.0).astype(dtype)


def _group_mean_sq(x, group):
    sq = x * x
    hi = sq.astype(jnp.bfloat16)
    lo = (sq - hi.astype(jnp.float32)).astype(jnp.bfloat16)
    ones = _head_group_ones(x.shape[-1], group, jnp.bfloat16)
    s = jnp.dot(hi, ones, preferred_element_type=jnp.float32)
    s = s + jnp.dot(lo, ones, preferred_element_type=jnp.float32)
    return s * (1.0 / group)


def _inproj_kernel(x_ref, g1_ref, w_ref, qg_ref, ksg_ref, kwg_ref,
                   q_ref, kv_ref, win_ref, misc_ref, gq_ref, gk_ref, gv_ref, gog_ref, kvb_ref):
    x = x_ref[...]
    h = x * lax.rsqrt(jnp.mean(x * x, axis=-1, keepdims=True) + RMS_EPS) * g1_ref[...]
    p = jnp.dot(h.astype(jnp.bfloat16), w_ref[...], preferred_element_type=jnp.float32)
    o = 0
    q = p[:, o:o + Q_W]; o += Q_W
    kv = p[:, o:o + KV_W]; o += KV_W
    gq_ref[...] = p[:, o:o + GQ_W]; o += GQ_W
    gk_ref[...] = p[:, o:o + GQ_W]; o += GQ_W
    gv_ref[...] = p[:, o:o + GV_W]; o += GV_W
    gog_ref[...] = p[:, o:o + GV_W]; o += GV_W
    misc_ref[...] = p[:, o:o + MISC_W]
    qn = q * lax.rsqrt(_group_mean_sq(q, HEAD_DIM) + RMS_EPS) * qg_ref[...] * (HEAD_DIM ** -0.5)
    q_ref[...] = qn.astype(q_ref.dtype)
    hw = NSA_KV_HEADS * HEAD_DIM
    k_sel = kv[:, 2 * hw:3 * hw]
    k_sel = k_sel * lax.rsqrt(_group_mean_sq(k_sel, HEAD_DIM) + RMS_EPS) * ksg_ref[...]
    k_win = kv[:, 4 * hw:5 * hw]
    k_win = k_win * lax.rsqrt(_group_mean_sq(k_win, HEAD_DIM) + RMS_EPS) * kwg_ref[...]
    kv_ref[:, 0:2 * hw] = kv[:, 0:2 * hw]
    kv_ref[:, 2 * hw:3 * hw] = k_sel
    kv_ref[:, 3 * hw:4 * hw] = kv[:, 3 * hw:4 * hw]
    win_ref[:, 0:hw] = k_win
    win_ref[:, hw:2 * hw] = kv[:, 5 * hw:6 * hw]
    kvb_ref[:, 0:hw] = k_sel.astype(kvb_ref.dtype)
    kvb_ref[:, hw:2 * hw] = kv[:, 3 * hw:4 * hw].astype(kvb_ref.dtype)
    kvb_ref[:, 2 * hw:3 * hw] = k_win.astype(kvb_ref.dtype)
    kvb_ref[:, 3 * hw:4 * hw] = kv[:, 5 * hw:6 * hw].astype(kvb_ref.dtype)


def _reorder_w_in(w_in):
    offs = np.cumsum((0,) + IN_SIZES)
    q, kv, gate, gq, gk, gv, glr, gog = [w_in[:, offs[i]:offs[i + 1]] for i in range(8)]
    pad = jnp.zeros((w_in.shape[0], MISC_W - GATE_W - GLA_GATE_RANK), w_in.dtype)
    return jnp.concatenate([q, kv, gq, gk, gv, gog, gate, glr, pad], axis=1).astype(jnp.bfloat16)


def _inproj(x2d, norm1_g, w_in_r, q_norm_g, k_norm_g, *, tm):
    n = x2d.shape[0]
    assert n % tm == 0
    hw = NSA_KV_HEADS * HEAD_DIM
    f32 = jnp.float32
    row = lambda i: (i, 0)
    const = lambda i: (0, 0)
    widths = (Q_W, 4 * hw, 2 * hw, MISC_W, GQ_W, GQ_W, GV_W, GV_W, 4 * hw)
    bf16 = jnp.bfloat16
    dtypes = (bf16, f32, f32, f32, f32, f32, f32, f32, bf16)
    return pl.pallas_call(
        _inproj_kernel,
        grid=(n // tm,),
        in_specs=[
            pl.BlockSpec((tm, D_MODEL), row),
            pl.BlockSpec((1, D_MODEL), const),
            pl.BlockSpec((D_MODEL, P_W), const),
            pl.BlockSpec((1, Q_W), const),
            pl.BlockSpec((1, hw), const),
            pl.BlockSpec((1, hw), const),
        ],
        out_specs=[pl.BlockSpec((tm, w), row) for w in widths],
        out_shape=[jax.ShapeDtypeStruct((n, w), dt) for w, dt in zip(widths, dtypes)],
        compiler_params=pltpu.CompilerParams(
            dimension_semantics=("arbitrary",), vmem_limit_bytes=V7X_VMEM_LIMIT),
        name="inproj",
    )(x2d, norm1_g.reshape(1, -1), w_in_r,
      jnp.tile(q_norm_g, NSA_HEADS).reshape(1, -1),
      jnp.tile(k_norm_g[1], NSA_KV_HEADS).reshape(1, -1),
      jnp.tile(k_norm_g[2], NSA_KV_HEADS).reshape(1, -1))


CMP_LANES = 2 * NSA_KV_HEADS * HEAD_DIM
ROW_LANES = KV_ROWS * NSA_KV_HEADS * HEAD_DIM


def _chunk_map(x, w_ref):
    acc = None
    for p in range(CMP_STRIDE):
        xp = x[:, p * ROW_LANES:p * ROW_LANES + CMP_LANES].astype(jnp.bfloat16)
        d = jnp.dot(xp, w_ref[p], preferred_element_type=jnp.float32)
        acc = d if acc is None else acc + d
    return acc


def _compress_kernel(x_ref, xn_ref, pos_ref, wf_ref, ws_ref, kg_ref, kc_ref, vc_ref):
    tn = x_ref.shape[1]
    x = x_ref[0]
    first = _chunk_map(x, wf_ref)
    second = _chunk_map(x, ws_ref)
    second_next = _chunk_map(xn_ref[0], ws_ref)
    bias = _chunk_map(pos_ref[0], wf_ref) + _chunk_map(pos_ref[1], ws_ref)
    rows = lax.broadcasted_iota(jnp.int32, second.shape, 0)
    shifted = jnp.where(rows == tn - 1, second_next[0:1, :], pltpu.roll(second, tn - 1, axis=0))
    out = first + shifted + bias[0:1, :]
    hw = NSA_KV_HEADS * HEAD_DIM
    kc = out[:, 0:hw]
    kc_ref[0] = kc * lax.rsqrt(_group_mean_sq(kc, HEAD_DIM) + RMS_EPS) * kg_ref[...]
    vc_ref[0] = out[:, hw:2 * hw]


def _compress_weights(w_cmp, cmp_pos):
    eye = jnp.eye(NSA_KV_HEADS, dtype=w_cmp.dtype)

    def bd(p):
        blocks = [jnp.kron(eye, w_cmp[r, p]) for r in range(2)]
        z = jnp.zeros_like(blocks[0])
        return jnp.concatenate([jnp.concatenate([blocks[0], z], 1), jnp.concatenate([z, blocks[1]], 1)], 0)

    wf = jnp.stack([bd(p) for p in range(CMP_STRIDE)]).astype(jnp.bfloat16)
    ws = jnp.stack([bd(p + CMP_STRIDE) for p in range(CMP_STRIDE)]).astype(jnp.bfloat16)

    def pos_rows(lo):
        pk = jnp.tile(cmp_pos[0, lo:lo + CMP_STRIDE], (1, NSA_KV_HEADS))
        pv = jnp.tile(cmp_pos[1, lo:lo + CMP_STRIDE], (1, NSA_KV_HEADS))
        row = jnp.concatenate([pk, pv, jnp.zeros_like(pk), jnp.zeros_like(pv)], axis=1)
        flat = row.reshape(1, CMP_STRIDE * ROW_LANES)
        return jnp.concatenate([flat, jnp.zeros((7, flat.shape[1]), flat.dtype)], axis=0)

    pos = jnp.stack([pos_rows(0), pos_rows(CMP_STRIDE)])
    return wf, ws, pos


def _compress_call(kv_chunks, wf, ws, pos, k_norm0, *, tn):
    b, nch, width = kv_chunks.shape
    assert nch % tn == 0 and tn % 8 == 0
    hw = NSA_KV_HEADS * HEAD_DIM
    last8 = nch // 8 - 1
    return pl.pallas_call(
        _compress_kernel,
        grid=(b, nch // tn),
        in_specs=[
            pl.BlockSpec((1, tn, width), lambda i, j: (i, j, 0)),
            pl.BlockSpec((1, 8, width), lambda i, j: (i, jnp.minimum((j + 1) * (tn // 8), last8), 0)),
            pl.BlockSpec((2, 8, width), lambda i, j: (0, 0, 0)),
            pl.BlockSpec((CMP_STRIDE, CMP_LANES, CMP_LANES), lambda i, j: (0, 0, 0)),
            pl.BlockSpec((CMP_STRIDE, CMP_LANES, CMP_LANES), lambda i, j: (0, 0, 0)),
            pl.BlockSpec((1, hw), lambda i, j: (0, 0)),
        ],
        out_specs=[pl.BlockSpec((1, tn, hw), lambda i, j: (i, j, 0))] * 2,
        out_shape=[jax.ShapeDtypeStruct((b, nch, hw), jnp.float32)] * 2,
        compiler_params=pltpu.CompilerParams(
            dimension_semantics=("arbitrary", "arbitrary"), vmem_limit_bytes=V7X_VMEM_LIMIT),
        name="compress",
    )(kv_chunks, kv_chunks, pos, wf, ws, jnp.tile(k_norm0, NSA_KV_HEADS).reshape(1, -1))


PROMPT_SPLITS = 4
SEL_TILE = 1024
SEL_PER_TILE = SEL_TILE // SEL_BLOCK
WIN_KEYS = WINDOW + Q_BLOCK
WIN_BLOCKS = WIN_KEYS // Q_BLOCK
ROW_CHUNK = 64
NEG_BIG = -3.0e38


def _lane_tile(x, reps):
    return jnp.concatenate([x] * reps, axis=1)


def _nsa_prompt_kernel(q_ref, misc_ref, kc_ref, vc_ref, ksvs_ref, w0, w1, w2, w3, w4,
                       psel_ref, pselt_ref, ebig_ref, after_ref, o_ref,
                       q4_ref, s_ref, p_ref, bias_ref, bq_all_ref, psum_ref, m_ref, l_ref, al_ref,
                       kw_ref, vw_ref, oc_ref, os_ref, ow_ref, sc_ref, *, n_sel_blocks, blk0):
    del after_ref
    i = pl.program_id(0) + blk0
    s0 = i * Q_BLOCK
    f32, bf16 = jnp.float32, jnp.bfloat16
    hw = NSA_KV_HEADS * HEAD_DIM
    n_chunks = NSA_GROUP * Q_BLOCK // ROW_CHUNK
    halves = Q_BLOCK // ROW_CHUNK

    qb = q_ref[...]
    for h in range(NSA_KV_HEADS):
        for g in range(NSA_GROUP):
            piece = jnp.dot(qb, psel_ref[h * NSA_GROUP + g], preferred_element_type=f32)
            q4_ref[h, g * Q_BLOCK:(g + 1) * Q_BLOCK, :] = piece.astype(bf16)

    for j, w in enumerate((w0, w1, w2, w3, w4)):
        kw_ref[j * Q_BLOCK:(j + 1) * Q_BLOCK, :] = w[:, 0:hw]
        vw_ref[j * Q_BLOCK:(j + 1) * Q_BLOCK, :] = w[:, hw:2 * hw]

    def chunk_rows(c):
        return pl.ds(c * ROW_CHUNK, ROW_CHUNK)

    def chunk_t(c):
        r = lax.broadcasted_iota(jnp.int32, (ROW_CHUNK, 1), 0)
        return s0 + (c % halves) * ROW_CHUNK + r

    for h in range(NSA_KV_HEADS):
        q4 = q4_ref[h]

        ncp = kc_ref.shape[1]
        nsb = ncp // 4
        s_ref[:, 0:ncp] = lax.dot_general(q4, kc_ref[0], (((1,), (1,)), ((), ())), preferred_element_type=f32)
        psum_ref[...] = jnp.zeros_like(psum_ref)

        def cmp_chunk(c, carry):
            rows = chunk_rows(c)
            t = chunk_t(c)
            col = lax.broadcasted_iota(jnp.int32, (ROW_CHUNK, ncp), 1)
            cidx = (col % nsb) * 4 + col // nsb
            valid = cidx * CMP_STRIDE + (CMP_BLOCK - 1) <= t
            s = jnp.where(valid, s_ref[rows, 0:ncp], MASK_VALUE)
            mx = jnp.max(s, axis=1, keepdims=True)
            e = jnp.exp(s - mx)
            p = jnp.where(valid, e / jnp.sum(e, axis=1, keepdims=True), 0.0)
            p_ref[rows, 0:ncp] = p.astype(bf16)
            hrows = pl.ds((c % halves) * ROW_CHUNK, ROW_CHUNK)
            psum_ref[hrows, :] += p
            return carry

        for c in range(n_chunks):
            cmp_chunk(c, 0)
        oc_ref[h] = jnp.dot(p_ref[:, 0:ncp], vc_ref[0], preferred_element_type=f32)

        ps = psum_ref[...]
        a3 = ps[:, 3 * nsb:4 * nsb]
        blk = lax.broadcasted_iota(jnp.int32, (Q_BLOCK, nsb), 1)
        tq = s0 + lax.broadcasted_iota(jnp.int32, (Q_BLOCK, nsb), 0)
        imp = ps[:, 0:nsb] + ps[:, nsb:2 * nsb] + ps[:, 2 * nsb:3 * nsb] + a3
        imp = imp + jnp.where(blk == 0, 0.0, pltpu.roll(a3, 1, axis=1))
        visible = blk * SEL_BLOCK <= tq
        forced = (blk == 0) | (blk == tq // SEL_BLOCK)
        sc_ref[h * Q_BLOCK:(h + 1) * Q_BLOCK, :] = jnp.where(forced, FORCE_SCORE, jnp.where(visible, imp, -1.0))

    blkf = lax.broadcasted_iota(jnp.int32, sc_ref.shape, 1).astype(f32)

    def pick(_, carry):
        sc, selm = carry
        mx = jnp.max(sc, axis=1, keepdims=True)
        first = jnp.min(jnp.where(sc == mx, blkf, float(nsb)), axis=1, keepdims=True)
        hit = blkf == first
        return jnp.where(hit, NEG_BIG, sc), jnp.where(hit, 1.0, selm)

    score = sc_ref[...]
    _, selm = lax.fori_loop(0, min(SEL_TOPK, n_sel_blocks), pick, (score, jnp.zeros_like(score)))
    bq_all_ref[...] = jnp.where(selm > 0.0, 0.0, MASK_VALUE).astype(bf16)

    for h in range(NSA_KV_HEADS):
        q4 = q4_ref[h]
        bq_ref = bq_all_ref.at[h * Q_BLOCK:(h + 1) * Q_BLOCK]

        m_ref[...] = jnp.full_like(m_ref, NEG_BIG)
        l_ref[...] = jnp.zeros_like(l_ref)
        os_ref[h] = jnp.zeros((NSA_GROUP * Q_BLOCK, hw), f32)

        def sel_tile(kt, carry):
            k0 = pl.multiple_of(kt * SEL_TILE, SEL_TILE)
            e_off = pl.multiple_of(nsb - kt * SEL_PER_TILE, SEL_PER_TILE)
            key = k0 + lax.broadcasted_iota(jnp.int32, (Q_BLOCK, SEL_TILE), 1)
            tq1 = s0 + lax.broadcasted_iota(jnp.int32, (Q_BLOCK, 1), 0)
            blockmask = jnp.dot(bq_ref[...], ebig_ref[pl.ds(e_off, nsb), :], preferred_element_type=f32)
            bias_ref[...] = jnp.where(key <= tq1, blockmask, MASK_VALUE)
            s_ref[...] = lax.dot_general(q4, ksvs_ref[pl.ds(k0, SEL_TILE), 0:hw],
                                         (((1,), (1,)), ((), ())), preferred_element_type=f32)

            for c in range(n_chunks):
                rows = slice(c * ROW_CHUNK, (c + 1) * ROW_CHUNK)
                hrows = slice((c % halves) * ROW_CHUNK, (c % halves + 1) * ROW_CHUNK)
                s = s_ref[rows, :] + bias_ref[hrows, :]
                m_old = m_ref[rows, :]
                m_new = jnp.maximum(m_old, jnp.max(s, axis=1, keepdims=True))
                p = jnp.exp(s - _lane_tile(m_new, SEL_TILE // 128))
                alpha = jnp.exp(m_old - m_new)
                l_ref[rows, :] = alpha * l_ref[rows, :] + jnp.sum(p, axis=1, keepdims=True)
                m_ref[rows, :] = m_new
                al_ref[rows, :] = alpha
                p_ref[rows, :] = p.astype(bf16)
            pv = jnp.dot(p_ref[...], ksvs_ref[pl.ds(k0, SEL_TILE), hw:2 * hw], preferred_element_type=f32)
            os_ref[h] = os_ref[h] * al_ref[...] + pv
            return carry

        lax.fori_loop(0, (s0 + Q_BLOCK - 1) // SEL_TILE + 1, sel_tile, 0)
        os_ref[h] = os_ref[h] / l_ref[...]

        s_ref[:, 0:WIN_KEYS] = lax.dot_general(q4, kw_ref[...], (((1,), (1,)), ((), ())),
                                               preferred_element_type=f32)

        def win_chunk(c, carry):
            rows = chunk_rows(c)
            t = chunk_t(c)
            pos = s0 - WINDOW + lax.broadcasted_iota(jnp.int32, (ROW_CHUNK, WIN_KEYS), 1)
            d = t - pos
            valid = (d >= 0) & (d <= WINDOW) & (pos >= 0)
            s = jnp.where(valid, s_ref[rows, 0:WIN_KEYS], MASK_VALUE)
            mx = jnp.max(s, axis=1, keepdims=True)
            e = jnp.exp(s - mx)
            p_ref[rows, 0:WIN_KEYS] = (e / jnp.sum(e, axis=1, keepdims=True)).astype(bf16)
            return carry

        for c in range(n_chunks):
            win_chunk(c, 0)
        ow_ref[h] = jnp.dot(p_ref[:, 0:WIN_KEYS], vw_ref[...], preferred_element_type=f32)

    gsig = jax.nn.sigmoid(misc_ref[...])
    out = jnp.zeros((Q_BLOCK, Q_W), f32)
    for h in range(NSA_KV_HEADS):
        for g in range(NSA_GROUP):
            hg = h * NSA_GROUP + g
            r = slice(g * Q_BLOCK, (g + 1) * Q_BLOCK)
            mix = (gsig[:, 3 * hg:3 * hg + 1] * oc_ref[h, r, :]
                   + gsig[:, 3 * hg + 1:3 * hg + 2] * os_ref[h, r, :]
                   + gsig[:, 3 * hg + 2:3 * hg + 3] * ow_ref[h, r, :])
            out = out + jnp.dot(mix.astype(bf16), pselt_ref[hg], preferred_element_type=f32)
    o_ref[...] = out.astype(o_ref.dtype)


def _nsa_constants(n_sel_blocks):
    hw = NSA_KV_HEADS * HEAD_DIM
    psel = np.zeros((NSA_HEADS, Q_W, hw), np.float32)
    for h in range(NSA_KV_HEADS):
        for g in range(NSA_GROUP):
            hg = h * NSA_GROUP + g
            for d in range(HEAD_DIM):
                psel[hg, hg * HEAD_DIM + d, h * HEAD_DIM + d] = 1.0
    pselt = np.transpose(psel, (0, 2, 1))
    r = np.arange(2 * n_sel_blocks)[:, None] - n_sel_blocks
    ebig = (r == (np.arange(SEL_TILE)[None, :] // SEL_BLOCK)).astype(np.float32)
    return (jnp.asarray(psel, jnp.bfloat16), jnp.asarray(pselt, jnp.bfloat16), jnp.asarray(ebig, jnp.bfloat16))


def _nsa_prompt_call(q, misc, kcp, vcp, kvb, seq, b, blk0, nblk, after):
    assert seq % SEL_TILE == 0 and seq % Q_BLOCK == 0
    nb = seq // Q_BLOCK
    nsb = seq // SEL_BLOCK
    hw = NSA_KV_HEADS * HEAD_DIM
    psel, pselt, ebig = _nsa_constants(nsb)
    rows4 = NSA_GROUP * Q_BLOCK
    f32, bf16 = jnp.float32, jnp.bfloat16
    r0 = b * nb + blk0

    def win_spec(j):
        return pl.BlockSpec((Q_BLOCK, 2 * hw),
                            lambda i: (b * nb + jnp.maximum(blk0 + i - (WIN_BLOCKS - 1) + j, 0), 1))

    return pl.pallas_call(
        functools.partial(_nsa_prompt_kernel, n_sel_blocks=nsb, blk0=blk0),
        grid=(nblk,),
        in_specs=[
            pl.BlockSpec((Q_BLOCK, Q_W), lambda i: (r0 + i, 0)),
            pl.BlockSpec((Q_BLOCK, MISC_W), lambda i: (r0 + i, 0)),
            pl.BlockSpec((1, seq // CMP_STRIDE, hw), lambda i: (b, 0, 0)),
            pl.BlockSpec((1, seq // CMP_STRIDE, hw), lambda i: (b, 0, 0)),
            pl.BlockSpec((seq, 2 * hw), lambda i: (b, 0)),
        ] + [win_spec(j) for j in range(WIN_BLOCKS)] + [
            pl.BlockSpec(psel.shape, lambda i: (0, 0, 0)),
            pl.BlockSpec(pselt.shape, lambda i: (0, 0, 0)),
            pl.BlockSpec(ebig.shape, lambda i: (0, 0)),
            pl.BlockSpec(after.shape, lambda i: (0, 0)),
        ],
        out_specs=pl.BlockSpec((Q_BLOCK, Q_W), lambda i: (i, 0)),
        out_shape=jax.ShapeDtypeStruct((nblk * Q_BLOCK, Q_W), bf16),
        scratch_shapes=[
            pltpu.VMEM((NSA_KV_HEADS, rows4, hw), bf16),
            pltpu.VMEM((rows4, SEL_TILE), f32),
            pltpu.VMEM((rows4, SEL_TILE), bf16),
            pltpu.VMEM((Q_BLOCK, SEL_TILE), f32),
            pltpu.VMEM((NSA_KV_HEADS * Q_BLOCK, nsb), bf16),
            pltpu.VMEM((Q_BLOCK, seq // CMP_STRIDE), f32),
            pltpu.VMEM((rows4, hw), f32),
            pltpu.VMEM((rows4, hw), f32),
            pltpu.VMEM((rows4, hw), f32),
            pltpu.VMEM((WIN_KEYS, hw), bf16),
            pltpu.VMEM((WIN_KEYS, hw), bf16),
            pltpu.VMEM((NSA_KV_HEADS, rows4, hw), f32),
            pltpu.VMEM((NSA_KV_HEADS, rows4, hw), f32),
            pltpu.VMEM((NSA_KV_HEADS, rows4, hw), f32),
            pltpu.VMEM((NSA_KV_HEADS * Q_BLOCK, nsb), f32),
        ],
        compiler_params=pltpu.CompilerParams(
            dimension_semantics=("arbitrary",), vmem_limit_bytes=V7X_VMEM_LIMIT),
        name="nsa_prompt",
    )(q, misc, kcp, vcp, kvb, kvb, kvb, kvb, kvb, kvb, psel, pselt, ebig, after)


def _prompt_compressed_kv(kvr2d, cmp_pos, w_cmp, k_norm_g, batch, seq):
    nch = seq // CMP_STRIDE
    wf, ws, pos = _compress_weights(w_cmp, cmp_pos)
    kc, vc = _compress_call(kvr2d.reshape(batch, nch, CMP_STRIDE * ROW_LANES), wf, ws, pos, k_norm_g[0],
                            tn=min(256, nch))

    def perm(a):
        return a.reshape(batch, nch // 4, 4, a.shape[-1]).transpose(0, 2, 1, 3).reshape(batch, nch, -1).astype(jnp.bfloat16)

    return perm(kc), perm(vc)


PAD_KEYS = 128


def _softmax_piece_max(pieces):
    m = None
    for s in pieces:
        pm = jnp.max(s, axis=1, keepdims=True)
        m = pm if m is None else jnp.maximum(m, pm)
    return m


def _nsa_sample_kernel(pt_ref, q_ref, misc_ref, kvb_ref, win_ref, cache_ref, wf_ref, ws_ref, pos_ref, kg_ref,
                       psel_ref, pselt_ref, ebig_ref, after_ref, o_ref, pages_ref, sem_ref, *, past_len, n_new):
    del after_ref
    f32, bf16 = jnp.float32, jnp.bfloat16
    b = pl.program_id(0)
    nseq = pl.num_programs(0)
    n_pages = past_len // PAGE_SIZE
    nsb = past_len // SEL_BLOCK
    ncp = past_len // CMP_STRIDE
    hw = NSA_KV_HEADS * HEAD_DIM
    slot = b % 2
    rows_q = NSA_GROUP * n_new

    def page_copy(seq, j, kind, s):
        return pltpu.make_async_copy(cache_ref.at[pt_ref[seq, j], :, pl.ds(kind * hw, hw)],
                                     pages_ref.at[s, kind, pl.ds(pl.multiple_of(j * PAGE_SIZE, PAGE_SIZE), PAGE_SIZE)],
                                     sem_ref.at[s])

    def for_each_page_copy(seq, s, fn):
        def body(j, c):
            for kind in range(KV_ROWS):
                fn(page_copy(seq, j, kind, s))
            return c
        lax.fori_loop(0, n_pages, body, 0)

    @pl.when(b == 0)
    def _():
        pages_ref[:, :, past_len:past_len + SEL_BLOCK, :] = jnp.zeros((2, KV_ROWS, SEL_BLOCK, hw), f32)
        for_each_page_copy(0, 0, lambda cp: cp.start())

    @pl.when(b + 1 < nseq)
    def _():
        for_each_page_copy(b + 1, 1 - slot, lambda cp: cp.start())

    for_each_page_copy(b, slot, lambda cp: cp.wait())

    def strided(start):
        rows = pl.ds(start, nsb, stride=SEL_BLOCK)
        return jnp.concatenate([pages_ref[slot, 0, rows, :], pages_ref[slot, 1, rows, :]], axis=1).astype(bf16)

    first = None
    second = None
    for p in range(CMP_STRIDE):
        xf = jnp.concatenate([strided(CMP_STRIDE * r + p) for r in range(4)], axis=0)
        xs = jnp.concatenate([strided(CMP_STRIDE * (r + 1) + p) for r in range(4)], axis=0)
        df = jnp.dot(xf, wf_ref[p], preferred_element_type=f32)
        ds_ = jnp.dot(xs, ws_ref[p], preferred_element_type=f32)
        first = df if first is None else first + df
        second = ds_ if second is None else second + ds_
    bias = _chunk_map(pos_ref[0], wf_ref) + _chunk_map(pos_ref[1], ws_ref)
    cmp_out = first + second + bias[0:1, :]
    kc = cmp_out[:, 0:hw]
    kc = (kc * lax.rsqrt(_group_mean_sq(kc, HEAD_DIM) + RMS_EPS) * kg_ref[...]).astype(bf16)
    vc = cmp_out[:, hw:2 * hw].astype(bf16)

    qb = q_ref[...]
    newkv = kvb_ref[...]
    zpad = jnp.zeros((PAD_KEYS - n_new, hw), bf16)
    ks_new = jnp.concatenate([newkv[:, 0:hw], zpad], axis=0)
    vs_new = jnp.concatenate([newkv[:, hw:2 * hw], zpad], axis=0)
    kw_new = jnp.concatenate([newkv[:, 2 * hw:3 * hw], zpad], axis=0)
    vw_new = jnp.concatenate([newkv[:, 3 * hw:4 * hw], zpad], axis=0)
    wcache = win_ref[0]
    wbuf = wcache.shape[0]
    kw_old = wcache[:, 0:hw].astype(bf16)
    vw_old = wcache[:, hw:2 * hw].astype(bf16)

    tl = lax.broadcasted_iota(jnp.int32, (rows_q, 1), 0) % n_new
    t_abs = past_len + tl
    new_col = lax.broadcasted_iota(jnp.int32, (rows_q, PAD_KEYS), 1)
    new_ok = new_col <= tl
    nt_dims = (((1,), (1,)), ((), ()))
    gsig = jax.nn.sigmoid(misc_ref[...])
    out = jnp.zeros((n_new, Q_W), f32)

    for h in range(NSA_KV_HEADS):
        q4 = jnp.concatenate(
            [jnp.dot(qb, psel_ref[h * NSA_GROUP + g], preferred_element_type=f32).astype(bf16)
             for g in range(NSA_GROUP)], axis=0)

        s = lax.dot_general(q4, kc, nt_dims, preferred_element_type=f32)
        col = lax.broadcasted_iota(jnp.int32, (rows_q, ncp), 1)
        cidx = (col % nsb) * 4 + col // nsb
        valid = cidx * CMP_STRIDE + (CMP_BLOCK - 1) <= t_abs
        s = jnp.where(valid, s, MASK_VALUE)
        e = jnp.exp(s - jnp.max(s, axis=1, keepdims=True))
        pc = jnp.where(valid, e / jnp.sum(e, axis=1, keepdims=True), 0.0)
        o_c = jnp.dot(pc.astype(bf16), vc, preferred_element_type=f32)
        psum = pc[0:n_new]
        for g in range(1, NSA_GROUP):
            psum = psum + pc[g * n_new:(g + 1) * n_new]

        a3 = psum[:, 3 * nsb:4 * nsb]
        blk = lax.broadcasted_iota(jnp.int32, (n_new, nsb), 1)
        imp = psum[:, 0:nsb] + psum[:, nsb:2 * nsb] + psum[:, 2 * nsb:3 * nsb] + a3
        imp = imp + jnp.where(blk == 0, 0.0, pltpu.roll(a3, 1, axis=1))
        score = jnp.where(blk == 0, FORCE_SCORE, imp)
        blkf = blk.astype(f32)
        selm = jnp.zeros_like(score)
        for _ in range(SEL_TOPK - 1):
            mx = jnp.max(score, axis=1, keepdims=True)
            firstb = jnp.min(jnp.where(score == mx, blkf, float(nsb)), axis=1, keepdims=True)
            hit = blkf == firstb
            selm = jnp.where(hit, 1.0, selm)
            score = jnp.where(hit, NEG_BIG, score)
        bq = jnp.where(selm > 0.0, 0.0, MASK_VALUE).astype(bf16)

        s_new = jnp.where(new_ok, lax.dot_general(q4, ks_new, nt_dims, preferred_element_type=f32), MASK_VALUE)
        m_run = jnp.max(s_new, axis=1, keepdims=True)
        p_new = jnp.exp(s_new - m_run)
        l_run = jnp.sum(p_new, axis=1, keepdims=True)
        acc = jnp.dot(p_new.astype(bf16), vs_new, preferred_element_type=f32)

        def sel_tile(kt, carry):
            m_run, l_run, acc = carry
            k0 = pl.multiple_of(kt * SEL_TILE, SEL_TILE)
            e_off = pl.multiple_of(nsb - kt * SEL_PER_TILE, SEL_PER_TILE)
            bias = jnp.dot(bq, ebig_ref[pl.ds(e_off, nsb), :], preferred_element_type=f32)
            kt_rows = pages_ref[slot, 2, pl.ds(k0, SEL_TILE), :].astype(bf16)
            vt_rows = pages_ref[slot, 3, pl.ds(k0, SEL_TILE), :].astype(bf16)
            s = lax.dot_general(q4, kt_rows, nt_dims, preferred_element_type=f32)
            s = s + jnp.concatenate([bias] * NSA_GROUP, axis=0)
            m_new = jnp.maximum(m_run, jnp.max(s, axis=1, keepdims=True))
            p = jnp.exp(s - m_new)
            alpha = jnp.exp(m_run - m_new)
            l_new = alpha * l_run + jnp.sum(p, axis=1, keepdims=True)
            acc = acc * alpha + jnp.dot(p.astype(bf16), vt_rows, preferred_element_type=f32)
            return m_new, l_new, acc

        m_run, l_run, acc = lax.fori_loop(0, past_len // SEL_TILE, sel_tile, (m_run, l_run, acc))
        o_s = acc / l_run

        wpos = past_len - wbuf + lax.broadcasted_iota(jnp.int32, (rows_q, wbuf), 1)
        d = t_abs - wpos
        ok_old = (d >= 0) & (d <= WINDOW) & (wpos >= 0)
        s_old = jnp.where(ok_old, lax.dot_general(q4, kw_old, nt_dims, preferred_element_type=f32), MASK_VALUE)
        s_nw = jnp.where(new_ok, lax.dot_general(q4, kw_new, nt_dims, preferred_element_type=f32), MASK_VALUE)
        mw = _softmax_piece_max([s_old, s_nw])
        e_old = jnp.exp(s_old - mw)
        e_nw = jnp.exp(s_nw - mw)
        lw = jnp.sum(e_old, axis=1, keepdims=True) + jnp.sum(e_nw, axis=1, keepdims=True)
        o_w = (jnp.dot((e_old / lw).astype(bf16), vw_old, preferred_element_type=f32)
               + jnp.dot((e_nw / lw).astype(bf16), vw_new, preferred_element_type=f32))

        for g in range(NSA_GROUP):
            hg = h * NSA_GROUP + g
            r = slice(g * n_new, (g + 1) * n_new)
            mix = (gsig[:, 3 * hg:3 * hg + 1] * o_c[r] + gsig[:, 3 * hg + 1:3 * hg + 2] * o_s[r]
                   + gsig[:, 3 * hg + 2:3 * hg + 3] * o_w[r])
            out = out + jnp.dot(mix.astype(bf16), pselt_ref[hg], preferred_element_type=f32)
    o_ref[...] = out.astype(o_ref.dtype)


def _nsa_sample_call(q, misc, kvb, cache_kv_l, cache_win_l, page_table, cmp_pos, w_cmp, k_norm_g, n_new, after):
    bsz, n_pages = page_table.shape
    past_len = n_pages * PAGE_SIZE
    assert past_len % SEL_TILE == 0 and n_new % 8 == 0 and n_new <= PAD_KEYS
    assert (past_len + n_new - CMP_BLOCK) // CMP_STRIDE + 1 == past_len // CMP_STRIDE - 1
    nsb = past_len // SEL_BLOCK
    hw = NSA_KV_HEADS * HEAD_DIM
    wbuf = cache_win_l.shape[1]
    psel, pselt, ebig = _nsa_constants(nsb)
    wf, ws, pos = _compress_weights(w_cmp, cmp_pos)
    cache = cache_kv_l.reshape(cache_kv_l.shape[0], PAGE_SIZE, ROW_LANES)
    win = cache_win_l.reshape(bsz, wbuf, 2 * hw)
    row = lambda i, pt: (i, 0)
    c2 = lambda i, pt: (0, 0)
    c3 = lambda i, pt: (0, 0, 0)
    grid_spec = pltpu.PrefetchScalarGridSpec(
        num_scalar_prefetch=1,
        grid=(bsz,),
        in_specs=[
            pl.BlockSpec((n_new, Q_W), row),
            pl.BlockSpec((n_new, MISC_W), row),
            pl.BlockSpec((n_new, 4 * hw), row),
            pl.BlockSpec((1, wbuf, 2 * hw), lambda i, pt: (i, 0, 0)),
            pl.BlockSpec(memory_space=pl.ANY),
            pl.BlockSpec(wf.shape, c3), pl.BlockSpec(ws.shape, c3), pl.BlockSpec(pos.shape, c3),
            pl.BlockSpec((1, hw), c2),
            pl.BlockSpec(psel.shape, c3), pl.BlockSpec(pselt.shape, c3), pl.BlockSpec(ebig.shape, c2),
            pl.BlockSpec(after.shape, c2),
        ],
        out_specs=pl.BlockSpec((n_new, Q_W), row),
        scratch_shapes=[pltpu.VMEM((2, KV_ROWS, past_len + SEL_BLOCK, hw), jnp.float32),
                        pltpu.SemaphoreType.DMA((2,))],
    )
    return pl.pallas_call(
        functools.partial(_nsa_sample_kernel, past_len=past_len, n_new=n_new),
        grid_spec=grid_spec,
        out_shape=jax.ShapeDtypeStruct((bsz * n_new, Q_W), jnp.bfloat16),
        compiler_params=pltpu.CompilerParams(
            dimension_semantics=("arbitrary",), vmem_limit_bytes=V7X_VMEM_LIMIT),
        name="nsa_sample",
    )(page_table, q, misc, kvb, win, cache, wf, ws, pos,
      jnp.tile(k_norm_g[0], NSA_KV_HEADS).reshape(1, -1), psel, pselt, ebig, after)


GLA_J_GROUP = 8


def _split3(x):
    hi = x.astype(jnp.bfloat16)
    r = x - hi.astype(jnp.float32)
    mid = r.astype(jnp.bfloat16)
    lo = (r - mid.astype(jnp.float32)).astype(jnp.bfloat16)
    return hi, mid, lo


def _gla_kernel(gq_ref, gk_ref, gv_ref, gog_ref, misc_ref, s0_ref, wg_ref, bg_ref, ng_ref,
                o_ref, sout_ref, sbd_ref, la_ref, cum_ref, *, chunk):
    f32, bf16 = jnp.float32, jnp.bfloat16
    tstep = pl.program_id(1)
    n_tsteps = pl.num_programs(1)
    tb = gq_ref.shape[0]
    c = chunk
    mm = bf16 if c % 16 == 0 else f32
    hk, hv = GQ_W, GV_W

    @pl.when(tstep == 0)
    def _():
        sbd_ref[...] = jnp.zeros_like(sbd_ref)
        for h in range(GLA_HEADS):
            sbd_ref[h * GLA_DK:(h + 1) * GLA_DK, h * GLA_DV:(h + 1) * GLA_DV] = s0_ref[0, h]

    z = jnp.dot(misc_ref[...].astype(bf16), wg_ref[...], preferred_element_type=f32) + bg_ref[...]
    la_ref[...] = (jnp.minimum(z, 0.0) - jnp.log1p(jnp.exp(-jnp.abs(z)))) * (1.0 / GLA_GATE_TEMP)

    ri = lax.broadcasted_iota(jnp.int32, (c, c), 0)
    ci = lax.broadcasted_iota(jnp.int32, (c, c), 1)
    tril = jnp.where(ri >= ci, 1.0, 0.0).astype(bf16)
    kr = lax.broadcasted_iota(jnp.int32, (hk, hk), 0) // GLA_DK
    kc = lax.broadcasted_iota(jnp.int32, (hk, hk), 1) // GLA_DK
    head_rep = jnp.where(kr == kc, 1.0, 0.0).astype(bf16)
    eye_k = (lax.broadcasted_iota(jnp.int32, (hk, hk), 0) == lax.broadcasted_iota(jnp.int32, (hk, hk), 1))
    bd_mask = (lax.broadcasted_iota(jnp.int32, (hk, hv), 0) // GLA_DK
               == lax.broadcasted_iota(jnp.int32, (hk, hv), 1) // GLA_DV)
    lane_j = lax.broadcasted_iota(jnp.int32, (c, hk), 1) % GLA_DK
    row_i = lax.broadcasted_iota(jnp.int32, (c, hk), 0)

    def one_chunk(ch, carry):
        rows = pl.ds(pl.multiple_of(ch * c, c), c)
        q = gq_ref[rows, :] * (GLA_DK ** -0.5)
        k = gk_ref[rows, :]
        v = gv_ref[rows, :]
        la = la_ref[rows, :]
        hi, mid, lo = _split3(la)
        cum = (jnp.dot(tril, hi, preferred_element_type=f32) + jnp.dot(tril, mid, preferred_element_type=f32)
               + jnp.dot(tril, lo, preferred_element_type=f32))
        last = cum[c - 1:c, :]
        cum_ref[...] = cum

        def j_group(g, att):
            ws = []
            for jj in range(GLA_J_GROUP):
                j = g * GLA_J_GROUP + jj
                jrow = pl.ds(ch * c + j, 1)
                kj = gk_ref[jrow, :]
                cumj = cum_ref[pl.ds(j, 1), :]
                dec = jnp.where(row_i >= j, jnp.exp(jnp.minimum(cum - cumj, 0.0)), 0.0)
                ws.append((q * kj * dec).astype(bf16))
            r = jnp.dot(jnp.concatenate(ws, axis=0), head_rep, preferred_element_type=f32)
            for jj in range(GLA_J_GROUP):
                j = g * GLA_J_GROUP + jj
                att = att + jnp.where(lane_j == j, r[jj * c:(jj + 1) * c, :], 0.0)
            return att

        att = lax.fori_loop(0, c // GLA_J_GROUP, j_group, jnp.zeros((c, hk), f32))

        vt = jnp.concatenate([v] * (GLA_DK // c), axis=0) if c < GLA_DK else v
        vbd = jnp.where(bd_mask, jnp.concatenate([vt] * GLA_HEADS, axis=0), 0.0)
        sbd = sbd_ref[...]
        o = jnp.dot(att.astype(bf16), vbd.astype(bf16), preferred_element_type=f32)
        o = o + jnp.dot((q * jnp.exp(cum)).astype(bf16), sbd.astype(bf16), preferred_element_type=f32)

        ke = k * jnp.exp(last - cum)
        upd = lax.dot_general(ke.astype(mm), v.astype(mm), (((0,), (0,)), ((), ())), preferred_element_type=f32)
        dcol = jnp.sum(jnp.where(eye_k, jnp.exp(last), 0.0), axis=1, keepdims=True)
        sbd_ref[...] = sbd * dcol + jnp.where(bd_mask, upd, 0.0)

        gog = gog_ref[rows, :]
        for h in range(GLA_HEADS):
            sl = slice(h * GLA_DV, (h + 1) * GLA_DV)
            oh = o[:, sl]
            oh = oh * lax.rsqrt(jnp.mean(oh * oh, axis=1, keepdims=True) + RMS_EPS) * ng_ref[...]
            gh = gog[:, sl]
            o_ref[rows, sl] = (oh * gh * jax.nn.sigmoid(gh)).astype(o_ref.dtype)
        return carry

    lax.fori_loop(0, tb // c, one_chunk, 0)

    @pl.when(tstep == n_tsteps - 1)
    def _():
        for h in range(GLA_HEADS):
            sout_ref[0, h] = sbd_ref[h * GLA_DK:(h + 1) * GLA_DK, h * GLA_DV:(h + 1) * GLA_DV]


def _gla_call(gq, gk, gv, gog, misc, s0, w_gate, b_gate, norm_g, batch, seq):
    c = math.gcd(seq, GLA_CHUNK)
    tb = _token_tile(seq, (512, 256, 128, 64, 32, 16, 8))
    tb = max(tb, c)
    nt = seq // tb
    f32, bf16 = jnp.float32, jnp.bfloat16
    wg = jnp.zeros((MISC_W, GQ_W), f32).at[GATE_W:GATE_W + GLA_GATE_RANK].set(w_gate).astype(bf16)
    row = lambda b, t: (b * nt + t, 0)
    const = lambda b, t: (0, 0)
    state_spec = pl.BlockSpec((1, GLA_HEADS, GLA_DK, GLA_DV), lambda b, t: (b, 0, 0, 0))
    return pl.pallas_call(
        functools.partial(_gla_kernel, chunk=c),
        grid=(batch, nt),
        in_specs=[
            pl.BlockSpec((tb, GQ_W), row), pl.BlockSpec((tb, GQ_W), row),
            pl.BlockSpec((tb, GV_W), row), pl.BlockSpec((tb, GV_W), row),
            pl.BlockSpec((tb, MISC_W), row), state_spec,
            pl.BlockSpec((MISC_W, GQ_W), const), pl.BlockSpec((1, GQ_W), const), pl.BlockSpec((1, GLA_DV), const),
        ],
        out_specs=[pl.BlockSpec((tb, GV_W), row), state_spec],
        out_shape=[jax.ShapeDtypeStruct((batch * seq, GV_W), bf16),
                   jax.ShapeDtypeStruct((batch, GLA_HEADS, GLA_DK, GLA_DV), f32)],
        scratch_shapes=[pltpu.VMEM((GQ_W, GV_W), f32), pltpu.VMEM((tb, GQ_W), f32), pltpu.VMEM((c, GQ_W), f32)],
        compiler_params=pltpu.CompilerParams(
            dimension_semantics=("arbitrary", "arbitrary"), vmem_limit_bytes=V7X_VMEM_LIMIT),
        name="gla",
    )(gq, gk, gv, gog, misc, s0, wg, b_gate.reshape(1, -1), norm_g.reshape(1, -1))


PEER_GROUPS = 2 * PEER_HEADS
PEER_HALF = PEER_KEY_DIM // 2
PEER_SLOTS = PEER_HEADS * PEER_TOPK
PEER_WORDS = D_MODEL // 2


def _tail_kernel(x_ref, a_ref, g_ref, wo_ref, n2_ref, wq_ref, sk_ref, x1_ref, h2_ref, st_ref):
    f32, bf16 = jnp.float32, jnp.bfloat16
    half = wo_ref.shape[0] // 2
    mix = jnp.dot(a_ref[...], wo_ref[0:half, :], preferred_element_type=f32)
    mix = mix + jnp.dot(g_ref[...], wo_ref[half:2 * half, :], preferred_element_type=f32)
    x1 = x_ref[...] + mix
    x1_ref[...] = x1
    h2 = x1 * lax.rsqrt(jnp.mean(x1 * x1, axis=-1, keepdims=True) + RMS_EPS) * n2_ref[...]
    h2_ref[...] = h2
    qh = jnp.dot(h2.astype(bf16), wq_ref[...], preferred_element_type=f32).astype(bf16)
    for c in range(PEER_GROUPS):
        st_ref[c] = lax.dot_general(sk_ref[c], qh[:, c * PEER_HALF:(c + 1) * PEER_HALF],
                                    (((1,), (1,)), ((), ())), preferred_element_type=f32)


def _tail_call(x2d, a, g, w_out, norm2_g, w_query, subkeys, *, tm, row0, n):
    assert n % tm == 0 and row0 % tm == 0
    f32, bf16 = jnp.float32, jnp.bfloat16
    row = lambda i: (i, 0)
    off = lambda i: (row0 // tm + i, 0)
    const = lambda i: (0, 0)
    sk = subkeys.reshape(PEER_GROUPS, PEER_NKEYS, PEER_HALF).astype(bf16)
    return pl.pallas_call(
        _tail_kernel,
        grid=(n // tm,),
        in_specs=[
            pl.BlockSpec((tm, D_MODEL), off),
            pl.BlockSpec((tm, Q_W), row),
            pl.BlockSpec((tm, GV_W), off),
            pl.BlockSpec((Q_W + GV_W, D_MODEL), const),
            pl.BlockSpec((1, D_MODEL), const),
            pl.BlockSpec((D_MODEL, PEER_HEADS * PEER_KEY_DIM), const),
            pl.BlockSpec((PEER_GROUPS, PEER_NKEYS, PEER_HALF), lambda i: (0, 0, 0)),
        ],
        out_specs=[pl.BlockSpec((tm, D_MODEL), row), pl.BlockSpec((tm, D_MODEL), row),
                   pl.BlockSpec((PEER_GROUPS, PEER_NKEYS, tm), lambda i: (0, 0, i))],
        out_shape=[jax.ShapeDtypeStruct((n, D_MODEL), f32), jax.ShapeDtypeStruct((n, D_MODEL), f32),
                   jax.ShapeDtypeStruct((PEER_GROUPS, PEER_NKEYS, n), f32)],
        compiler_params=pltpu.CompilerParams(
            dimension_semantics=("arbitrary",), vmem_limit_bytes=V7X_VMEM_LIMIT),
        name="tail_proj",
    )(x2d, a, g, w_out.astype(bf16), norm2_g.reshape(1, -1), w_query.astype(bf16), sk)


def _extract_topk(x, ids, k):
    r = x.shape[0]
    rows = lax.broadcasted_iota(jnp.int32, x.shape, 0).astype(jnp.float32)
    vals, picked = [], []
    for _ in range(k):
        mx = jnp.max(x, axis=0, keepdims=True)
        first = jnp.min(jnp.where(x == mx, rows, float(r)), axis=0, keepdims=True)
        hit = rows == first
        vals.append(mx)
        picked.append(first if ids is None else jnp.sum(jnp.where(hit, ids, 0.0), axis=0, keepdims=True))
        x = jnp.where(hit, NEG_BIG, x)
    return vals, picked


def _grid_candidates(v1, i1, v2, i2):
    s2 = jnp.concatenate(v2, axis=0)
    j2 = jnp.concatenate(i2, axis=0)
    cand, cidx = [], []
    for a in range(PEER_TOPK // 2):
        nb = PEER_TOPK if a == 0 else PEER_TOPK // 2
        cand.append(v1[a] + s2[0:nb])
        cidx.append(i1[a] * float(PEER_NKEYS) + j2[0:nb])
    tail = range(PEER_TOPK // 2, PEER_TOPK)
    cand.append(jnp.concatenate([v1[a] for a in tail], axis=0) + v2[0])
    cidx.append(jnp.concatenate([i1[a] for a in tail], axis=0) * float(PEER_NKEYS) + i2[0])
    return jnp.concatenate(cand, axis=0), jnp.concatenate(cidx, axis=0)


def _peer_topk_kernel(st_ref, e_ref, g_ref, ids_ref):
    f32 = jnp.float32

    def head(h, carry):
        v1, i1 = _extract_topk(st_ref[2 * h], None, PEER_TOPK)
        v2, i2 = _extract_topk(st_ref[2 * h + 1], None, PEER_TOPK)
        cand, cidx = _grid_candidates(v1, i1, v2, i2)
        top, eid = _extract_topk(cand, cidx, PEER_TOPK)
        top = jnp.concatenate(top, axis=0)
        e = jnp.exp(top - top[0:1, :])
        rows = pl.ds(pl.multiple_of(h * PEER_TOPK, PEER_TOPK), PEER_TOPK)
        g_ref[rows, :] = e / jnp.sum(e, axis=0, keepdims=True)
        ids_ref[rows, :] = jnp.concatenate(eid, axis=0)
        return carry

    lax.fori_loop(0, PEER_HEADS, head, 0)
    e_ref[...] = ids_ref[...].T.astype(jnp.int32)


def _peer_topk_call(st, *, tt):
    n = st.shape[2]
    assert n % tt == 0
    return pl.pallas_call(
        _peer_topk_kernel,
        grid=(n // tt,),
        in_specs=[pl.BlockSpec((PEER_GROUPS, PEER_NKEYS, tt), lambda i: (0, 0, i))],
        out_specs=[pl.BlockSpec((tt, PEER_SLOTS), lambda i: (i, 0)), pl.BlockSpec((PEER_SLOTS, tt), lambda i: (0, i))],
        out_shape=[jax.ShapeDtypeStruct((n, PEER_SLOTS), jnp.int32),
                   jax.ShapeDtypeStruct((PEER_SLOTS, n), jnp.float32)],
        scratch_shapes=[pltpu.VMEM((PEER_SLOTS, tt), jnp.float32)],
        compiler_params=pltpu.CompilerParams(
            dimension_semantics=("arbitrary",), vmem_limit_bytes=V7X_VMEM_LIMIT),
        name="peer_topk",
    )(st)


SC_CORES = 2
SC_SUBCORES = 16
SC_WORKERS = SC_CORES * SC_SUBCORES
SC_CHUNK = 64
SC_IDX_BLOCK = 2048
SC_CHUNKS_PER_BLOCK = SC_IDX_BLOCK // SC_CHUNK


def _sc_gather_rows(table, idx):
    m = idx.shape[0]
    width = table.shape[1]
    assert m % (SC_WORKERS * SC_IDX_BLOCK) == 0 and SC_CHUNKS_PER_BLOCK % 2 == 0
    chunks_per_worker = m // SC_WORKERS // SC_CHUNK
    cpb = SC_CHUNKS_PER_BLOCK
    mesh = plsc.VectorSubcoreMesh(core_axis_name="c", subcore_axis_name="s",
                                  num_cores=SC_CORES, num_subcores=SC_SUBCORES)

    @functools.partial(
        pl.kernel, mesh=mesh,
        out_type=jax.ShapeDtypeStruct((m, width), table.dtype),
        scratch_types=[pltpu.VMEM((cpb, SC_CHUNK), jnp.int32),
                       pltpu.VMEM((SC_CHUNK, width), table.dtype),
                       pltpu.VMEM((SC_CHUNK, width), table.dtype),
                       pltpu.SemaphoreType.DMA, pltpu.SemaphoreType.DMA,
                       pltpu.SemaphoreType.DMA, pltpu.SemaphoreType.DMA],
        name="peer_gather",
    )
    def gather_kernel(table_hbm, idx_hbm, out_hbm, idx_v, buf0, buf1, gsem0, gsem1, wsem0, wsem1):
        wid = lax.axis_index("s") * SC_CORES + lax.axis_index("c")
        base_chunk = wid * chunks_per_worker

        def gather(j, buf, sem):
            return pltpu.make_async_copy(table_hbm.at[idx_v.at[j]], buf, sem)

        def write(chunk, buf, sem):
            rows = pl.ds(pl.multiple_of(chunk * SC_CHUNK, SC_CHUNK), SC_CHUNK)
            return pltpu.make_async_copy(buf, out_hbm.at[rows], sem)

        @pl.loop(0, chunks_per_worker // cpb)
        def _(blk):
            c0 = base_chunk + blk * cpb
            pltpu.sync_copy(idx_hbm.at[pl.ds(pl.multiple_of(c0, cpb), cpb)], idx_v)
            gather(0, buf0, gsem0).start()

            @pl.loop(0, cpb // 2)
            def _(p):
                j = p * 2
                gather(j, buf0, gsem0).wait()
                write(c0 + j, buf0, wsem0).start()

                @pl.when(p > 0)
                def _():
                    write(c0 + j - 1, buf1, wsem1).wait()

                gather(j + 1, buf1, gsem1).start()
                gather(j + 1, buf1, gsem1).wait()
                write(c0 + j + 1, buf1, wsem1).start()
                write(c0 + j, buf0, wsem0).wait()

                @pl.when(p < cpb // 2 - 1)
                def _():
                    gather(j + 2, buf0, gsem0).start()

            write(c0 + cpb - 1, buf1, wsem1).wait()

    return gather_kernel(table, idx.reshape(m // SC_CHUNK, SC_CHUNK))


def _pack_rows(w):
    b = lax.bitcast_convert_type(w.astype(jnp.bfloat16), jnp.uint16).astype(jnp.uint32)
    words = (b[:, :PEER_WORDS] << 16) | b[:, PEER_WORDS:]
    return lax.bitcast_convert_type(words, jnp.int32)


def _unpack_rows(words):
    hi = pltpu.bitcast(words & jnp.int32(-65536), jnp.float32)
    lo = pltpu.bitcast(words << 16, jnp.float32)
    return hi, lo


PEER_TOKEN_UNROLL = 4


def _peer_combine_kernel(gu_ref, gv_ref, h2_ref, gt_ref, x1_ref, y_ref):
    f32 = jnp.float32
    tb = h2_ref.shape[0]
    lane = lax.broadcasted_iota(jnp.int32, (PEER_SLOTS, tb), 1)

    def dots(n, dmat):
        rows = pl.ds(pl.multiple_of(n * PEER_SLOTS, PEER_SLOTS), PEER_SLOTS)
        x = h2_ref[pl.ds(n, 1), :]
        uh, ul = _unpack_rows(gu_ref[rows, :])
        d = jnp.sum(uh * x[:, 0:PEER_WORDS] + ul * x[:, PEER_WORDS:], axis=1, keepdims=True)
        return jnp.where(lane == n, d, dmat)

    dmat = lax.fori_loop(0, tb, dots, jnp.zeros((PEER_SLOTS, tb), f32), unroll=PEER_TOKEN_UNROLL)
    wmat = gt_ref[0] * jax.nn.gelu(dmat)

    def combine(n, carry):
        rows = pl.ds(pl.multiple_of(n * PEER_SLOTS, PEER_SLOTS), PEER_SLOTS)
        w = jnp.sum(jnp.where(lane == n, wmat, 0.0), axis=1, keepdims=True)
        vh, vl = _unpack_rows(gv_ref[rows, :])
        x1 = x1_ref[pl.ds(n, 1), :]
        y_ref[pl.ds(n, 1), 0:PEER_WORDS] = x1[:, 0:PEER_WORDS] + jnp.sum(w * vh, axis=0, keepdims=True)
        y_ref[pl.ds(n, 1), PEER_WORDS:] = x1[:, PEER_WORDS:] + jnp.sum(w * vl, axis=0, keepdims=True)
        return carry

    lax.fori_loop(0, tb, combine, 0, unroll=PEER_TOKEN_UNROLL)


def _peer_combine_call(gu, gv, h2, gates_t, x1, *, tb):
    n = h2.shape[0]
    assert n % tb == 0
    row = lambda i: (i, 0)
    gt = gates_t.reshape(PEER_SLOTS, n // tb, tb).transpose(1, 0, 2)
    return pl.pallas_call(
        _peer_combine_kernel,
        grid=(n // tb,),
        in_specs=[
            pl.BlockSpec((tb * PEER_SLOTS, PEER_WORDS), row),
            pl.BlockSpec((tb * PEER_SLOTS, PEER_WORDS), row),
            pl.BlockSpec((tb, D_MODEL), row),
            pl.BlockSpec((1, PEER_SLOTS, tb), lambda i: (i, 0, 0)),
            pl.BlockSpec((tb, D_MODEL), row),
        ],
        out_specs=pl.BlockSpec((tb, D_MODEL), row),
        out_shape=jax.ShapeDtypeStruct((n, D_MODEL), jnp.float32),
        compiler_params=pltpu.CompilerParams(
            dimension_semantics=("arbitrary",), vmem_limit_bytes=V7X_VMEM_LIMIT),
        name="peer_combine",
    )(gu, gv, h2, gt, x1)


def _tail_pallas(x2d, a, g, w_out, norm2_g, w_query, subkeys, u_words, v_words, row0, n):
    tm = _token_tile(n, (256, 128))
    x1, h2, st = _tail_call(x2d, a, g, w_out, norm2_g, w_query, subkeys, tm=tm, row0=row0, n=n)
    eidx, gates_t = _peer_topk_call(st, tt=tm)
    flat = eidx.reshape(n * PEER_SLOTS)
    gu = _sc_gather_rows(u_words, flat)
    gv = _sc_gather_rows(v_words, flat)
    return _peer_combine_call(gu, gv, h2, gates_t, x1, tb=16), eidx[0:8]


def _compress(rows, pos, w):
    b, l = rows.shape[:2]
    nc = (l - CMP_BLOCK) // CMP_STRIDE + 1
    chunks = rows[:, :(nc + 1) * CMP_STRIDE].reshape(b, nc + 1, CMP_STRIDE, NSA_KV_HEADS, HEAD_DIM)
    first = jnp.einsum('bnphd,pde->bnhe', chunks, w[:CMP_STRIDE])
    second = jnp.einsum('bnphd,pde->bnhe', chunks, w[CMP_STRIDE:])
    bias = jnp.einsum('pd,pde->e', pos, w)
    return first[:, :-1] + second[:, 1:] + bias


def _nsa_keys(kv_all, cmp_pos, w_cmp, k_norm_g):
    kc = _rmsnorm(_compress(kv_all[:, :, 0], cmp_pos[0], w_cmp[0]), k_norm_g[0])
    vc = _compress(kv_all[:, :, 1], cmp_pos[1], w_cmp[1])
    return kc, vc, kv_all[:, :, 2], kv_all[:, :, 3]


def _nsa_block(q, t_pos, kc, vc, ks, vs, kw, vw, w_pos, gate):
    b, nq = q.shape[:2]
    nc = kc.shape[1]
    l = ks.shape[1]
    ns = -(-l // SEL_BLOCK)
    f32 = jnp.float32
    cmp_end = jnp.arange(nc) * CMP_STRIDE + (CMP_BLOCK - 1)
    m_c = cmp_end[None, :] <= t_pos[:, None]
    s_c = jnp.einsum('bqhgd,bnhd->bhgqn', q, kc).astype(f32)
    p_c = jax.nn.softmax(jnp.where(m_c, s_c, MASK_VALUE), axis=-1) * m_c
    o_c = jnp.einsum('bhgqn,bnhd->bqhgd', p_c.astype(vc.dtype), vc)
    ratio = SEL_BLOCK // CMP_STRIDE
    imp = jnp.pad(p_c.sum(axis=2), ((0, 0), (0, 0), (0, 0), (0, ns * ratio - nc)))
    imp = imp.reshape(b, NSA_KV_HEADS, nq, ns, ratio)
    imp = imp.sum(-1) + jnp.pad(imp[..., :-1, ratio - 1], ((0, 0), (0, 0), (0, 0), (1, 0)))
    blk = jnp.arange(ns)
    visible = blk[None, :] * SEL_BLOCK <= t_pos[:, None]
    forced = (blk[None, :] == 0) | (blk[None, :] == t_pos[:, None] // SEL_BLOCK)
    score = jnp.where(forced, FORCE_SCORE, jnp.where(visible, imp, -1.0))
    _, idx = lax.top_k(score, min(SEL_TOPK, ns))
    tok = idx[..., None] * SEL_BLOCK + jnp.arange(SEL_BLOCK)
    bi = jnp.arange(b)[:, None, None, None, None]
    hi = jnp.arange(NSA_KV_HEADS)[None, :, None, None, None]
    safe_tok = jnp.minimum(tok, l - 1)
    ks_g = ks[bi, safe_tok, hi]
    vs_g = vs[bi, safe_tok, hi]
    m_s = (tok <= t_pos[None, None, :, None, None])[:, :, None]
    s_s = jnp.einsum('bqhgd,bhqkpd->bhgqkp', q, ks_g).astype(f32)
    s_s = jnp.where(m_s, s_s, MASK_VALUE)
    p_s = jax.nn.softmax(s_s.reshape(s_s.shape[:4] + (-1,)), axis=-1).reshape(s_s.shape)
    o_s = jnp.einsum('bhgqkp,bhqkpd->bqhgd', p_s.astype(vs.dtype), vs_g)
    d = t_pos[:, None] - w_pos[None, :]
    m_w = (d >= 0) & (d <= WINDOW) & (w_pos[None, :] >= 0)
    s_w = jnp.einsum('bqhgd,bkhd->bhgqk', q, kw).astype(f32)
    p_w = jax.nn.softmax(jnp.where(m_w, s_w, MASK_VALUE), axis=-1)
    o_w = jnp.einsum('bhgqk,bkhd->bqhgd', p_w.astype(vw.dtype), vw)
    out = gate[..., 0:1] * o_c + gate[..., 1:2] * o_s + gate[..., 2:3] * o_w
    return out.reshape(b, nq, NSA_HEADS * HEAD_DIM).astype(q.dtype)


def _nsa_prompt(q, kv_rows, win_rows, gate, cmp_pos, w_cmp, k_norm_g):
    b, t = q.shape[:2]
    kc, vc, ks, vs = _nsa_keys(kv_rows, cmp_pos, w_cmp, k_norm_g)
    win = jnp.pad(win_rows, ((0, 0), (WINDOW, 0), (0, 0), (0, 0), (0, 0)))

    def one_block(blk):
        s = blk * Q_BLOCK
        qb = lax.dynamic_slice_in_dim(q, s, Q_BLOCK, axis=1)
        gb = lax.dynamic_slice_in_dim(gate, s, Q_BLOCK, axis=1)
        wb = lax.dynamic_slice_in_dim(win, s, WINDOW + Q_BLOCK, axis=1)
        t_pos = s + jnp.arange(Q_BLOCK)
        w_pos = s - WINDOW + jnp.arange(WINDOW + Q_BLOCK)
        return _nsa_block(qb, t_pos, kc, vc, ks, vs, wb[:, :, 0], wb[:, :, 1], w_pos, gb)

    out = lax.map(one_block, jnp.arange(t // Q_BLOCK))
    return out.transpose(1, 0, 2, 3).reshape(b, t, -1)


def _nsa_sample(q, kv_rows, win_rows, gate, cache_kv_l, cache_win, page_table, cmp_pos, w_cmp, k_norm_g):
    bd, t = q.shape[:2]
    past_len = page_table.shape[1] * PAGE_SIZE
    past = cache_kv_l[page_table].reshape(bd, past_len, KV_ROWS, NSA_KV_HEADS, HEAD_DIM)
    kc, vc, ks, vs = _nsa_keys(jnp.concatenate([past, kv_rows], axis=1), cmp_pos, w_cmp, k_norm_g)
    wbuf = cache_win.shape[1]
    win_all = jnp.concatenate([cache_win, win_rows], axis=1)
    t_pos = past_len + jnp.arange(t)
    w_pos = past_len - wbuf + jnp.arange(wbuf + t)
    out = _nsa_block(q, t_pos, kc, vc, ks, vs, win_all[:, :, 0], win_all[:, :, 1], w_pos, gate)
    return out, win_all[:, -min(WINDOW, past_len + t):]


def _gla_chunked(q, k, v, log_a, s0):
    b, t, h = q.shape[:3]
    c = math.gcd(t, GLA_CHUNK)
    n = t // c

    def to_chunks(a):
        return a.reshape(b, n, c, h, a.shape[-1]).transpose(1, 0, 3, 2, 4)

    causal = jnp.tril(jnp.ones((c, c), dtype=bool))

    def step(S, inp):
        qc, kc, vc, ac = inp
        cum = jnp.cumsum(ac, axis=2)
        diff = jnp.minimum(cum[:, :, :, None] - cum[:, :, None, :], 0.0)
        decay = jnp.where(causal[..., None], jnp.exp(diff), 0.0)
        att = jnp.einsum('bhid,bhjd,bhijd->bhij', qc, kc, decay)
        o = jnp.einsum('bhij,bhjv->bhiv', att, vc) + jnp.einsum('bhid,bhdv->bhiv', qc * jnp.exp(cum), S)
        last = cum[:, :, -1:]
        S = jnp.exp(last)[:, :, 0, :, None] * S + jnp.einsum('bhjd,bhjv->bhdv', kc * jnp.exp(last - cum), vc)
        return S, o

    S, o = lax.scan(step, s0, (to_chunks(q), to_chunks(k), to_chunks(v), to_chunks(log_a)))
    return o.transpose(1, 0, 3, 2, 4).reshape(b, t, h, -1), S


def _gla_mixer(gq, gk, gv, glr, gog, s0, w_gate, b_gate, norm_g):
    b, t = gq.shape[:2]
    f32 = jnp.float32
    q = gq.reshape(b, t, GLA_HEADS, GLA_DK).astype(f32) * (GLA_DK ** -0.5)
    k = gk.reshape(b, t, GLA_HEADS, GLA_DK).astype(f32)
    v = gv.reshape(b, t, GLA_HEADS, GLA_DV).astype(f32)
    log_a = jax.nn.log_sigmoid((glr @ w_gate + b_gate).astype(f32)).reshape(b, t, GLA_HEADS, GLA_DK) / GLA_GATE_TEMP
    o, S = _gla_chunked(q, k, v, log_a, s0.astype(f32))
    o = _rmsnorm(o, norm_g) * jax.nn.silu(gog.astype(f32)).reshape(b, t, GLA_HEADS, GLA_DV)
    return o.reshape(b, t, -1).astype(gq.dtype), S


def _peer_ffn(h, w_query, subkeys, u, v):
    b, t, d = h.shape
    n = b * t
    nb = -(-n // PEER_TOKEN_BLOCK)
    flat = jnp.pad(h.reshape(n, d), ((0, nb * PEER_TOKEN_BLOCK - n), (0, 0))).reshape(nb, PEER_TOKEN_BLOCK, d)

    def one_block(xb):
        qh = (xb @ w_query).reshape(-1, PEER_HEADS, 2, PEER_KEY_DIM // 2)
        s = jnp.einsum('nhcd,hckd->nhck', qh, subkeys).astype(jnp.float32)
        s1, i1 = lax.top_k(s[:, :, 0], PEER_TOPK)
        s2, i2 = lax.top_k(s[:, :, 1], PEER_TOPK)
        cand = (s1[..., :, None] + s2[..., None, :]).reshape(s1.shape[:-1] + (-1,))
        cidx = (i1[..., :, None] * PEER_NKEYS + i2[..., None, :]).reshape(i1.shape[:-1] + (-1,))
        top, pos = lax.top_k(cand, PEER_TOPK)
        eidx = jnp.take_along_axis(cidx, pos, axis=-1)
        g = jax.nn.softmax(top, axis=-1)
        act = jax.nn.gelu(jnp.einsum('nhkd,nd->nhk', jnp.take(u, eidx, axis=0), xb).astype(jnp.float32))
        return jnp.einsum('nhk,nhkd->nd', (g * act).astype(xb.dtype), jnp.take(v, eidx, axis=0))

    out = lax.map(one_block, flat)
    return out.reshape(-1, d)[:n].reshape(b, t, d)


def _residual_tail(x, mix, w_out, norm2_g, w_query, subkeys, u, v):
    x = x + (mix @ w_out).astype(x.dtype)
    return x + _peer_ffn(_rmsnorm(x, norm2_g), w_query, subkeys, u, v).astype(x.dtype)


def _mixer_inputs(x, norm1_g, w_in_r, q_norm_g, k_norm_g, *, tm):
    b, t = x.shape[:2]
    q2d, kvr, winr, misc, gq, gk, gv, gog, kvb = _inproj(
        x.reshape(b * t, D_MODEL), norm1_g, w_in_r, q_norm_g, k_norm_g, tm=tm)
    q = q2d.astype(jnp.float32).reshape(b, t, NSA_KV_HEADS, NSA_GROUP, HEAD_DIM)
    kv_rows = kvr.reshape(b, t, 4, NSA_KV_HEADS, HEAD_DIM)
    win_rows = winr.reshape(b, t, 2, NSA_KV_HEADS, HEAD_DIM)
    gate = jax.nn.sigmoid(misc[:, :GATE_W]).reshape(b, t, NSA_KV_HEADS, NSA_GROUP, 3)
    glr = misc[:, GATE_W:GATE_W + GLA_GATE_RANK].reshape(b, t, -1)
    rs = lambda a: a.reshape(b, t, -1)
    return q, kv_rows, win_rows, gate, (rs(gq), rs(gk), rs(gv), glr, rs(gog)), (q2d, misc, kvr, kvb, gq, gk, gv, gog)


def _token_tile(n, candidates=(512, 256, 128, 64, 32, 16, 8)):
    for tm in candidates:
        if n % tm == 0:
            return tm
    raise ValueError(n)


def kernel(x_prompt, x_sample, cache_kv, cache_win, state_gla, page_table, norm1_g, w_in, q_norm_g, k_norm_g, cmp_pos, w_cmp, gla_w_gate, gla_b_gate, gla_norm_g, w_out, norm2_g, peer_w_query, peer_subkeys, peer_u, peer_v):
    depth = w_in.shape[0]
    xp, xs = x_prompt, x_sample
    kv_p, win_p, gla_p, kv_s, win_s, gla_s = [], [], [], [], [], []
    for l in range(depth):
        w_in_r = _reorder_w_in(w_in[l])
        u_words, v_words = _pack_rows(peer_u[l]), _pack_rows(peer_v[l])
        kv_shape = (KV_ROWS, NSA_KV_HEADS, HEAD_DIM)
        win_shape = (2, NSA_KV_HEADS, HEAD_DIM)

        def sample_group(xs, after):
            bs, ts = xs.shape[0], xs.shape[1]
            ns = bs * ts
            xs2d = xs.reshape(ns, D_MODEL)
            q2d, kvr, winr, misc, gq, gk, gv, gog, kvb = _inproj(
                xs2d, norm1_g[l], w_in_r, q_norm_g[l], k_norm_g[l], tm=_token_tile(ns))
            a = _nsa_sample_call(q2d, misc, kvb, cache_kv[l], cache_win[l], page_table,
                                 cmp_pos[l], w_cmp[l], k_norm_g[l], ts, after)
            g, s_new = _gla_call(gq, gk, gv, gog, misc, state_gla[l].astype(jnp.float32), gla_w_gate[l],
                                 gla_b_gate[l], gla_norm_g[l], bs, ts)
            win_all = jnp.concatenate([cache_win[l], winr.reshape((bs, ts) + win_shape)], axis=1)
            kv_s.append(kvr.reshape((bs, ts) + kv_shape))
            win_s.append(win_all[:, -min(WINDOW, page_table.shape[1] * PAGE_SIZE + ts):])
            gla_s.append(s_new.astype(state_gla.dtype))
            y, token = _tail_pallas(xs2d, a, g, w_out[l], norm2_g[l], peer_w_query[l], peer_subkeys[l],
                                    u_words, v_words, 0, ns)
            return y.reshape(xs.shape), token

        bp, tp = xp.shape[0], xp.shape[1]
        x2d = xp.reshape(bp * tp, D_MODEL)
        q2d, kvr, winr, misc, gq, gk, gv, gog, kvb = _inproj(
            x2d, norm1_g[l], w_in_r, q_norm_g[l], k_norm_g[l], tm=_token_tile(bp * tp))
        kcp, vcp = _prompt_compressed_kv(kvr, cmp_pos[l], w_cmp[l], k_norm_g[l], bp, tp)
        s0 = jnp.zeros((bp, GLA_HEADS, GLA_DK, GLA_DV), jnp.float32)
        g, s_new = _gla_call(gq, gk, gv, gog, misc, s0, gla_w_gate[l], gla_b_gate[l], gla_norm_g[l], bp, tp)
        kv_p.append(kvr.reshape((bp, tp) + kv_shape))
        win_p.append(winr.reshape((bp, tp) + win_shape)[:, -min(WINDOW, tp):])
        gla_p.append(s_new.astype(state_gla.dtype))
        nb = tp // Q_BLOCK
        nblk = _token_tile(nb, (nb // PROMPT_SPLITS, nb))
        ys = {}
        token = jnp.zeros((8, PEER_SLOTS), jnp.int32)
        chunks = [(b, blk0) for blk0 in range(0, nb, nblk) for b in range(bp)]
        for ci, (b, blk0) in enumerate(chunks):
            a = _nsa_prompt_call(q2d, misc, kcp, vcp, kvb, tp, b, blk0, nblk, token)
            ys[(b, blk0)], token = _tail_pallas(x2d, a, g, w_out[l], norm2_g[l], peer_w_query[l], peer_subkeys[l],
                                                u_words, v_words, (b * nb + blk0) * Q_BLOCK, nblk * Q_BLOCK)
            if ci == 0:
                xs, token = sample_group(xs, token)
        xp = jnp.concatenate([ys[k] for k in sorted(ys)], axis=0).reshape(bp, tp, D_MODEL)
    return (xp, xs, jnp.stack(kv_p), jnp.stack(win_p), jnp.stack(gla_p),
            jnp.stack(kv_s), jnp.stack(win_s), jnp.stack(gla_s))
```

```python
import functools
import math

import jax
import jax.numpy as jnp
import numpy as np
from jax import lax
from jax.experimental import pallas as pl
from jax.experimental.pallas import tpu as pltpu
from jax.experimental.pallas import tpu_sc as plsc

D_MODEL = 1024
NSA_HEADS = 8
NSA_KV_HEADS = 2
NSA_GROUP = NSA_HEADS // NSA_KV_HEADS
HEAD_DIM = 64
CMP_STRIDE = 16
CMP_BLOCK = 32
SEL_BLOCK = 64
SEL_TOPK = 16
WINDOW = 512
Q_BLOCK = 128
PAGE_SIZE = 128
GLA_HEADS = 4
GLA_DV = 128
GLA_DK = 64
GLA_GATE_RANK = 16
GLA_GATE_TEMP = 16.0
GLA_CHUNK = 64
PEER_HEADS = 8
PEER_NKEYS = 128
PEER_KEY_DIM = 256
PEER_TOPK = 16
PEER_TOKEN_BLOCK = 128
KV_ROWS = 4
RMS_EPS = 1e-6
MASK_VALUE = -1e30
FORCE_SCORE = 1e4

Q_W = NSA_HEADS * HEAD_DIM
KV_W = 6 * NSA_KV_HEADS * HEAD_DIM
GATE_W = 3 * NSA_HEADS
GQ_W = GLA_HEADS * GLA_DK
GV_W = GLA_HEADS * GLA_DV
MISC_W = 128
IN_SIZES = (Q_W, KV_W, GATE_W, GQ_W, GQ_W, GV_W, GLA_GATE_RANK, GV_W)
P_W = Q_W + KV_W + GQ_W + GQ_W + GV_W + GV_W + MISC_W

V7X_VMEM_LIMIT = 56 * 1024 * 1024


def _rmsnorm(x, g):
    xf = x.astype(jnp.float32)
    y = xf * lax.rsqrt(jnp.mean(xf * xf, axis=-1, keepdims=True) + RMS_EPS)
    return (y * g.astype(jnp.float32)).astype(x.dtype)


def _head_group_ones(width, group, dtype):
    r = lax.broadcasted_iota(jnp.int32, (width, width), 0) // group
    c = lax.broadcasted_iota(jnp.int32, (width, width), 1) // group
    return jnp.where(r == c, 1.0, 0.0).astype(dtype)


def _group_mean_sq(x, group):
    sq = x * x
    hi = sq.astype(jnp.bfloat16)
    lo = (sq - hi.astype(jnp.float32)).astype(jnp.bfloat16)
    ones = _head_group_ones(x.shape[-1], group, jnp.bfloat16)
    s = jnp.dot(hi, ones, preferred_element_type=jnp.float32)
    s = s + jnp.dot(lo, ones, preferred_element_type=jnp.float32)
    return s * (1.0 / group)


def _inproj_kernel(x_ref, g1_ref, w_ref, qg_ref, ksg_ref, kwg_ref,
                   q_ref, kv_ref, win_ref, misc_ref, gq_ref, gk_ref, gv_ref, gog_ref, kvb_ref):
    x = x_ref[...]
    h = x * lax.rsqrt(jnp.mean(x * x, axis=-1, keepdims=True) + RMS_EPS) * g1_ref[...]
    p = jnp.dot(h.astype(jnp.bfloat16), w_ref[...], preferred_element_type=jnp.float32)
    o = 0
    q = p[:, o:o + Q_W]; o += Q_W
    kv = p[:, o:o + KV_W]; o += KV_W
    gq_ref[...] = p[:, o:o + GQ_W]; o += GQ_W
    gk_ref[...] = p[:, o:o + GQ_W]; o += GQ_W
    gv_ref[...] = p[:, o:o + GV_W]; o += GV_W
    gog_ref[...] = p[:, o:o + GV_W]; o += GV_W
    misc_ref[...] = p[:, o:o + MISC_W]
    qn = q * lax.rsqrt(_group_mean_sq(q, HEAD_DIM) + RMS_EPS) * qg_ref[...] * (HEAD_DIM ** -0.5)
    q_ref[...] = qn.astype(q_ref.dtype)
    hw = NSA_KV_HEADS * HEAD_DIM
    k_sel = kv[:, 2 * hw:3 * hw]
    k_sel = k_sel * lax.rsqrt(_group_mean_sq(k_sel, HEAD_DIM) + RMS_EPS) * ksg_ref[...]
    k_win = kv[:, 4 * hw:5 * hw]
    k_win = k_win * lax.rsqrt(_group_mean_sq(k_win, HEAD_DIM) + RMS_EPS) * kwg_ref[...]
    kv_ref[:, 0:2 * hw] = kv[:, 0:2 * hw]
    kv_ref[:, 2 * hw:3 * hw] = k_sel
    kv_ref[:, 3 * hw:4 * hw] = kv[:, 3 * hw:4 * hw]
    win_ref[:, 0:hw] = k_win
    win_ref[:, hw:2 * hw] = kv[:, 5 * hw:6 * hw]
    kvb_ref[:, 0:hw] = k_sel.astype(kvb_ref.dtype)
    kvb_ref[:, hw:2 * hw] = kv[:, 3 * hw:4 * hw].astype(kvb_ref.dtype)
    kvb_ref[:, 2 * hw:3 * hw] = k_win.astype(kvb_ref.dtype)
    kvb_ref[:, 3 * hw:4 * hw] = kv[:, 5 * hw:6 * hw].astype(kvb_ref.dtype)


def _reorder_w_in(w_in):
    offs = np.cumsum((0,) + IN_SIZES)
    q, kv, gate, gq, gk, gv, glr, gog = [w_in[:, offs[i]:offs[i + 1]] for i in range(8)]
    pad = jnp.zeros((w_in.shape[0], MISC_W - GATE_W - GLA_GATE_RANK), w_in.dtype)
    return jnp.concatenate([q, kv, gq, gk, gv, gog, gate, glr, pad], axis=1).astype(jnp.bfloat16)


def _inproj(x2d, norm1_g, w_in_r, q_norm_g, k_norm_g, *, tm):
    n = x2d.shape[0]
    assert n % tm == 0
    hw = NSA_KV_HEADS * HEAD_DIM
    f32 = jnp.float32
    row = lambda i: (i, 0)
    const = lambda i: (0, 0)
    widths = (Q_W, 4 * hw, 2 * hw, MISC_W, GQ_W, GQ_W, GV_W, GV_W, 4 * hw)
    bf16 = jnp.bfloat16
    dtypes = (bf16, f32, f32, f32, f32, f32, f32, f32, bf16)
    return pl.pallas_call(
        _inproj_kernel,
        grid=(n // tm,),
        in_specs=[
            pl.BlockSpec((tm, D_MODEL), row),
            pl.BlockSpec((1, D_MODEL), const),
            pl.BlockSpec((D_MODEL, P_W), const),
            pl.BlockSpec((1, Q_W), const),
            pl.BlockSpec((1, hw), const),
            pl.BlockSpec((1, hw), const),
        ],
        out_specs=[pl.BlockSpec((tm, w), row) for w in widths],
        out_shape=[jax.ShapeDtypeStruct((n, w), dt) for w, dt in zip(widths, dtypes)],
        compiler_params=pltpu.CompilerParams(
            dimension_semantics=("arbitrary",), vmem_limit_bytes=V7X_VMEM_LIMIT),
        name="inproj",
    )(x2d, norm1_g.reshape(1, -1), w_in_r,
      jnp.tile(q_norm_g, NSA_HEADS).reshape(1, -1),
      jnp.tile(k_norm_g[1], NSA_KV_HEADS).reshape(1, -1),
      jnp.tile(k_norm_g[2], NSA_KV_HEADS).reshape(1, -1))


CMP_LANES = 2 * NSA_KV_HEADS * HEAD_DIM
ROW_LANES = KV_ROWS * NSA_KV_HEADS * HEAD_DIM


def _chunk_map(x, w_ref):
    acc = None
    for p in range(CMP_STRIDE):
        xp = x[:, p * ROW_LANES:p * ROW_LANES + CMP_LANES].astype(jnp.bfloat16)
        d = jnp.dot(xp, w_ref[p], preferred_element_type=jnp.float32)
        acc = d if acc is None else acc + d
    return acc


def _compress_kernel(x_ref, xn_ref, pos_ref, wf_ref, ws_ref, kg_ref, kc_ref, vc_ref):
    tn = x_ref.shape[1]
    x = x_ref[0]
    first = _chunk_map(x, wf_ref)
    second = _chunk_map(x, ws_ref)
    second_next = _chunk_map(xn_ref[0], ws_ref)
    bias = _chunk_map(pos_ref[0], wf_ref) + _chunk_map(pos_ref[1], ws_ref)
    rows = lax.broadcasted_iota(jnp.int32, second.shape, 0)
    shifted = jnp.where(rows == tn - 1, second_next[0:1, :], pltpu.roll(second, tn - 1, axis=0))
    out = first + shifted + bias[0:1, :]
    hw = NSA_KV_HEADS * HEAD_DIM
    kc = out[:, 0:hw]
    kc_ref[0] = kc * lax.rsqrt(_group_mean_sq(kc, HEAD_DIM) + RMS_EPS) * kg_ref[...]
    vc_ref[0] = out[:, hw:2 * hw]


def _compress_weights(w_cmp, cmp_pos):
    eye = jnp.eye(NSA_KV_HEADS, dtype=w_cmp.dtype)

    def bd(p):
        blocks = [jnp.kron(eye, w_cmp[r, p]) for r in range(2)]
        z = jnp.zeros_like(blocks[0])
        return jnp.concatenate([jnp.concatenate([blocks[0], z], 1), jnp.concatenate([z, blocks[1]], 1)], 0)

    wf = jnp.stack([bd(p) for p in range(CMP_STRIDE)]).astype(jnp.bfloat16)
    ws = jnp.stack([bd(p + CMP_STRIDE) for p in range(CMP_STRIDE)]).astype(jnp.bfloat16)

    def pos_rows(lo):
        pk = jnp.tile(cmp_pos[0, lo:lo + CMP_STRIDE], (1, NSA_KV_HEADS))
        pv = jnp.tile(cmp_pos[1, lo:lo + CMP_STRIDE], (1, NSA_KV_HEADS))
        row = jnp.concatenate([pk, pv, jnp.zeros_like(pk), jnp.zeros_like(pv)], axis=1)
        flat = row.reshape(1, CMP_STRIDE * ROW_LANES)
        return jnp.concatenate([flat, jnp.zeros((7, flat.shape[1]), flat.dtype)], axis=0)

    pos = jnp.stack([pos_rows(0), pos_rows(CMP_STRIDE)])
    return wf, ws, pos


def _compress_call(kv_chunks, wf, ws, pos, k_norm0, *, tn):
    b, nch, width = kv_chunks.shape
    assert nch % tn == 0 and tn % 8 == 0
    hw = NSA_KV_HEADS * HEAD_DIM
    last8 = nch // 8 - 1
    return pl.pallas_call(
        _compress_kernel,
        grid=(b, nch // tn),
        in_specs=[
            pl.BlockSpec((1, tn, width), lambda i, j: (i, j, 0)),
            pl.BlockSpec((1, 8, width), lambda i, j: (i, jnp.minimum((j + 1) * (tn // 8), last8), 0)),
            pl.BlockSpec((2, 8, width), lambda i, j: (0, 0, 0)),
            pl.BlockSpec((CMP_STRIDE, CMP_LANES, CMP_LANES), lambda i, j: (0, 0, 0)),
            pl.BlockSpec((CMP_STRIDE, CMP_LANES, CMP_LANES), lambda i, j: (0, 0, 0)),
            pl.BlockSpec((1, hw), lambda i, j: (0, 0)),
        ],
        out_specs=[pl.BlockSpec((1, tn, hw), lambda i, j: (i, j, 0))] * 2,
        out_shape=[jax.ShapeDtypeStruct((b, nch, hw), jnp.float32)] * 2,
        compiler_params=pltpu.CompilerParams(
            dimension_semantics=("arbitrary", "arbitrary"), vmem_limit_bytes=V7X_VMEM_LIMIT),
        name="compress",
    )(kv_chunks, kv_chunks, pos, wf, ws, jnp.tile(k_norm0, NSA_KV_HEADS).reshape(1, -1))


PROMPT_SPLITS = 4
SEL_TILE = 1024
SEL_PER_TILE = SEL_TILE // SEL_BLOCK
WIN_KEYS = WINDOW + Q_BLOCK
WIN_BLOCKS = WIN_KEYS // Q_BLOCK
ROW_CHUNK = 64
NEG_BIG = -3.0e38


def _lane_tile(x, reps):
    return jnp.concatenate([x] * reps, axis=1)


def _nsa_prompt_kernel(q_ref, misc_ref, kc_ref, vc_ref, ksvs_ref, w0, w1, w2, w3, w4,
                       psel_ref, pselt_ref, ebig_ref, o_ref,
                       q4_ref, s_ref, p_ref, bias_ref, bq_all_ref, psum_ref, m_ref, l_ref, al_ref,
                       kw_ref, vw_ref, oc_ref, os_ref, ow_ref, sc_ref, *, n_sel_blocks, blk0):
    i = pl.program_id(0) + blk0
    s0 = i * Q_BLOCK
    f32, bf16 = jnp.float32, jnp.bfloat16
    hw = NSA_KV_HEADS * HEAD_DIM
    n_chunks = NSA_GROUP * Q_BLOCK // ROW_CHUNK
    halves = Q_BLOCK // ROW_CHUNK

    qb = q_ref[...]
    for h in range(NSA_KV_HEADS):
        for g in range(NSA_GROUP):
            piece = jnp.dot(qb, psel_ref[h * NSA_GROUP + g], preferred_element_type=f32)
            q4_ref[h, g * Q_BLOCK:(g + 1) * Q_BLOCK, :] = piece.astype(bf16)

    for j, w in enumerate((w0, w1, w2, w3, w4)):
        kw_ref[j * Q_BLOCK:(j + 1) * Q_BLOCK, :] = w[:, 0:hw]
        vw_ref[j * Q_BLOCK:(j + 1) * Q_BLOCK, :] = w[:, hw:2 * hw]

    def chunk_rows(c):
        return pl.ds(c * ROW_CHUNK, ROW_CHUNK)

    def chunk_t(c):
        r = lax.broadcasted_iota(jnp.int32, (ROW_CHUNK, 1), 0)
        return s0 + (c % halves) * ROW_CHUNK + r

    for h in range(NSA_KV_HEADS):
        q4 = q4_ref[h]

        ncp = kc_ref.shape[1]
        nsb = ncp // 4
        s_ref[:, 0:ncp] = lax.dot_general(q4, kc_ref[0], (((1,), (1,)), ((), ())), preferred_element_type=f32)
        psum_ref[...] = jnp.zeros_like(psum_ref)

        def cmp_chunk(c, carry):
            rows = chunk_rows(c)
            t = chunk_t(c)
            col = lax.broadcasted_iota(jnp.int32, (ROW_CHUNK, ncp), 1)
            cidx = (col % nsb) * 4 + col // nsb
            valid = cidx * CMP_STRIDE + (CMP_BLOCK - 1) <= t
            s = jnp.where(valid, s_ref[rows, 0:ncp], MASK_VALUE)
            mx = jnp.max(s, axis=1, keepdims=True)
            e = jnp.exp(s - mx)
            p = jnp.where(valid, e / jnp.sum(e, axis=1, keepdims=True), 0.0)
            p_ref[rows, 0:ncp] = p.astype(bf16)
            hrows = pl.ds((c % halves) * ROW_CHUNK, ROW_CHUNK)
            psum_ref[hrows, :] += p
            return carry

        for c in range(n_chunks):
            cmp_chunk(c, 0)
        oc_ref[h] = jnp.dot(p_ref[:, 0:ncp], vc_ref[0], preferred_element_type=f32)

        ps = psum_ref[...]
        a3 = ps[:, 3 * nsb:4 * nsb]
        blk = lax.broadcasted_iota(jnp.int32, (Q_BLOCK, nsb), 1)
        tq = s0 + lax.broadcasted_iota(jnp.int32, (Q_BLOCK, nsb), 0)
        imp = ps[:, 0:nsb] + ps[:, nsb:2 * nsb] + ps[:, 2 * nsb:3 * nsb] + a3
        imp = imp + jnp.where(blk == 0, 0.0, pltpu.roll(a3, 1, axis=1))
        visible = blk * SEL_BLOCK <= tq
        forced = (blk == 0) | (blk == tq // SEL_BLOCK)
        sc_ref[h * Q_BLOCK:(h + 1) * Q_BLOCK, :] = jnp.where(forced, FORCE_SCORE, jnp.where(visible, imp, -1.0))

    blkf = lax.broadcasted_iota(jnp.int32, sc_ref.shape, 1).astype(f32)

    def pick(_, carry):
        sc, selm = carry
        mx = jnp.max(sc, axis=1, keepdims=True)
        first = jnp.min(jnp.where(sc == mx, blkf, float(nsb)), axis=1, keepdims=True)
        hit = blkf == first
        return jnp.where(hit, NEG_BIG, sc), jnp.where(hit, 1.0, selm)

    score = sc_ref[...]
    _, selm = lax.fori_loop(0, min(SEL_TOPK, n_sel_blocks), pick, (score, jnp.zeros_like(score)))
    bq_all_ref[...] = jnp.where(selm > 0.0, 0.0, MASK_VALUE).astype(bf16)

    for h in range(NSA_KV_HEADS):
        q4 = q4_ref[h]
        bq_ref = bq_all_ref.at[h * Q_BLOCK:(h + 1) * Q_BLOCK]

        m_ref[...] = jnp.full_like(m_ref, NEG_BIG)
        l_ref[...] = jnp.zeros_like(l_ref)
        os_ref[h] = jnp.zeros((NSA_GROUP * Q_BLOCK, hw), f32)

        def sel_tile(kt, carry):
            k0 = pl.multiple_of(kt * SEL_TILE, SEL_TILE)
            e_off = pl.multiple_of(nsb - kt * SEL_PER_TILE, SEL_PER_TILE)
            key = k0 + lax.broadcasted_iota(jnp.int32, (Q_BLOCK, SEL_TILE), 1)
            tq1 = s0 + lax.broadcasted_iota(jnp.int32, (Q_BLOCK, 1), 0)
            blockmask = jnp.dot(bq_ref[...], ebig_ref[pl.ds(e_off, nsb), :], preferred_element_type=f32)
            bias_ref[...] = jnp.where(key <= tq1, blockmask, MASK_VALUE)
            s_ref[...] = lax.dot_general(q4, ksvs_ref[pl.ds(k0, SEL_TILE), 0:hw],
                                         (((1,), (1,)), ((), ())), preferred_element_type=f32)

            for c in range(n_chunks):
                rows = slice(c * ROW_CHUNK, (c + 1) * ROW_CHUNK)
                hrows = slice((c % halves) * ROW_CHUNK, (c % halves + 1) * ROW_CHUNK)
                s = s_ref[rows, :] + bias_ref[hrows, :]
                m_old = m_ref[rows, :]
                m_new = jnp.maximum(m_old, jnp.max(s, axis=1, keepdims=True))
                p = jnp.exp(s - _lane_tile(m_new, SEL_TILE // 128))
                alpha = jnp.exp(m_old - m_new)
                l_ref[rows, :] = alpha * l_ref[rows, :] + jnp.sum(p, axis=1, keepdims=True)
                m_ref[rows, :] = m_new
                al_ref[rows, :] = alpha
                p_ref[rows, :] = p.astype(bf16)
            pv = jnp.dot(p_ref[...], ksvs_ref[pl.ds(k0, SEL_TILE), hw:2 * hw], preferred_element_type=f32)
            os_ref[h] = os_ref[h] * al_ref[...] + pv
            return carry

        lax.fori_loop(0, (s0 + Q_BLOCK - 1) // SEL_TILE + 1, sel_tile, 0)
        os_ref[h] = os_ref[h] / l_ref[...]

        s_ref[:, 0:WIN_KEYS] = lax.dot_general(q4, kw_ref[...], (((1,), (1,)), ((), ())),
                                               preferred_element_type=f32)

        def win_chunk(c, carry):
            rows = chunk_rows(c)
            t = chunk_t(c)
            pos = s0 - WINDOW + lax.broadcasted_iota(jnp.int32, (ROW_CHUNK, WIN_KEYS), 1)
            d = t - pos
            valid = (d >= 0) & (d <= WINDOW) & (pos >= 0)
            s = jnp.where(valid, s_ref[rows, 0:WIN_KEYS], MASK_VALUE)
            mx = jnp.max(s, axis=1, keepdims=True)
            e = jnp.exp(s - mx)
            p_ref[rows, 0:WIN_KEYS] = (e / jnp.sum(e, axis=1, keepdims=True)).astype(bf16)
            return carry

        for c in range(n_chunks):
            win_chunk(c, 0)
        ow_ref[h] = jnp.dot(p_ref[:, 0:WIN_KEYS], vw_ref[...], preferred_element_type=f32)

    gsig = jax.nn.sigmoid(misc_ref[...])
    out = jnp.zeros((Q_BLOCK, Q_W), f32)
    for h in range(NSA_KV_HEADS):
        for g in range(NSA_GROUP):
            hg = h * NSA_GROUP + g
            r = slice(g * Q_BLOCK, (g + 1) * Q_BLOCK)
            mix = (gsig[:, 3 * hg:3 * hg + 1] * oc_ref[h, r, :]
                   + gsig[:, 3 * hg + 1:3 * hg + 2] * os_ref[h, r, :]
                   + gsig[:, 3 * hg + 2:3 * hg + 3] * ow_ref[h, r, :])
            out = out + jnp.dot(mix.astype(bf16), pselt_ref[hg], preferred_element_type=f32)
    o_ref[...] = out.astype(o_ref.dtype)


def _nsa_constants(n_sel_blocks):
    hw = NSA_KV_HEADS * HEAD_DIM
    psel = np.zeros((NSA_HEADS, Q_W, hw), np.float32)
    for h in range(NSA_KV_HEADS):
        for g in range(NSA_GROUP):
            hg = h * NSA_GROUP + g
            for d in range(HEAD_DIM):
                psel[hg, hg * HEAD_DIM + d, h * HEAD_DIM + d] = 1.0
    pselt = np.transpose(psel, (0, 2, 1))
    r = np.arange(2 * n_sel_blocks)[:, None] - n_sel_blocks
    ebig = (r == (np.arange(SEL_TILE)[None, :] // SEL_BLOCK)).astype(np.float32)
    return (jnp.asarray(psel, jnp.bfloat16), jnp.asarray(pselt, jnp.bfloat16), jnp.asarray(ebig, jnp.bfloat16))


def _nsa_prompt_call(q, misc, kcp, vcp, kvb, seq, b, blk0, nblk):
    assert seq % SEL_TILE == 0 and seq % Q_BLOCK == 0
    nb = seq // Q_BLOCK
    nsb = seq // SEL_BLOCK
    hw = NSA_KV_HEADS * HEAD_DIM
    psel, pselt, ebig = _nsa_constants(nsb)
    rows4 = NSA_GROUP * Q_BLOCK
    f32, bf16 = jnp.float32, jnp.bfloat16
    r0 = b * nb + blk0

    def win_spec(j):
        return pl.BlockSpec((Q_BLOCK, 2 * hw),
                            lambda i: (b * nb + jnp.maximum(blk0 + i - (WIN_BLOCKS - 1) + j, 0), 1))

    return pl.pallas_call(
        functools.partial(_nsa_prompt_kernel, n_sel_blocks=nsb, blk0=blk0),
        grid=(nblk,),
        in_specs=[
            pl.BlockSpec((Q_BLOCK, Q_W), lambda i: (r0 + i, 0)),
            pl.BlockSpec((Q_BLOCK, MISC_W), lambda i: (r0 + i, 0)),
            pl.BlockSpec((1, seq // CMP_STRIDE, hw), lambda i: (b, 0, 0)),
            pl.BlockSpec((1, seq // CMP_STRIDE, hw), lambda i: (b, 0, 0)),
            pl.BlockSpec((seq, 2 * hw), lambda i: (b, 0)),
        ] + [win_spec(j) for j in range(WIN_BLOCKS)] + [
            pl.BlockSpec(psel.shape, lambda i: (0, 0, 0)),
            pl.BlockSpec(pselt.shape, lambda i: (0, 0, 0)),
            pl.BlockSpec(ebig.shape, lambda i: (0, 0)),
        ],
        out_specs=pl.BlockSpec((Q_BLOCK, Q_W), lambda i: (i, 0)),
        out_shape=jax.ShapeDtypeStruct((nblk * Q_BLOCK, Q_W), bf16),
        scratch_shapes=[
            pltpu.VMEM((NSA_KV_HEADS, rows4, hw), bf16),
            pltpu.VMEM((rows4, SEL_TILE), f32),
            pltpu.VMEM((rows4, SEL_TILE), bf16),
            pltpu.VMEM((Q_BLOCK, SEL_TILE), f32),
            pltpu.VMEM((NSA_KV_HEADS * Q_BLOCK, nsb), bf16),
            pltpu.VMEM((Q_BLOCK, seq // CMP_STRIDE), f32),
            pltpu.VMEM((rows4, hw), f32),
            pltpu.VMEM((rows4, hw), f32),
            pltpu.VMEM((rows4, hw), f32),
            pltpu.VMEM((WIN_KEYS, hw), bf16),
            pltpu.VMEM((WIN_KEYS, hw), bf16),
            pltpu.VMEM((NSA_KV_HEADS, rows4, hw), f32),
            pltpu.VMEM((NSA_KV_HEADS, rows4, hw), f32),
            pltpu.VMEM((NSA_KV_HEADS, rows4, hw), f32),
            pltpu.VMEM((NSA_KV_HEADS * Q_BLOCK, nsb), f32),
        ],
        compiler_params=pltpu.CompilerParams(
            dimension_semantics=("arbitrary",), vmem_limit_bytes=V7X_VMEM_LIMIT),
        name="nsa_prompt",
    )(q, misc, kcp, vcp, kvb, kvb, kvb, kvb, kvb, kvb, psel, pselt, ebig)


def _prompt_compressed_kv(kvr2d, cmp_pos, w_cmp, k_norm_g, batch, seq):
    nch = seq // CMP_STRIDE
    wf, ws, pos = _compress_weights(w_cmp, cmp_pos)
    kc, vc = _compress_call(kvr2d.reshape(batch, nch, CMP_STRIDE * ROW_LANES), wf, ws, pos, k_norm_g[0],
                            tn=min(256, nch))

    def perm(a):
        return a.reshape(batch, nch // 4, 4, a.shape[-1]).transpose(0, 2, 1, 3).reshape(batch, nch, -1).astype(jnp.bfloat16)

    return perm(kc), perm(vc)


PAD_KEYS = 128


def _softmax_piece_max(pieces):
    m = None
    for s in pieces:
        pm = jnp.max(s, axis=1, keepdims=True)
        m = pm if m is None else jnp.maximum(m, pm)
    return m


def _nsa_sample_kernel(pt_ref, q_ref, misc_ref, kvb_ref, win_ref, cache_ref, wf_ref, ws_ref, pos_ref, kg_ref,
                       psel_ref, pselt_ref, ebig_ref, o_ref, pages_ref, sem_ref, *, past_len, n_new):
    f32, bf16 = jnp.float32, jnp.bfloat16
    b = pl.program_id(0)
    nseq = pl.num_programs(0)
    n_pages = past_len // PAGE_SIZE
    nsb = past_len // SEL_BLOCK
    ncp = past_len // CMP_STRIDE
    hw = NSA_KV_HEADS * HEAD_DIM
    slot = b % 2
    rows_q = NSA_GROUP * n_new

    def page_copy(seq, j, kind, s):
        return pltpu.make_async_copy(cache_ref.at[pt_ref[seq, j], :, pl.ds(kind * hw, hw)],
                                     pages_ref.at[s, kind, pl.ds(pl.multiple_of(j * PAGE_SIZE, PAGE_SIZE), PAGE_SIZE)],
                                     sem_ref.at[s])

    def for_each_page_copy(seq, s, fn):
        def body(j, c):
            for kind in range(KV_ROWS):
                fn(page_copy(seq, j, kind, s))
            return c
        lax.fori_loop(0, n_pages, body, 0)

    @pl.when(b == 0)
    def _():
        pages_ref[:, :, past_len:past_len + SEL_BLOCK, :] = jnp.zeros((2, KV_ROWS, SEL_BLOCK, hw), f32)
        for_each_page_copy(0, 0, lambda cp: cp.start())

    @pl.when(b + 1 < nseq)
    def _():
        for_each_page_copy(b + 1, 1 - slot, lambda cp: cp.start())

    for_each_page_copy(b, slot, lambda cp: cp.wait())

    def strided(start):
        rows = pl.ds(start, nsb, stride=SEL_BLOCK)
        return jnp.concatenate([pages_ref[slot, 0, rows, :], pages_ref[slot, 1, rows, :]], axis=1).astype(bf16)

    first = None
    second = None
    for p in range(CMP_STRIDE):
        xf = jnp.concatenate([strided(CMP_STRIDE * r + p) for r in range(4)], axis=0)
        xs = jnp.concatenate([strided(CMP_STRIDE * (r + 1) + p) for r in range(4)], axis=0)
        df = jnp.dot(xf, wf_ref[p], preferred_element_type=f32)
        ds_ = jnp.dot(xs, ws_ref[p], preferred_element_type=f32)
        first = df if first is None else first + df
        second = ds_ if second is None else second + ds_
    bias = _chunk_map(pos_ref[0], wf_ref) + _chunk_map(pos_ref[1], ws_ref)
    cmp_out = first + second + bias[0:1, :]
    kc = cmp_out[:, 0:hw]
    kc = (kc * lax.rsqrt(_group_mean_sq(kc, HEAD_DIM) + RMS_EPS) * kg_ref[...]).astype(bf16)
    vc = cmp_out[:, hw:2 * hw].astype(bf16)

    qb = q_ref[...]
    newkv = kvb_ref[...]
    zpad = jnp.zeros((PAD_KEYS - n_new, hw), bf16)
    ks_new = jnp.concatenate([newkv[:, 0:hw], zpad], axis=0)
    vs_new = jnp.concatenate([newkv[:, hw:2 * hw], zpad], axis=0)
    kw_new = jnp.concatenate([newkv[:, 2 * hw:3 * hw], zpad], axis=0)
    vw_new = jnp.concatenate([newkv[:, 3 * hw:4 * hw], zpad], axis=0)
    wcache = win_ref[0]
    wbuf = wcache.shape[0]
    kw_old = wcache[:, 0:hw].astype(bf16)
    vw_old = wcache[:, hw:2 * hw].astype(bf16)

    tl = lax.broadcasted_iota(jnp.int32, (rows_q, 1), 0) % n_new
    t_abs = past_len + tl
    new_col = lax.broadcasted_iota(jnp.int32, (rows_q, PAD_KEYS), 1)
    new_ok = new_col <= tl
    nt_dims = (((1,), (1,)), ((), ()))
    gsig = jax.nn.sigmoid(misc_ref[...])
    out = jnp.zeros((n_new, Q_W), f32)

    for h in range(NSA_KV_HEADS):
        q4 = jnp.concatenate(
            [jnp.dot(qb, psel_ref[h * NSA_GROUP + g], preferred_element_type=f32).astype(bf16)
             for g in range(NSA_GROUP)], axis=0)

        s = lax.dot_general(q4, kc, nt_dims, preferred_element_type=f32)
        col = lax.broadcasted_iota(jnp.int32, (rows_q, ncp), 1)
        cidx = (col % nsb) * 4 + col // nsb
        valid = cidx * CMP_STRIDE + (CMP_BLOCK - 1) <= t_abs
        s = jnp.where(valid, s, MASK_VALUE)
        e = jnp.exp(s - jnp.max(s, axis=1, keepdims=True))
        pc = jnp.where(valid, e / jnp.sum(e, axis=1, keepdims=True), 0.0)
        o_c = jnp.dot(pc.astype(bf16), vc, preferred_element_type=f32)
        psum = pc[0:n_new]
        for g in range(1, NSA_GROUP):
            psum = psum + pc[g * n_new:(g + 1) * n_new]

        a3 = psum[:, 3 * nsb:4 * nsb]
        blk = lax.broadcasted_iota(jnp.int32, (n_new, nsb), 1)
        imp = psum[:, 0:nsb] + psum[:, nsb:2 * nsb] + psum[:, 2 * nsb:3 * nsb] + a3
        imp = imp + jnp.where(blk == 0, 0.0, pltpu.roll(a3, 1, axis=1))
        score = jnp.where(blk == 0, FORCE_SCORE, imp)
        blkf = blk.astype(f32)
        selm = jnp.zeros_like(score)
        for _ in range(SEL_TOPK - 1):
            mx = jnp.max(score, axis=1, keepdims=True)
            firstb = jnp.min(jnp.where(score == mx, blkf, float(nsb)), axis=1, keepdims=True)
            hit = blkf == firstb
            selm = jnp.where(hit, 1.0, selm)
            score = jnp.where(hit, NEG_BIG, score)
        bq = jnp.where(selm > 0.0, 0.0, MASK_VALUE).astype(bf16)

        s_new = jnp.where(new_ok, lax.dot_general(q4, ks_new, nt_dims, preferred_element_type=f32), MASK_VALUE)
        m_run = jnp.max(s_new, axis=1, keepdims=True)
        p_new = jnp.exp(s_new - m_run)
        l_run = jnp.sum(p_new, axis=1, keepdims=True)
        acc = jnp.dot(p_new.astype(bf16), vs_new, preferred_element_type=f32)

        def sel_tile(kt, carry):
            m_run, l_run, acc = carry
            k0 = pl.multiple_of(kt * SEL_TILE, SEL_TILE)
            e_off = pl.multiple_of(nsb - kt * SEL_PER_TILE, SEL_PER_TILE)
            bias = jnp.dot(bq, ebig_ref[pl.ds(e_off, nsb), :], preferred_element_type=f32)
            kt_rows = pages_ref[slot, 2, pl.ds(k0, SEL_TILE), :].astype(bf16)
            vt_rows = pages_ref[slot, 3, pl.ds(k0, SEL_TILE), :].astype(bf16)
            s = lax.dot_general(q4, kt_rows, nt_dims, preferred_element_type=f32)
            s = s + jnp.concatenate([bias] * NSA_GROUP, axis=0)
            m_new = jnp.maximum(m_run, jnp.max(s, axis=1, keepdims=True))
            p = jnp.exp(s - m_new)
            alpha = jnp.exp(m_run - m_new)
            l_new = alpha * l_run + jnp.sum(p, axis=1, keepdims=True)
            acc = acc * alpha + jnp.dot(p.astype(bf16), vt_rows, preferred_element_type=f32)
            return m_new, l_new, acc

        m_run, l_run, acc = lax.fori_loop(0, past_len // SEL_TILE, sel_tile, (m_run, l_run, acc))
        o_s = acc / l_run

        wpos = past_len - wbuf + lax.broadcasted_iota(jnp.int32, (rows_q, wbuf), 1)
        d = t_abs - wpos
        ok_old = (d >= 0) & (d <= WINDOW) & (wpos >= 0)
        s_old = jnp.where(ok_old, lax.dot_general(q4, kw_old, nt_dims, preferred_element_type=f32), MASK_VALUE)
        s_nw = jnp.where(new_ok, lax.dot_general(q4, kw_new, nt_dims, preferred_element_type=f32), MASK_VALUE)
        mw = _softmax_piece_max([s_old, s_nw])
        e_old = jnp.exp(s_old - mw)
        e_nw = jnp.exp(s_nw - mw)
        lw = jnp.sum(e_old, axis=1, keepdims=True) + jnp.sum(e_nw, axis=1, keepdims=True)
        o_w = (jnp.dot((e_old / lw).astype(bf16), vw_old, preferred_element_type=f32)
               + jnp.dot((e_nw / lw).astype(bf16), vw_new, preferred_element_type=f32))

        for g in range(NSA_GROUP):
            hg = h * NSA_GROUP + g
            r = slice(g * n_new, (g + 1) * n_new)
            mix = (gsig[:, 3 * hg:3 * hg + 1] * o_c[r] + gsig[:, 3 * hg + 1:3 * hg + 2] * o_s[r]
                   + gsig[:, 3 * hg + 2:3 * hg + 3] * o_w[r])
            out = out + jnp.dot(mix.astype(bf16), pselt_ref[hg], preferred_element_type=f32)
    o_ref[...] = out.astype(o_ref.dtype)


def _nsa_sample_call(q, misc, kvb, cache_kv_l, cache_win_l, page_table, cmp_pos, w_cmp, k_norm_g, n_new):
    bsz, n_pages = page_table.shape
    past_len = n_pages * PAGE_SIZE
    assert past_len % SEL_TILE == 0 and n_new % 8 == 0 and n_new <= PAD_KEYS
    assert (past_len + n_new - CMP_BLOCK) // CMP_STRIDE + 1 == past_len // CMP_STRIDE - 1
    nsb = past_len // SEL_BLOCK
    hw = NSA_KV_HEADS * HEAD_DIM
    wbuf = cache_win_l.shape[1]
    psel, pselt, ebig = _nsa_constants(nsb)
    wf, ws, pos = _compress_weights(w_cmp, cmp_pos)
    cache = cache_kv_l.reshape(cache_kv_l.shape[0], PAGE_SIZE, ROW_LANES)
    win = cache_win_l.reshape(bsz, wbuf, 2 * hw)
    row = lambda i, pt: (i, 0)
    c2 = lambda i, pt: (0, 0)
    c3 = lambda i, pt: (0, 0, 0)
    grid_spec = pltpu.PrefetchScalarGridSpec(
        num_scalar_prefetch=1,
        grid=(bsz,),
        in_specs=[
            pl.BlockSpec((n_new, Q_W), row),
            pl.BlockSpec((n_new, MISC_W), row),
            pl.BlockSpec((n_new, 4 * hw), row),
            pl.BlockSpec((1, wbuf, 2 * hw), lambda i, pt: (i, 0, 0)),
            pl.BlockSpec(memory_space=pl.ANY),
            pl.BlockSpec(wf.shape, c3), pl.BlockSpec(ws.shape, c3), pl.BlockSpec(pos.shape, c3),
            pl.BlockSpec((1, hw), c2),
            pl.BlockSpec(psel.shape, c3), pl.BlockSpec(pselt.shape, c3), pl.BlockSpec(ebig.shape, c2),
        ],
        out_specs=pl.BlockSpec((n_new, Q_W), row),
        scratch_shapes=[pltpu.VMEM((2, KV_ROWS, past_len + SEL_BLOCK, hw), jnp.float32),
                        pltpu.SemaphoreType.DMA((2,))],
    )
    return pl.pallas_call(
        functools.partial(_nsa_sample_kernel, past_len=past_len, n_new=n_new),
        grid_spec=grid_spec,
        out_shape=jax.ShapeDtypeStruct((bsz * n_new, Q_W), jnp.bfloat16),
        compiler_params=pltpu.CompilerParams(
            dimension_semantics=("arbitrary",), vmem_limit_bytes=V7X_VMEM_LIMIT),
        name="nsa_sample",
    )(page_table, q, misc, kvb, win, cache, wf, ws, pos,
      jnp.tile(k_norm_g[0], NSA_KV_HEADS).reshape(1, -1), psel, pselt, ebig)


GLA_J_GROUP = 8


def _split3(x):
    hi = x.astype(jnp.bfloat16)
    r = x - hi.astype(jnp.float32)
    mid = r.astype(jnp.bfloat16)
    lo = (r - mid.astype(jnp.float32)).astype(jnp.bfloat16)
    return hi, mid, lo


def _gla_kernel(gq_ref, gk_ref, gv_ref, gog_ref, misc_ref, s0_ref, wg_ref, bg_ref, ng_ref,
                o_ref, sout_ref, sbd_ref, la_ref, cum_ref, *, chunk):
    f32, bf16 = jnp.float32, jnp.bfloat16
    tstep = pl.program_id(1)
    n_tsteps = pl.num_programs(1)
    tb = gq_ref.shape[0]
    c = chunk
    mm = bf16 if c % 16 == 0 else f32
    hk, hv = GQ_W, GV_W

    @pl.when(tstep == 0)
    def _():
        sbd_ref[...] = jnp.zeros_like(sbd_ref)
        for h in range(GLA_HEADS):
            sbd_ref[h * GLA_DK:(h + 1) * GLA_DK, h * GLA_DV:(h + 1) * GLA_DV] = s0_ref[0, h]

    z = jnp.dot(misc_ref[...].astype(bf16), wg_ref[...], preferred_element_type=f32) + bg_ref[...]
    la_ref[...] = (jnp.minimum(z, 0.0) - jnp.log1p(jnp.exp(-jnp.abs(z)))) * (1.0 / GLA_GATE_TEMP)

    ri = lax.broadcasted_iota(jnp.int32, (c, c), 0)
    ci = lax.broadcasted_iota(jnp.int32, (c, c), 1)
    tril = jnp.where(ri >= ci, 1.0, 0.0).astype(bf16)
    kr = lax.broadcasted_iota(jnp.int32, (hk, hk), 0) // GLA_DK
    kc = lax.broadcasted_iota(jnp.int32, (hk, hk), 1) // GLA_DK
    head_rep = jnp.where(kr == kc, 1.0, 0.0).astype(bf16)
    eye_k = (lax.broadcasted_iota(jnp.int32, (hk, hk), 0) == lax.broadcasted_iota(jnp.int32, (hk, hk), 1))
    bd_mask = (lax.broadcasted_iota(jnp.int32, (hk, hv), 0) // GLA_DK
               == lax.broadcasted_iota(jnp.int32, (hk, hv), 1) // GLA_DV)
    lane_j = lax.broadcasted_iota(jnp.int32, (c, hk), 1) % GLA_DK
    row_i = lax.broadcasted_iota(jnp.int32, (c, hk), 0)

    def one_chunk(ch, carry):
        rows = pl.ds(pl.multiple_of(ch * c, c), c)
        q = gq_ref[rows, :] * (GLA_DK ** -0.5)
        k = gk_ref[rows, :]
        v = gv_ref[rows, :]
        la = la_ref[rows, :]
        hi, mid, lo = _split3(la)
        cum = (jnp.dot(tril, hi, preferred_element_type=f32) + jnp.dot(tril, mid, preferred_element_type=f32)
               + jnp.dot(tril, lo, preferred_element_type=f32))
        last = cum[c - 1:c, :]
        cum_ref[...] = cum

        def j_group(g, att):
            ws = []
            for jj in range(GLA_J_GROUP):
                j = g * GLA_J_GROUP + jj
                jrow = pl.ds(ch * c + j, 1)
                kj = gk_ref[jrow, :]
                cumj = cum_ref[pl.ds(j, 1), :]
                dec = jnp.where(row_i >= j, jnp.exp(jnp.minimum(cum - cumj, 0.0)), 0.0)
                ws.append((q * kj * dec).astype(bf16))
            r = jnp.dot(jnp.concatenate(ws, axis=0), head_rep, preferred_element_type=f32)
            for jj in range(GLA_J_GROUP):
                j = g * GLA_J_GROUP + jj
                att = att + jnp.where(lane_j == j, r[jj * c:(jj + 1) * c, :], 0.0)
            return att

        att = lax.fori_loop(0, c // GLA_J_GROUP, j_group, jnp.zeros((c, hk), f32))

        vt = jnp.concatenate([v] * (GLA_DK // c), axis=0) if c < GLA_DK else v
        vbd = jnp.where(bd_mask, jnp.concatenate([vt] * GLA_HEADS, axis=0), 0.0)
        sbd = sbd_ref[...]
        o = jnp.dot(att.astype(bf16), vbd.astype(bf16), preferred_element_type=f32)
        o = o + jnp.dot((q * jnp.exp(cum)).astype(bf16), sbd.astype(bf16), preferred_element_type=f32)

        ke = k * jnp.exp(last - cum)
        upd = lax.dot_general(ke.astype(mm), v.astype(mm), (((0,), (0,)), ((), ())), preferred_element_type=f32)
        dcol = jnp.sum(jnp.where(eye_k, jnp.exp(last), 0.0), axis=1, keepdims=True)
        sbd_ref[...] = sbd * dcol + jnp.where(bd_mask, upd, 0.0)

        gog = gog_ref[rows, :]
        for h in range(GLA_HEADS):
            sl = slice(h * GLA_DV, (h + 1) * GLA_DV)
            oh = o[:, sl]
            oh = oh * lax.rsqrt(jnp.mean(oh * oh, axis=1, keepdims=True) + RMS_EPS) * ng_ref[...]
            gh = gog[:, sl]
            o_ref[rows, sl] = (oh * gh * jax.nn.sigmoid(gh)).astype(o_ref.dtype)
        return carry

    lax.fori_loop(0, tb // c, one_chunk, 0)

    @pl.when(tstep == n_tsteps - 1)
    def _():
        for h in range(GLA_HEADS):
            sout_ref[0, h] = sbd_ref[h * GLA_DK:(h + 1) * GLA_DK, h * GLA_DV:(h + 1) * GLA_DV]


def _gla_call(gq, gk, gv, gog, misc, s0, w_gate, b_gate, norm_g, batch, seq):
    c = math.gcd(seq, GLA_CHUNK)
    tb = _token_tile(seq, (512, 256, 128, 64, 32, 16, 8))
    tb = max(tb, c)
    nt = seq // tb
    f32, bf16 = jnp.float32, jnp.bfloat16
    wg = jnp.zeros((MISC_W, GQ_W), f32).at[GATE_W:GATE_W + GLA_GATE_RANK].set(w_gate).astype(bf16)
    row = lambda b, t: (b * nt + t, 0)
    const = lambda b, t: (0, 0)
    state_spec = pl.BlockSpec((1, GLA_HEADS, GLA_DK, GLA_DV), lambda b, t: (b, 0, 0, 0))
    return pl.pallas_call(
        functools.partial(_gla_kernel, chunk=c),
        grid=(batch, nt),
        in_specs=[
            pl.BlockSpec((tb, GQ_W), row), pl.BlockSpec((tb, GQ_W), row),
            pl.BlockSpec((tb, GV_W), row), pl.BlockSpec((tb, GV_W), row),
            pl.BlockSpec((tb, MISC_W), row), state_spec,
            pl.BlockSpec((MISC_W, GQ_W), const), pl.BlockSpec((1, GQ_W), const), pl.BlockSpec((1, GLA_DV), const),
        ],
        out_specs=[pl.BlockSpec((tb, GV_W), row), state_spec],
        out_shape=[jax.ShapeDtypeStruct((batch * seq, GV_W), bf16),
                   jax.ShapeDtypeStruct((batch, GLA_HEADS, GLA_DK, GLA_DV), f32)],
        scratch_shapes=[pltpu.VMEM((GQ_W, GV_W), f32), pltpu.VMEM((tb, GQ_W), f32), pltpu.VMEM((c, GQ_W), f32)],
        compiler_params=pltpu.CompilerParams(
            dimension_semantics=("arbitrary", "arbitrary"), vmem_limit_bytes=V7X_VMEM_LIMIT),
        name="gla",
    )(gq, gk, gv, gog, misc, s0, wg, b_gate.reshape(1, -1), norm_g.reshape(1, -1))


PEER_GROUPS = 2 * PEER_HEADS
PEER_HALF = PEER_KEY_DIM // 2
PEER_SLOTS = PEER_HEADS * PEER_TOPK
PEER_WORDS = D_MODEL // 2


def _tail_kernel(x_ref, a_ref, g_ref, wo_ref, n2_ref, wq_ref, sk_ref, x1_ref, h2_ref, st_ref):
    f32, bf16 = jnp.float32, jnp.bfloat16
    half = wo_ref.shape[0] // 2
    mix = jnp.dot(a_ref[...], wo_ref[0:half, :], preferred_element_type=f32)
    mix = mix + jnp.dot(g_ref[...], wo_ref[half:2 * half, :], preferred_element_type=f32)
    x1 = x_ref[...] + mix
    x1_ref[...] = x1
    h2 = x1 * lax.rsqrt(jnp.mean(x1 * x1, axis=-1, keepdims=True) + RMS_EPS) * n2_ref[...]
    h2_ref[...] = h2
    qh = jnp.dot(h2.astype(bf16), wq_ref[...], preferred_element_type=f32).astype(bf16)
    for c in range(PEER_GROUPS):
        st_ref[c] = lax.dot_general(sk_ref[c], qh[:, c * PEER_HALF:(c + 1) * PEER_HALF],
                                    (((1,), (1,)), ((), ())), preferred_element_type=f32)


def _tail_call(x2d, a, g, w_out, norm2_g, w_query, subkeys, *, tm, row0, n):
    assert n % tm == 0 and row0 % tm == 0
    f32, bf16 = jnp.float32, jnp.bfloat16
    row = lambda i: (i, 0)
    off = lambda i: (row0 // tm + i, 0)
    const = lambda i: (0, 0)
    sk = subkeys.reshape(PEER_GROUPS, PEER_NKEYS, PEER_HALF).astype(bf16)
    return pl.pallas_call(
        _tail_kernel,
        grid=(n // tm,),
        in_specs=[
            pl.BlockSpec((tm, D_MODEL), off),
            pl.BlockSpec((tm, Q_W), row),
            pl.BlockSpec((tm, GV_W), off),
            pl.BlockSpec((Q_W + GV_W, D_MODEL), const),
            pl.BlockSpec((1, D_MODEL), const),
            pl.BlockSpec((D_MODEL, PEER_HEADS * PEER_KEY_DIM), const),
            pl.BlockSpec((PEER_GROUPS, PEER_NKEYS, PEER_HALF), lambda i: (0, 0, 0)),
        ],
        out_specs=[pl.BlockSpec((tm, D_MODEL), row), pl.BlockSpec((tm, D_MODEL), row),
                   pl.BlockSpec((PEER_GROUPS, PEER_NKEYS, tm), lambda i: (0, 0, i))],
        out_shape=[jax.ShapeDtypeStruct((n, D_MODEL), f32), jax.ShapeDtypeStruct((n, D_MODEL), f32),
                   jax.ShapeDtypeStruct((PEER_GROUPS, PEER_NKEYS, n), f32)],
        compiler_params=pltpu.CompilerParams(
            dimension_semantics=("arbitrary",), vmem_limit_bytes=V7X_VMEM_LIMIT),
        name="tail_proj",
    )(x2d, a, g, w_out.astype(bf16), norm2_g.reshape(1, -1), w_query.astype(bf16), sk)


def _extract_topk(x, ids, k):
    r = x.shape[0]
    rows = lax.broadcasted_iota(jnp.int32, x.shape, 0).astype(jnp.float32)
    vals, picked = [], []
    for _ in range(k):
        mx = jnp.max(x, axis=0, keepdims=True)
        first = jnp.min(jnp.where(x == mx, rows, float(r)), axis=0, keepdims=True)
        hit = rows == first
        vals.append(mx)
        picked.append(first if ids is None else jnp.sum(jnp.where(hit, ids, 0.0), axis=0, keepdims=True))
        x = jnp.where(hit, NEG_BIG, x)
    return vals, picked


def _grid_candidates(v1, i1, v2, i2):
    s2 = jnp.concatenate(v2, axis=0)
    j2 = jnp.concatenate(i2, axis=0)
    cand, cidx = [], []
    for a in range(PEER_TOPK // 2):
        nb = PEER_TOPK if a == 0 else PEER_TOPK // 2
        cand.append(v1[a] + s2[0:nb])
        cidx.append(i1[a] * float(PEER_NKEYS) + j2[0:nb])
    tail = range(PEER_TOPK // 2, PEER_TOPK)
    cand.append(jnp.concatenate([v1[a] for a in tail], axis=0) + v2[0])
    cidx.append(jnp.concatenate([i1[a] for a in tail], axis=0) * float(PEER_NKEYS) + i2[0])
    return jnp.concatenate(cand, axis=0), jnp.concatenate(cidx, axis=0)


def _peer_topk_kernel(st_ref, e_ref, g_ref, ids_ref):
    f32 = jnp.float32

    def head(h, carry):
        v1, i1 = _extract_topk(st_ref[2 * h], None, PEER_TOPK)
        v2, i2 = _extract_topk(st_ref[2 * h + 1], None, PEER_TOPK)
        cand, cidx = _grid_candidates(v1, i1, v2, i2)
        top, eid = _extract_topk(cand, cidx, PEER_TOPK)
        top = jnp.concatenate(top, axis=0)
        e = jnp.exp(top - top[0:1, :])
        rows = pl.ds(pl.multiple_of(h * PEER_TOPK, PEER_TOPK), PEER_TOPK)
        g_ref[rows, :] = e / jnp.sum(e, axis=0, keepdims=True)
        ids_ref[rows, :] = jnp.concatenate(eid, axis=0)
        return carry

    lax.fori_loop(0, PEER_HEADS, head, 0)
    e_ref[...] = ids_ref[...].T.astype(jnp.int32)


def _peer_topk_call(st, *, tt):
    n = st.shape[2]
    assert n % tt == 0
    return pl.pallas_call(
        _peer_topk_kernel,
        grid=(n // tt,),
        in_specs=[pl.BlockSpec((PEER_GROUPS, PEER_NKEYS, tt), lambda i: (0, 0, i))],
        out_specs=[pl.BlockSpec((tt, PEER_SLOTS), lambda i: (i, 0)), pl.BlockSpec((PEER_SLOTS, tt), lambda i: (0, i))],
        out_shape=[jax.ShapeDtypeStruct((n, PEER_SLOTS), jnp.int32),
                   jax.ShapeDtypeStruct((PEER_SLOTS, n), jnp.float32)],
        scratch_shapes=[pltpu.VMEM((PEER_SLOTS, tt), jnp.float32)],
        compiler_params=pltpu.CompilerParams(
            dimension_semantics=("arbitrary",), vmem_limit_bytes=V7X_VMEM_LIMIT),
        name="peer_topk",
    )(st)


SC_CORES = 2
SC_SUBCORES = 16
SC_WORKERS = SC_CORES * SC_SUBCORES
SC_CHUNK = 64
SC_IDX_BLOCK = 2048
SC_CHUNKS_PER_BLOCK = SC_IDX_BLOCK // SC_CHUNK


def _sc_gather_rows(table, idx):
    m = idx.shape[0]
    width = table.shape[1]
    assert m % (SC_WORKERS * SC_IDX_BLOCK) == 0 and SC_CHUNKS_PER_BLOCK % 2 == 0
    chunks_per_worker = m // SC_WORKERS // SC_CHUNK
    cpb = SC_CHUNKS_PER_BLOCK
    mesh = plsc.VectorSubcoreMesh(core_axis_name="c", subcore_axis_name="s",
                                  num_cores=SC_CORES, num_subcores=SC_SUBCORES)

    @functools.partial(
        pl.kernel, mesh=mesh,
        out_type=jax.ShapeDtypeStruct((m, width), table.dtype),
        scratch_types=[pltpu.VMEM((cpb, SC_CHUNK), jnp.int32),
                       pltpu.VMEM((SC_CHUNK, width), table.dtype),
                       pltpu.VMEM((SC_CHUNK, width), table.dtype),
                       pltpu.SemaphoreType.DMA, pltpu.SemaphoreType.DMA,
                       pltpu.SemaphoreType.DMA, pltpu.SemaphoreType.DMA],
        name="peer_gather",
    )
    def gather_kernel(table_hbm, idx_hbm, out_hbm, idx_v, buf0, buf1, gsem0, gsem1, wsem0, wsem1):
        wid = lax.axis_index("s") * SC_CORES + lax.axis_index("c")
        base_chunk = wid * chunks_per_worker

        def gather(j, buf, sem):
            return pltpu.make_async_copy(table_hbm.at[idx_v.at[j]], buf, sem)

        def write(chunk, buf, sem):
            rows = pl.ds(pl.multiple_of(chunk * SC_CHUNK, SC_CHUNK), SC_CHUNK)
            return pltpu.make_async_copy(buf, out_hbm.at[rows], sem)

        @pl.loop(0, chunks_per_worker // cpb)
        def _(blk):
            c0 = base_chunk + blk * cpb
            pltpu.sync_copy(idx_hbm.at[pl.ds(pl.multiple_of(c0, cpb), cpb)], idx_v)
            gather(0, buf0, gsem0).start()

            @pl.loop(0, cpb // 2)
            def _(p):
                j = p * 2
                gather(j, buf0, gsem0).wait()
                write(c0 + j, buf0, wsem0).start()

                @pl.when(p > 0)
                def _():
                    write(c0 + j - 1, buf1, wsem1).wait()

                gather(j + 1, buf1, gsem1).start()
                gather(j + 1, buf1, gsem1).wait()
                write(c0 + j + 1, buf1, wsem1).start()
                write(c0 + j, buf0, wsem0).wait()

                @pl.when(p < cpb // 2 - 1)
                def _():
                    gather(j + 2, buf0, gsem0).start()

            write(c0 + cpb - 1, buf1, wsem1).wait()

    return gather_kernel(table, idx.reshape(m // SC_CHUNK, SC_CHUNK))


def _pack_rows(w):
    b = lax.bitcast_convert_type(w.astype(jnp.bfloat16), jnp.uint16).astype(jnp.uint32)
    words = (b[:, :PEER_WORDS] << 16) | b[:, PEER_WORDS:]
    return lax.bitcast_convert_type(words, jnp.int32)


def _unpack_rows(words):
    hi = pltpu.bitcast(words & jnp.int32(-65536), jnp.float32)
    lo = pltpu.bitcast(words << 16, jnp.float32)
    return hi, lo


PEER_TOKEN_UNROLL = 4


def _peer_combine_kernel(gu_ref, gv_ref, h2_ref, gt_ref, x1_ref, y_ref):
    f32 = jnp.float32
    tb = h2_ref.shape[0]
    gate_tokens = gt_ref.shape[1]
    lane = lax.broadcasted_iota(jnp.int32, (PEER_SLOTS, gate_tokens), 1)
    lane0 = (pl.program_id(0) % (gate_tokens // tb)) * tb

    def dots(n, dmat):
        rows = pl.ds(pl.multiple_of(n * PEER_SLOTS, PEER_SLOTS), PEER_SLOTS)
        x = h2_ref[pl.ds(n, 1), :]
        uh, ul = _unpack_rows(gu_ref[rows, :])
        d = jnp.sum(uh * x[:, 0:PEER_WORDS] + ul * x[:, PEER_WORDS:], axis=1, keepdims=True)
        return jnp.where(lane == lane0 + n, d, dmat)

    dmat = lax.fori_loop(0, tb, dots, jnp.zeros((PEER_SLOTS, gate_tokens), f32), unroll=PEER_TOKEN_UNROLL)
    wmat = gt_ref[...] * jax.nn.gelu(dmat)

    def combine(n, carry):
        rows = pl.ds(pl.multiple_of(n * PEER_SLOTS, PEER_SLOTS), PEER_SLOTS)
        w = jnp.sum(jnp.where(lane == lane0 + n, wmat, 0.0), axis=1, keepdims=True)
        vh, vl = _unpack_rows(gv_ref[rows, :])
        x1 = x1_ref[pl.ds(n, 1), :]
        y_ref[pl.ds(n, 1), 0:PEER_WORDS] = x1[:, 0:PEER_WORDS] + jnp.sum(w * vh, axis=0, keepdims=True)
        y_ref[pl.ds(n, 1), PEER_WORDS:] = x1[:, PEER_WORDS:] + jnp.sum(w * vl, axis=0, keepdims=True)
        return carry

    lax.fori_loop(0, tb, combine, 0, unroll=PEER_TOKEN_UNROLL)


def _peer_combine_call(gu, gv, h2, gates_t, x1, *, tb):
    n = h2.shape[0]
    gate_tokens = 128
    assert n % gate_tokens == 0 and gate_tokens % tb == 0
    row = lambda i: (i, 0)
    return pl.pallas_call(
        _peer_combine_kernel,
        grid=(n // tb,),
        in_specs=[
            pl.BlockSpec((tb * PEER_SLOTS, PEER_WORDS), row),
            pl.BlockSpec((tb * PEER_SLOTS, PEER_WORDS), row),
            pl.BlockSpec((tb, D_MODEL), row),
            pl.BlockSpec((PEER_SLOTS, gate_tokens), lambda i: (0, i // (gate_tokens // tb))),
            pl.BlockSpec((tb, D_MODEL), row),
        ],
        out_specs=pl.BlockSpec((tb, D_MODEL), row),
        out_shape=jax.ShapeDtypeStruct((n, D_MODEL), jnp.float32),
        compiler_params=pltpu.CompilerParams(
            dimension_semantics=("arbitrary",), vmem_limit_bytes=V7X_VMEM_LIMIT),
        name="peer_combine",
    )(gu, gv, h2, gates_t, x1)


def _tail_pallas(x2d, a, g, w_out, norm2_g, w_query, subkeys, u_words, v_words, row0, n):
    tm = _token_tile(n, (256, 128))
    x1, h2, st = _tail_call(x2d, a, g, w_out, norm2_g, w_query, subkeys, tm=tm, row0=row0, n=n)
    eidx, gates_t = _peer_topk_call(st, tt=tm)
    flat = eidx.reshape(n * PEER_SLOTS)
    gu = _sc_gather_rows(u_words, flat)
    gv = _sc_gather_rows(v_words, flat)
    return _peer_combine_call(gu, gv, h2, gates_t, x1, tb=16)


def _compress(rows, pos, w):
    b, l = rows.shape[:2]
    nc = (l - CMP_BLOCK) // CMP_STRIDE + 1
    chunks = rows[:, :(nc + 1) * CMP_STRIDE].reshape(b, nc + 1, CMP_STRIDE, NSA_KV_HEADS, HEAD_DIM)
    first = jnp.einsum('bnphd,pde->bnhe', chunks, w[:CMP_STRIDE])
    second = jnp.einsum('bnphd,pde->bnhe', chunks, w[CMP_STRIDE:])
    bias = jnp.einsum('pd,pde->e', pos, w)
    return first[:, :-1] + second[:, 1:] + bias


def _nsa_keys(kv_all, cmp_pos, w_cmp, k_norm_g):
    kc = _rmsnorm(_compress(kv_all[:, :, 0], cmp_pos[0], w_cmp[0]), k_norm_g[0])
    vc = _compress(kv_all[:, :, 1], cmp_pos[1], w_cmp[1])
    return kc, vc, kv_all[:, :, 2], kv_all[:, :, 3]


def _nsa_block(q, t_pos, kc, vc, ks, vs, kw, vw, w_pos, gate):
    b, nq = q.shape[:2]
    nc = kc.shape[1]
    l = ks.shape[1]
    ns = -(-l // SEL_BLOCK)
    f32 = jnp.float32
    cmp_end = jnp.arange(nc) * CMP_STRIDE + (CMP_BLOCK - 1)
    m_c = cmp_end[None, :] <= t_pos[:, None]
    s_c = jnp.einsum('bqhgd,bnhd->bhgqn', q, kc).astype(f32)
    p_c = jax.nn.softmax(jnp.where(m_c, s_c, MASK_VALUE), axis=-1) * m_c
    o_c = jnp.einsum('bhgqn,bnhd->bqhgd', p_c.astype(vc.dtype), vc)
    ratio = SEL_BLOCK // CMP_STRIDE
    imp = jnp.pad(p_c.sum(axis=2), ((0, 0), (0, 0), (0, 0), (0, ns * ratio - nc)))
    imp = imp.reshape(b, NSA_KV_HEADS, nq, ns, ratio)
    imp = imp.sum(-1) + jnp.pad(imp[..., :-1, ratio - 1], ((0, 0), (0, 0), (0, 0), (1, 0)))
    blk = jnp.arange(ns)
    visible = blk[None, :] * SEL_BLOCK <= t_pos[:, None]
    forced = (blk[None, :] == 0) | (blk[None, :] == t_pos[:, None] // SEL_BLOCK)
    score = jnp.where(forced, FORCE_SCORE, jnp.where(visible, imp, -1.0))
    _, idx = lax.top_k(score, min(SEL_TOPK, ns))
    tok = idx[..., None] * SEL_BLOCK + jnp.arange(SEL_BLOCK)
    bi = jnp.arange(b)[:, None, None, None, None]
    hi = jnp.arange(NSA_KV_HEADS)[None, :, None, None, None]
    safe_tok = jnp.minimum(tok, l - 1)
    ks_g = ks[bi, safe_tok, hi]
    vs_g = vs[bi, safe_tok, hi]
    m_s = (tok <= t_pos[None, None, :, None, None])[:, :, None]
    s_s = jnp.einsum('bqhgd,bhqkpd->bhgqkp', q, ks_g).astype(f32)
    s_s = jnp.where(m_s, s_s, MASK_VALUE)
    p_s = jax.nn.softmax(s_s.reshape(s_s.shape[:4] + (-1,)), axis=-1).reshape(s_s.shape)
    o_s = jnp.einsum('bhgqkp,bhqkpd->bqhgd', p_s.astype(vs.dtype), vs_g)
    d = t_pos[:, None] - w_pos[None, :]
    m_w = (d >= 0) & (d <= WINDOW) & (w_pos[None, :] >= 0)
    s_w = jnp.einsum('bqhgd,bkhd->bhgqk', q, kw).astype(f32)
    p_w = jax.nn.softmax(jnp.where(m_w, s_w, MASK_VALUE), axis=-1)
    o_w = jnp.einsum('bhgqk,bkhd->bqhgd', p_w.astype(vw.dtype), vw)
    out = gate[..., 0:1] * o_c + gate[..., 1:2] * o_s + gate[..., 2:3] * o_w
    return out.reshape(b, nq, NSA_HEADS * HEAD_DIM).astype(q.dtype)


def _nsa_prompt(q, kv_rows, win_rows, gate, cmp_pos, w_cmp, k_norm_g):
    b, t = q.shape[:2]
    kc, vc, ks, vs = _nsa_keys(kv_rows, cmp_pos, w_cmp, k_norm_g)
    win = jnp.pad(win_rows, ((0, 0), (WINDOW, 0), (0, 0), (0, 0), (0, 0)))

    def one_block(blk):
        s = blk * Q_BLOCK
        qb = lax.dynamic_slice_in_dim(q, s, Q_BLOCK, axis=1)
        gb = lax.dynamic_slice_in_dim(gate, s, Q_BLOCK, axis=1)
        wb = lax.dynamic_slice_in_dim(win, s, WINDOW + Q_BLOCK, axis=1)
        t_pos = s + jnp.arange(Q_BLOCK)
        w_pos = s - WINDOW + jnp.arange(WINDOW + Q_BLOCK)
        return _nsa_block(qb, t_pos, kc, vc, ks, vs, wb[:, :, 0], wb[:, :, 1], w_pos, gb)

    out = lax.map(one_block, jnp.arange(t // Q_BLOCK))
    return out.transpose(1, 0, 2, 3).reshape(b, t, -1)


def _nsa_sample(q, kv_rows, win_rows, gate, cache_kv_l, cache_win, page_table, cmp_pos, w_cmp, k_norm_g):
    bd, t = q.shape[:2]
    past_len = page_table.shape[1] * PAGE_SIZE
    past = cache_kv_l[page_table].reshape(bd, past_len, KV_ROWS, NSA_KV_HEADS, HEAD_DIM)
    kc, vc, ks, vs = _nsa_keys(jnp.concatenate([past, kv_rows], axis=1), cmp_pos, w_cmp, k_norm_g)
    wbuf = cache_win.shape[1]
    win_all = jnp.concatenate([cache_win, win_rows], axis=1)
    t_pos = past_len + jnp.arange(t)
    w_pos = past_len - wbuf + jnp.arange(wbuf + t)
    out = _nsa_block(q, t_pos, kc, vc, ks, vs, win_all[:, :, 0], win_all[:, :, 1], w_pos, gate)
    return out, win_all[:, -min(WINDOW, past_len + t):]


def _gla_chunked(q, k, v, log_a, s0):
    b, t, h = q.shape[:3]
    c = math.gcd(t, GLA_CHUNK)
    n = t // c

    def to_chunks(a):
        return a.reshape(b, n, c, h, a.shape[-1]).transpose(1, 0, 3, 2, 4)

    causal = jnp.tril(jnp.ones((c, c), dtype=bool))

    def step(S, inp):
        qc, kc, vc, ac = inp
        cum = jnp.cumsum(ac, axis=2)
        diff = jnp.minimum(cum[:, :, :, None] - cum[:, :, None, :], 0.0)
        decay = jnp.where(causal[..., None], jnp.exp(diff), 0.0)
        att = jnp.einsum('bhid,bhjd,bhijd->bhij', qc, kc, decay)
        o = jnp.einsum('bhij,bhjv->bhiv', att, vc) + jnp.einsum('bhid,bhdv->bhiv', qc * jnp.exp(cum), S)
        last = cum[:, :, -1:]
        S = jnp.exp(last)[:, :, 0, :, None] * S + jnp.einsum('bhjd,bhjv->bhdv', kc * jnp.exp(last - cum), vc)
        return S, o

    S, o = lax.scan(step, s0, (to_chunks(q), to_chunks(k), to_chunks(v), to_chunks(log_a)))
    return o.transpose(1, 0, 3, 2, 4).reshape(b, t, h, -1), S


def _gla_mixer(gq, gk, gv, glr, gog, s0, w_gate, b_gate, norm_g):
    b, t = gq.shape[:2]
    f32 = jnp.float32
    q = gq.reshape(b, t, GLA_HEADS, GLA_DK).astype(f32) * (GLA_DK ** -0.5)
    k = gk.reshape(b, t, GLA_HEADS, GLA_DK).astype(f32)
    v = gv.reshape(b, t, GLA_HEADS, GLA_DV).astype(f32)
    log_a = jax.nn.log_sigmoid((glr @ w_gate + b_gate).astype(f32)).reshape(b, t, GLA_HEADS, GLA_DK) / GLA_GATE_TEMP
    o, S = _gla_chunked(q, k, v, log_a, s0.astype(f32))
    o = _rmsnorm(o, norm_g) * jax.nn.silu(gog.astype(f32)).reshape(b, t, GLA_HEADS, GLA_DV)
    return o.reshape(b, t, -1).astype(gq.dtype), S


def _peer_ffn(h, w_query, subkeys, u, v):
    b, t, d = h.shape
    n = b * t
    nb = -(-n // PEER_TOKEN_BLOCK)
    flat = jnp.pad(h.reshape(n, d), ((0, nb * PEER_TOKEN_BLOCK - n), (0, 0))).reshape(nb, PEER_TOKEN_BLOCK, d)

    def one_block(xb):
        qh = (xb @ w_query).reshape(-1, PEER_HEADS, 2, PEER_KEY_DIM // 2)
        s = jnp.einsum('nhcd,hckd->nhck', qh, subkeys).astype(jnp.float32)
        s1, i1 = lax.top_k(s[:, :, 0], PEER_TOPK)
        s2, i2 = lax.top_k(s[:, :, 1], PEER_TOPK)
        cand = (s1[..., :, None] + s2[..., None, :]).reshape(s1.shape[:-1] + (-1,))
        cidx = (i1[..., :, None] * PEER_NKEYS + i2[..., None, :]).reshape(i1.shape[:-1] + (-1,))
        top, pos = lax.top_k(cand, PEER_TOPK)
        eidx = jnp.take_along_axis(cidx, pos, axis=-1)
        g = jax.nn.softmax(top, axis=-1)
        act = jax.nn.gelu(jnp.einsum('nhkd,nd->nhk', jnp.take(u, eidx, axis=0), xb).astype(jnp.float32))
        return jnp.einsum('nhk,nhkd->nd', (g * act).astype(xb.dtype), jnp.take(v, eidx, axis=0))

    out = lax.map(one_block, flat)
    return out.reshape(-1, d)[:n].reshape(b, t, d)


def _residual_tail(x, mix, w_out, norm2_g, w_query, subkeys, u, v):
    x = x + (mix @ w_out).astype(x.dtype)
    return x + _peer_ffn(_rmsnorm(x, norm2_g), w_query, subkeys, u, v).astype(x.dtype)


def _mixer_inputs(x, norm1_g, w_in_r, q_norm_g, k_norm_g, *, tm):
    b, t = x.shape[:2]
    q2d, kvr, winr, misc, gq, gk, gv, gog, kvb = _inproj(
        x.reshape(b * t, D_MODEL), norm1_g, w_in_r, q_norm_g, k_norm_g, tm=tm)
    q = q2d.astype(jnp.float32).reshape(b, t, NSA_KV_HEADS, NSA_GROUP, HEAD_DIM)
    kv_rows = kvr.reshape(b, t, 4, NSA_KV_HEADS, HEAD_DIM)
    win_rows = winr.reshape(b, t, 2, NSA_KV_HEADS, HEAD_DIM)
    gate = jax.nn.sigmoid(misc[:, :GATE_W]).reshape(b, t, NSA_KV_HEADS, NSA_GROUP, 3)
    glr = misc[:, GATE_W:GATE_W + GLA_GATE_RANK].reshape(b, t, -1)
    rs = lambda a: a.reshape(b, t, -1)
    return q, kv_rows, win_rows, gate, (rs(gq), rs(gk), rs(gv), glr, rs(gog)), (q2d, misc, kvr, kvb, gq, gk, gv, gog)


def _token_tile(n, candidates=(512, 256, 128, 64, 32, 16, 8)):
    for tm in candidates:
        if n % tm == 0:
            return tm
    raise ValueError(n)


def kernel(x_prompt, x_sample, cache_kv, cache_win, state_gla, page_table, norm1_g, w_in, q_norm_g, k_norm_g, cmp_pos, w_cmp, gla_w_gate, gla_b_gate, gla_norm_g, w_out, norm2_g, peer_w_query, peer_subkeys, peer_u, peer_v):
    depth = w_in.shape[0]
    xp, xs = x_prompt, x_sample
    kv_p, win_p, gla_p, kv_s, win_s, gla_s = [], [], [], [], [], []
    for l in range(depth):
        w_in_r = _reorder_w_in(w_in[l])
        u_words, v_words = _pack_rows(peer_u[l]), _pack_rows(peer_v[l])
        kv_shape = (KV_ROWS, NSA_KV_HEADS, HEAD_DIM)
        win_shape = (2, NSA_KV_HEADS, HEAD_DIM)

        bp, tp = xp.shape[0], xp.shape[1]
        x2d = xp.reshape(bp * tp, D_MODEL)
        q2d, kvr, winr, misc, gq, gk, gv, gog, kvb = _inproj(
            x2d, norm1_g[l], w_in_r, q_norm_g[l], k_norm_g[l], tm=_token_tile(bp * tp))
        kcp, vcp = _prompt_compressed_kv(kvr, cmp_pos[l], w_cmp[l], k_norm_g[l], bp, tp)
        s0 = jnp.zeros((bp, GLA_HEADS, GLA_DK, GLA_DV), jnp.float32)
        g, s_new = _gla_call(gq, gk, gv, gog, misc, s0, gla_w_gate[l], gla_b_gate[l], gla_norm_g[l], bp, tp)
        kv_p.append(kvr.reshape((bp, tp) + kv_shape))
        win_p.append(winr.reshape((bp, tp) + win_shape)[:, -min(WINDOW, tp):])
        gla_p.append(s_new.astype(state_gla.dtype))
        nb = tp // Q_BLOCK
        nblk = _token_tile(nb, (nb // PROMPT_SPLITS, nb))
        ys = {}
        for blk0 in reversed(range(0, nb, nblk)):
            for b in reversed(range(bp)):
                a = _nsa_prompt_call(q2d, misc, kcp, vcp, kvb, tp, b, blk0, nblk)
                ys[(b, blk0)] = _tail_pallas(x2d, a, g, w_out[l], norm2_g[l], peer_w_query[l], peer_subkeys[l],
                                             u_words, v_words, (b * nb + blk0) * Q_BLOCK, nblk * Q_BLOCK)
        xp = jnp.concatenate([ys[k] for k in sorted(ys)], axis=0).reshape(bp, tp, D_MODEL)

        bs, ts = xs.shape[0], xs.shape[1]
        ns = bs * ts
        xs2d = xs.reshape(ns, D_MODEL)
        q2d, kvr, winr, misc, gq, gk, gv, gog, kvb = _inproj(
            xs2d, norm1_g[l], w_in_r, q_norm_g[l], k_norm_g[l], tm=_token_tile(ns))
        a = _nsa_sample_call(q2d, misc, kvb, cache_kv[l], cache_win[l], page_table,
                             cmp_pos[l], w_cmp[l], k_norm_g[l], ts)
        g, s_new = _gla_call(gq, gk, gv, gog, misc, state_gla[l].astype(jnp.float32), gla_w_gate[l],
                             gla_b_gate[l], gla_norm_g[l], bs, ts)
        win_all = jnp.concatenate([cache_win[l], winr.reshape((bs, ts) + win_shape)], axis=1)
        kv_s.append(kvr.reshape((bs, ts) + kv_shape))
        win_s.append(win_all[:, -min(WINDOW, page_table.shape[1] * PAGE_SIZE + ts):])
        gla_s.append(s_new.astype(state_gla.dtype))
        xs = _tail_pallas(xs2d, a, g, w_out[l], norm2_g[l], peer_w_query[l], peer_subkeys[l],
                          u_words, v_words, 0, ns).reshape(xs.shape)
    return (xp, xs, jnp.stack(kv_p), jnp.stack(win_p), jnp.stack(gla_p),
            jnp.stack(kv_s), jnp.stack(win_s), jnp.stack(gla_s))
```

```python
import functools
import math

import jax
import jax.numpy as jnp
import numpy as np
from jax import lax
from jax.experimental import pallas as pl
from jax.experimental.pallas import tpu as pltpu
from jax.experimental.pallas import tpu_sc as plsc

D_MODEL = 1024
NSA_HEADS = 8
NSA_KV_HEADS = 2
NSA_GROUP = NSA_HEADS // NSA_KV_HEADS
HEAD_DIM = 64
CMP_STRIDE = 16
CMP_BLOCK = 32
SEL_BLOCK = 64
SEL_TOPK = 16
WINDOW = 512
Q_BLOCK = 128
PAGE_SIZE = 128
GLA_HEADS = 4
GLA_DV = 128
GLA_DK = 64
GLA_GATE_RANK = 16
GLA_GATE_TEMP = 16.0
GLA_CHUNK = 64
PEER_HEADS = 8
PEER_NKEYS = 128
PEER_KEY_DIM = 256
PEER_TOPK = 16
KV_ROWS = 4
RMS_EPS = 1e-6
MASK_VALUE = -1e30
FORCE_SCORE = 1e4

Q_W = NSA_HEADS * HEAD_DIM
KV_W = 6 * NSA_KV_HEADS * HEAD_DIM
GATE_W = 3 * NSA_HEADS
GQ_W = GLA_HEADS * GLA_DK
GV_W = GLA_HEADS * GLA_DV
MISC_W = 128
IN_SIZES = (Q_W, KV_W, GATE_W, GQ_W, GQ_W, GV_W, GLA_GATE_RANK, GV_W)
P_W = Q_W + KV_W + GQ_W + GQ_W + GV_W + GV_W + MISC_W

V7X_VMEM_LIMIT = 56 * 1024 * 1024


def _head_group_ones(width, group, dtype):
    r = lax.broadcasted_iota(jnp.int32, (width, width), 0) // group
    c = lax.broadcasted_iota(jnp.int32, (width, width), 1) // group
    return jnp.where(r == c, 1.0, 0.0).astype(dtype)


def _group_mean_sq(x, group):
    sq = x * x
    hi = sq.astype(jnp.bfloat16)
    lo = (sq - hi.astype(jnp.float32)).astype(jnp.bfloat16)
    ones = _head_group_ones(x.shape[-1], group, jnp.bfloat16)
    s = jnp.dot(hi, ones, preferred_element_type=jnp.float32)
    s = s + jnp.dot(lo, ones, preferred_element_type=jnp.float32)
    return s * (1.0 / group)


def _inproj_kernel(x_ref, g1_ref, w_ref, qg_ref, ksg_ref, kwg_ref,
                   q_ref, kv_ref, win_ref, misc_ref, gq_ref, gk_ref, gv_ref, gog_ref, kvb_ref):
    x = x_ref[...]
    h = x * lax.rsqrt(jnp.mean(x * x, axis=-1, keepdims=True) + RMS_EPS) * g1_ref[...]
    p = jnp.dot(h.astype(jnp.bfloat16), w_ref[...], preferred_element_type=jnp.float32)
    o = 0
    q = p[:, o:o + Q_W]; o += Q_W
    kv = p[:, o:o + KV_W]; o += KV_W
    gq_ref[...] = p[:, o:o + GQ_W]; o += GQ_W
    gk_ref[...] = p[:, o:o + GQ_W]; o += GQ_W
    gv_ref[...] = p[:, o:o + GV_W]; o += GV_W
    gog_ref[...] = p[:, o:o + GV_W]; o += GV_W
    misc_ref[...] = p[:, o:o + MISC_W]
    qn = q * lax.rsqrt(_group_mean_sq(q, HEAD_DIM) + RMS_EPS) * qg_ref[...] * (HEAD_DIM ** -0.5)
    q_ref[...] = qn.astype(q_ref.dtype)
    hw = NSA_KV_HEADS * HEAD_DIM
    k_sel = kv[:, 2 * hw:3 * hw]
    k_sel = k_sel * lax.rsqrt(_group_mean_sq(k_sel, HEAD_DIM) + RMS_EPS) * ksg_ref[...]
    k_win = kv[:, 4 * hw:5 * hw]
    k_win = k_win * lax.rsqrt(_group_mean_sq(k_win, HEAD_DIM) + RMS_EPS) * kwg_ref[...]
    kv_ref[:, 0:2 * hw] = kv[:, 0:2 * hw]
    kv_ref[:, 2 * hw:3 * hw] = k_sel
    kv_ref[:, 3 * hw:4 * hw] = kv[:, 3 * hw:4 * hw]
    win_ref[:, 0:hw] = k_win
    win_ref[:, hw:2 * hw] = kv[:, 5 * hw:6 * hw]
    kvb_ref[:, 0:hw] = k_sel.astype(kvb_ref.dtype)
    kvb_ref[:, hw:2 * hw] = kv[:, 3 * hw:4 * hw].astype(kvb_ref.dtype)
    kvb_ref[:, 2 * hw:3 * hw] = k_win.astype(kvb_ref.dtype)
    kvb_ref[:, 3 * hw:4 * hw] = kv[:, 5 * hw:6 * hw].astype(kvb_ref.dtype)


def _reorder_w_in(w_in):
    offs = np.cumsum((0,) + IN_SIZES)
    q, kv, gate, gq, gk, gv, glr, gog = [w_in[:, offs[i]:offs[i + 1]] for i in range(8)]
    pad = jnp.zeros((w_in.shape[0], MISC_W - GATE_W - GLA_GATE_RANK), w_in.dtype)
    return jnp.concatenate([q, kv, gq, gk, gv, gog, gate, glr, pad], axis=1).astype(jnp.bfloat16)


def _inproj(x2d, norm1_g, w_in_r, q_norm_g, k_norm_g, *, tm):
    n = x2d.shape[0]
    assert n % tm == 0
    hw = NSA_KV_HEADS * HEAD_DIM
    f32 = jnp.float32
    row = lambda i: (i, 0)
    const = lambda i: (0, 0)
    widths = (Q_W, 4 * hw, 2 * hw, MISC_W, GQ_W, GQ_W, GV_W, GV_W, 4 * hw)
    bf16 = jnp.bfloat16
    dtypes = (bf16, f32, f32, f32, f32, f32, f32, f32, bf16)
    return pl.pallas_call(
        _inproj_kernel,
        grid=(n // tm,),
        in_specs=[
            pl.BlockSpec((tm, D_MODEL), row),
            pl.BlockSpec((1, D_MODEL), const),
            pl.BlockSpec((D_MODEL, P_W), const),
            pl.BlockSpec((1, Q_W), const),
            pl.BlockSpec((1, hw), const),
            pl.BlockSpec((1, hw), const),
        ],
        out_specs=[pl.BlockSpec((tm, w), row) for w in widths],
        out_shape=[jax.ShapeDtypeStruct((n, w), dt) for w, dt in zip(widths, dtypes)],
        compiler_params=pltpu.CompilerParams(
            dimension_semantics=("arbitrary",), vmem_limit_bytes=V7X_VMEM_LIMIT),
        name="inproj",
    )(x2d, norm1_g.reshape(1, -1), w_in_r,
      jnp.tile(q_norm_g, NSA_HEADS).reshape(1, -1),
      jnp.tile(k_norm_g[1], NSA_KV_HEADS).reshape(1, -1),
      jnp.tile(k_norm_g[2], NSA_KV_HEADS).reshape(1, -1))


CMP_LANES = 2 * NSA_KV_HEADS * HEAD_DIM
ROW_LANES = KV_ROWS * NSA_KV_HEADS * HEAD_DIM


def _chunk_map(x, w_ref):
    acc = None
    for p in range(CMP_STRIDE):
        xp = x[:, p * ROW_LANES:p * ROW_LANES + CMP_LANES].astype(jnp.bfloat16)
        d = jnp.dot(xp, w_ref[p], preferred_element_type=jnp.float32)
        acc = d if acc is None else acc + d
    return acc


def _compress_kernel(x_ref, xn_ref, pos_ref, wf_ref, ws_ref, kg_ref, kc_ref, vc_ref):
    tn = x_ref.shape[1]
    x = x_ref[0]
    first = _chunk_map(x, wf_ref)
    second = _chunk_map(x, ws_ref)
    second_next = _chunk_map(xn_ref[0], ws_ref)
    bias = _chunk_map(pos_ref[0], wf_ref) + _chunk_map(pos_ref[1], ws_ref)
    rows = lax.broadcasted_iota(jnp.int32, second.shape, 0)
    shifted = jnp.where(rows == tn - 1, second_next[0:1, :], pltpu.roll(second, tn - 1, axis=0))
    out = first + shifted + bias[0:1, :]
    hw = NSA_KV_HEADS * HEAD_DIM
    kc = out[:, 0:hw]
    kc_ref[0] = kc * lax.rsqrt(_group_mean_sq(kc, HEAD_DIM) + RMS_EPS) * kg_ref[...]
    vc_ref[0] = out[:, hw:2 * hw]


def _compress_weights(w_cmp, cmp_pos):
    eye = jnp.eye(NSA_KV_HEADS, dtype=w_cmp.dtype)

    def bd(p):
        blocks = [jnp.kron(eye, w_cmp[r, p]) for r in range(2)]
        z = jnp.zeros_like(blocks[0])
        return jnp.concatenate([jnp.concatenate([blocks[0], z], 1), jnp.concatenate([z, blocks[1]], 1)], 0)

    wf = jnp.stack([bd(p) for p in range(CMP_STRIDE)]).astype(jnp.bfloat16)
    ws = jnp.stack([bd(p + CMP_STRIDE) for p in range(CMP_STRIDE)]).astype(jnp.bfloat16)

    def pos_rows(lo):
        pk = jnp.tile(cmp_pos[0, lo:lo + CMP_STRIDE], (1, NSA_KV_HEADS))
        pv = jnp.tile(cmp_pos[1, lo:lo + CMP_STRIDE], (1, NSA_KV_HEADS))
        row = jnp.concatenate([pk, pv, jnp.zeros_like(pk), jnp.zeros_like(pv)], axis=1)
        flat = row.reshape(1, CMP_STRIDE * ROW_LANES)
        return jnp.concatenate([flat, jnp.zeros((7, flat.shape[1]), flat.dtype)], axis=0)

    pos = jnp.stack([pos_rows(0), pos_rows(CMP_STRIDE)])
    return wf, ws, pos


def _compress_call(kv_chunks, wf, ws, pos, k_norm0, *, tn):
    b, nch, width = kv_chunks.shape
    assert nch % tn == 0 and tn % 8 == 0
    hw = NSA_KV_HEADS * HEAD_DIM
    last8 = nch // 8 - 1
    return pl.pallas_call(
        _compress_kernel,
        grid=(b, nch // tn),
        in_specs=[
            pl.BlockSpec((1, tn, width), lambda i, j: (i, j, 0)),
            pl.BlockSpec((1, 8, width), lambda i, j: (i, jnp.minimum((j + 1) * (tn // 8), last8), 0)),
            pl.BlockSpec((2, 8, width), lambda i, j: (0, 0, 0)),
            pl.BlockSpec((CMP_STRIDE, CMP_LANES, CMP_LANES), lambda i, j: (0, 0, 0)),
            pl.BlockSpec((CMP_STRIDE, CMP_LANES, CMP_LANES), lambda i, j: (0, 0, 0)),
            pl.BlockSpec((1, hw), lambda i, j: (0, 0)),
        ],
        out_specs=[pl.BlockSpec((1, tn, hw), lambda i, j: (i, j, 0))] * 2,
        out_shape=[jax.ShapeDtypeStruct((b, nch, hw), jnp.float32)] * 2,
        compiler_params=pltpu.CompilerParams(
            dimension_semantics=("arbitrary", "arbitrary"), vmem_limit_bytes=V7X_VMEM_LIMIT),
        name="compress",
    )(kv_chunks, kv_chunks, pos, wf, ws, jnp.tile(k_norm0, NSA_KV_HEADS).reshape(1, -1))


PROMPT_SPLITS = 4
SEL_TILE = 1024
SEL_PER_TILE = SEL_TILE // SEL_BLOCK
WIN_KEYS = WINDOW + Q_BLOCK
WIN_BLOCKS = WIN_KEYS // Q_BLOCK
ROW_CHUNK = 64
NEG_BIG = -3.0e38


def _lane_tile(x, reps):
    return jnp.concatenate([x] * reps, axis=1)


def _nsa_prompt_kernel(q_ref, misc_ref, kc_ref, vc_ref, ksvs_ref, w0, w1, w2, w3, w4,
                       psel_ref, pselt_ref, ebig_ref, o_ref,
                       q4_ref, s_ref, p_ref, bias_ref, bq_all_ref, psum_ref, m_ref, l_ref, al_ref,
                       kw_ref, vw_ref, oc_ref, os_ref, ow_ref, sc_ref, *, n_sel_blocks, blk0):
    i = pl.program_id(0) + blk0
    s0 = i * Q_BLOCK
    f32, bf16 = jnp.float32, jnp.bfloat16
    hw = NSA_KV_HEADS * HEAD_DIM
    n_chunks = NSA_GROUP * Q_BLOCK // ROW_CHUNK
    halves = Q_BLOCK // ROW_CHUNK

    qb = q_ref[...]
    for h in range(NSA_KV_HEADS):
        for g in range(NSA_GROUP):
            piece = jnp.dot(qb, psel_ref[h * NSA_GROUP + g], preferred_element_type=f32)
            q4_ref[h, g * Q_BLOCK:(g + 1) * Q_BLOCK, :] = piece.astype(bf16)

    for j, w in enumerate((w0, w1, w2, w3, w4)):
        kw_ref[j * Q_BLOCK:(j + 1) * Q_BLOCK, :] = w[:, 0:hw]
        vw_ref[j * Q_BLOCK:(j + 1) * Q_BLOCK, :] = w[:, hw:2 * hw]

    def chunk_rows(c):
        return pl.ds(c * ROW_CHUNK, ROW_CHUNK)

    def chunk_t(c):
        r = lax.broadcasted_iota(jnp.int32, (ROW_CHUNK, 1), 0)
        return s0 + (c % halves) * ROW_CHUNK + r

    for h in range(NSA_KV_HEADS):
        q4 = q4_ref[h]

        ncp = kc_ref.shape[1]
        nsb = ncp // 4
        s_ref[:, 0:ncp] = lax.dot_general(q4, kc_ref[0], (((1,), (1,)), ((), ())), preferred_element_type=f32)
        psum_ref[...] = jnp.zeros_like(psum_ref)

        def cmp_chunk(c, carry):
            rows = chunk_rows(c)
            t = chunk_t(c)
            col = lax.broadcasted_iota(jnp.int32, (ROW_CHUNK, ncp), 1)
            cidx = (col % nsb) * 4 + col // nsb
            valid = cidx * CMP_STRIDE + (CMP_BLOCK - 1) <= t
            s = jnp.where(valid, s_ref[rows, 0:ncp], MASK_VALUE)
            mx = jnp.max(s, axis=1, keepdims=True)
            e = jnp.exp(s - mx)
            p = jnp.where(valid, e / jnp.sum(e, axis=1, keepdims=True), 0.0)
            p_ref[rows, 0:ncp] = p.astype(bf16)
            hrows = pl.ds((c % halves) * ROW_CHUNK, ROW_CHUNK)
            psum_ref[hrows, :] += p
            return carry

        for c in range(n_chunks):
            cmp_chunk(c, 0)
        oc_ref[h] = jnp.dot(p_ref[:, 0:ncp], vc_ref[0], preferred_element_type=f32)

        ps = psum_ref[...]
        a3 = ps[:, 3 * nsb:4 * nsb]
        blk = lax.broadcasted_iota(jnp.int32, (Q_BLOCK, nsb), 1)
        tq = s0 + lax.broadcasted_iota(jnp.int32, (Q_BLOCK, nsb), 0)
        imp = ps[:, 0:nsb] + ps[:, nsb:2 * nsb] + ps[:, 2 * nsb:3 * nsb] + a3
        imp = imp + jnp.where(blk == 0, 0.0, pltpu.roll(a3, 1, axis=1))
        visible = blk * SEL_BLOCK <= tq
        forced = (blk == 0) | (blk == tq // SEL_BLOCK)
        sc_ref[h * Q_BLOCK:(h + 1) * Q_BLOCK, :] = jnp.where(forced, FORCE_SCORE, jnp.where(visible, imp, -1.0))

    blkf = lax.broadcasted_iota(jnp.int32, sc_ref.shape, 1).astype(f32)

    def pick(_, carry):
        sc, selm = carry
        mx = jnp.max(sc, axis=1, keepdims=True)
        first = jnp.min(jnp.where(sc == mx, blkf, float(nsb)), axis=1, keepdims=True)
        hit = blkf == first
        return jnp.where(hit, NEG_BIG, sc), jnp.where(hit, 1.0, selm)

    score = sc_ref[...]
    _, selm = lax.fori_loop(0, min(SEL_TOPK, n_sel_blocks), pick, (score, jnp.zeros_like(score)))
    bq_all_ref[...] = jnp.where(selm > 0.0, 0.0, MASK_VALUE).astype(bf16)

    for h in range(NSA_KV_HEADS):
        q4 = q4_ref[h]
        bq_ref = bq_all_ref.at[h * Q_BLOCK:(h + 1) * Q_BLOCK]

        m_ref[...] = jnp.full_like(m_ref, NEG_BIG)
        l_ref[...] = jnp.zeros_like(l_ref)
        os_ref[h] = jnp.zeros((NSA_GROUP * Q_BLOCK, hw), f32)

        def sel_tile(kt, carry):
            k0 = pl.multiple_of(kt * SEL_TILE, SEL_TILE)
            e_off = pl.multiple_of(nsb - kt * SEL_PER_TILE, SEL_PER_TILE)
            key = k0 + lax.broadcasted_iota(jnp.int32, (Q_BLOCK, SEL_TILE), 1)
            tq1 = s0 + lax.broadcasted_iota(jnp.int32, (Q_BLOCK, 1), 0)
            blockmask = jnp.dot(bq_ref[...], ebig_ref[pl.ds(e_off, nsb), :], preferred_element_type=f32)
            bias_ref[...] = jnp.where(key <= tq1, blockmask, MASK_VALUE)
            s_ref[...] = lax.dot_general(q4, ksvs_ref[pl.ds(k0, SEL_TILE), 0:hw],
                                         (((1,), (1,)), ((), ())), preferred_element_type=f32)

            for c in range(n_chunks):
                rows = slice(c * ROW_CHUNK, (c + 1) * ROW_CHUNK)
                hrows = slice((c % halves) * ROW_CHUNK, (c % halves + 1) * ROW_CHUNK)
                s = s_ref[rows, :] + bias_ref[hrows, :]
                m_old = m_ref[rows, :]
                m_new = jnp.maximum(m_old, jnp.max(s, axis=1, keepdims=True))
                p = jnp.exp(s - _lane_tile(m_new, SEL_TILE // 128))
                alpha = jnp.exp(m_old - m_new)
                l_ref[rows, :] = alpha * l_ref[rows, :] + jnp.sum(p, axis=1, keepdims=True)
                m_ref[rows, :] = m_new
                al_ref[rows, :] = alpha
                p_ref[rows, :] = p.astype(bf16)
            pv = jnp.dot(p_ref[...], ksvs_ref[pl.ds(k0, SEL_TILE), hw:2 * hw], preferred_element_type=f32)
            os_ref[h] = os_ref[h] * al_ref[...] + pv
            return carry

        lax.fori_loop(0, (s0 + Q_BLOCK - 1) // SEL_TILE + 1, sel_tile, 0)
        os_ref[h] = os_ref[h] / l_ref[...]

        s_ref[:, 0:WIN_KEYS] = lax.dot_general(q4, kw_ref[...], (((1,), (1,)), ((), ())),
                                               preferred_element_type=f32)

        def win_chunk(c, carry):
            rows = chunk_rows(c)
            t = chunk_t(c)
            pos = s0 - WINDOW + lax.broadcasted_iota(jnp.int32, (ROW_CHUNK, WIN_KEYS), 1)
            d = t - pos
            valid = (d >= 0) & (d <= WINDOW) & (pos >= 0)
            s = jnp.where(valid, s_ref[rows, 0:WIN_KEYS], MASK_VALUE)
            mx = jnp.max(s, axis=1, keepdims=True)
            e = jnp.exp(s - mx)
            p_ref[rows, 0:WIN_KEYS] = (e / jnp.sum(e, axis=1, keepdims=True)).astype(bf16)
            return carry

        for c in range(n_chunks):
            win_chunk(c, 0)
        ow_ref[h] = jnp.dot(p_ref[:, 0:WIN_KEYS], vw_ref[...], preferred_element_type=f32)

    gsig = jax.nn.sigmoid(misc_ref[...])
    out = jnp.zeros((Q_BLOCK, Q_W), f32)
    for h in range(NSA_KV_HEADS):
        for g in range(NSA_GROUP):
            hg = h * NSA_GROUP + g
            r = slice(g * Q_BLOCK, (g + 1) * Q_BLOCK)
            mix = (gsig[:, 3 * hg:3 * hg + 1] * oc_ref[h, r, :]
                   + gsig[:, 3 * hg + 1:3 * hg + 2] * os_ref[h, r, :]
                   + gsig[:, 3 * hg + 2:3 * hg + 3] * ow_ref[h, r, :])
            out = out + jnp.dot(mix.astype(bf16), pselt_ref[hg], preferred_element_type=f32)
    o_ref[...] = out.astype(o_ref.dtype)


def _nsa_constants(n_sel_blocks):
    hw = NSA_KV_HEADS * HEAD_DIM
    psel = np.zeros((NSA_HEADS, Q_W, hw), np.float32)
    for h in range(NSA_KV_HEADS):
        for g in range(NSA_GROUP):
            hg = h * NSA_GROUP + g
            for d in range(HEAD_DIM):
                psel[hg, hg * HEAD_DIM + d, h * HEAD_DIM + d] = 1.0
    pselt = np.transpose(psel, (0, 2, 1))
    r = np.arange(2 * n_sel_blocks)[:, None] - n_sel_blocks
    ebig = (r == (np.arange(SEL_TILE)[None, :] // SEL_BLOCK)).astype(np.float32)
    return (jnp.asarray(psel, jnp.bfloat16), jnp.asarray(pselt, jnp.bfloat16), jnp.asarray(ebig, jnp.bfloat16))


def _nsa_prompt_call(q, misc, kcp, vcp, kvb, seq, b, blk0, nblk):
    assert seq % SEL_TILE == 0 and seq % Q_BLOCK == 0
    nb = seq // Q_BLOCK
    nsb = seq // SEL_BLOCK
    hw = NSA_KV_HEADS * HEAD_DIM
    psel, pselt, ebig = _nsa_constants(nsb)
    rows4 = NSA_GROUP * Q_BLOCK
    f32, bf16 = jnp.float32, jnp.bfloat16
    r0 = b * nb + blk0

    def win_spec(j):
        return pl.BlockSpec((Q_BLOCK, 2 * hw),
                            lambda i: (b * nb + jnp.maximum(blk0 + i - (WIN_BLOCKS - 1) + j, 0), 1))

    return pl.pallas_call(
        functools.partial(_nsa_prompt_kernel, n_sel_blocks=nsb, blk0=blk0),
        grid=(nblk,),
        in_specs=[
            pl.BlockSpec((Q_BLOCK, Q_W), lambda i: (r0 + i, 0)),
            pl.BlockSpec((Q_BLOCK, MISC_W), lambda i: (r0 + i, 0)),
            pl.BlockSpec((1, seq // CMP_STRIDE, hw), lambda i: (b, 0, 0)),
            pl.BlockSpec((1, seq // CMP_STRIDE, hw), lambda i: (b, 0, 0)),
            pl.BlockSpec((seq, 2 * hw), lambda i: (b, 0)),
        ] + [win_spec(j) for j in range(WIN_BLOCKS)] + [
            pl.BlockSpec(psel.shape, lambda i: (0, 0, 0)),
            pl.BlockSpec(pselt.shape, lambda i: (0, 0, 0)),
            pl.BlockSpec(ebig.shape, lambda i: (0, 0)),
        ],
        out_specs=pl.BlockSpec((Q_BLOCK, Q_W), lambda i: (i, 0)),
        out_shape=jax.ShapeDtypeStruct((nblk * Q_BLOCK, Q_W), bf16),
        scratch_shapes=[
            pltpu.VMEM((NSA_KV_HEADS, rows4, hw), bf16),
            pltpu.VMEM((rows4, SEL_TILE), f32),
            pltpu.VMEM((rows4, SEL_TILE), bf16),
            pltpu.VMEM((Q_BLOCK, SEL_TILE), f32),
            pltpu.VMEM((NSA_KV_HEADS * Q_BLOCK, nsb), bf16),
            pltpu.VMEM((Q_BLOCK, seq // CMP_STRIDE), f32),
            pltpu.VMEM((rows4, hw), f32),
            pltpu.VMEM((rows4, hw), f32),
            pltpu.VMEM((rows4, hw), f32),
            pltpu.VMEM((WIN_KEYS, hw), bf16),
            pltpu.VMEM((WIN_KEYS, hw), bf16),
            pltpu.VMEM((NSA_KV_HEADS, rows4, hw), f32),
            pltpu.VMEM((NSA_KV_HEADS, rows4, hw), f32),
            pltpu.VMEM((NSA_KV_HEADS, rows4, hw), f32),
            pltpu.VMEM((NSA_KV_HEADS * Q_BLOCK, nsb), f32),
        ],
        compiler_params=pltpu.CompilerParams(
            dimension_semantics=("arbitrary",), vmem_limit_bytes=V7X_VMEM_LIMIT),
        name="nsa_prompt",
    )(q, misc, kcp, vcp, kvb, kvb, kvb, kvb, kvb, kvb, psel, pselt, ebig)


def _prompt_compressed_kv(kvr2d, cmp_pos, w_cmp, k_norm_g, batch, seq):
    nch = seq // CMP_STRIDE
    wf, ws, pos = _compress_weights(w_cmp, cmp_pos)
    kc, vc = _compress_call(kvr2d.reshape(batch, nch, CMP_STRIDE * ROW_LANES), wf, ws, pos, k_norm_g[0],
                            tn=min(256, nch))

    def perm(a):
        return a.reshape(batch, nch // 4, 4, a.shape[-1]).transpose(0, 2, 1, 3).reshape(batch, nch, -1).astype(jnp.bfloat16)

    return perm(kc), perm(vc)


PAD_KEYS = 128


def _softmax_piece_max(pieces):
    m = None
    for s in pieces:
        pm = jnp.max(s, axis=1, keepdims=True)
        m = pm if m is None else jnp.maximum(m, pm)
    return m


def _nsa_sample_kernel(pt_ref, q_ref, misc_ref, kvb_ref, win_ref, cache_ref, wf_ref, ws_ref, pos_ref, kg_ref,
                       psel_ref, pselt_ref, ebig_ref, after_ref, o_ref, pages_ref, cmpt_ref, sem_ref, *, past_len, n_new):
    del after_ref
    f32, bf16 = jnp.float32, jnp.bfloat16
    b = pl.program_id(0)
    nseq = pl.num_programs(0)
    n_pages = past_len // PAGE_SIZE
    nsb = past_len // SEL_BLOCK
    ncp = past_len // CMP_STRIDE
    hw = NSA_KV_HEADS * HEAD_DIM
    slot = b % 2
    rows_q = NSA_GROUP * n_new

    blocks_per_page = PAGE_SIZE // SEL_BLOCK

    def sel_copy(seq, j, kind, s):
        return pltpu.make_async_copy(cache_ref.at[pt_ref[seq, j], :, pl.ds((2 + kind) * hw, hw)],
                                     pages_ref.at[s, kind, pl.ds(pl.multiple_of(j * PAGE_SIZE, PAGE_SIZE), PAGE_SIZE)],
                                     sem_ref.at[s])

    def cmp_copy(seq, j, kind, half, s):
        return pltpu.make_async_copy(cache_ref.at[pt_ref[seq, j], pl.ds(half * SEL_BLOCK, SEL_BLOCK), pl.ds(kind * hw, hw)],
                                     cmpt_ref.at[s, kind, :, j * blocks_per_page + half, :],
                                     sem_ref.at[s])

    def for_each_page_copy(seq, s, fn):
        def body(j, c):
            for kind in range(2):
                fn(sel_copy(seq, j, kind, s))
                for half in range(blocks_per_page):
                    fn(cmp_copy(seq, j, kind, half, s))
            return c
        lax.fori_loop(0, n_pages, body, 0)

    @pl.when(b == 0)
    def _():
        cmpt_ref[:, :, :, nsb:nsb + 8, :] = jnp.zeros((2, 2, SEL_BLOCK, 8, hw), f32)
        for_each_page_copy(0, 0, lambda cp: cp.start())

    @pl.when(b + 1 < nseq)
    def _():
        for_each_page_copy(b + 1, 1 - slot, lambda cp: cp.start())

    for_each_page_copy(b, slot, lambda cp: cp.wait())

    def strided(start):
        q, b0 = start % SEL_BLOCK, start // SEL_BLOCK
        return jnp.concatenate([cmpt_ref[slot, 0, q, b0:b0 + nsb, :], cmpt_ref[slot, 1, q, b0:b0 + nsb, :]],
                               axis=1).astype(bf16)

    first = None
    second = None
    for p in range(CMP_STRIDE):
        xf = jnp.concatenate([strided(CMP_STRIDE * r + p) for r in range(4)], axis=0)
        xs = jnp.concatenate([strided(CMP_STRIDE * (r + 1) + p) for r in range(4)], axis=0)
        df = jnp.dot(xf, wf_ref[p], preferred_element_type=f32)
        ds_ = jnp.dot(xs, ws_ref[p], preferred_element_type=f32)
        first = df if first is None else first + df
        second = ds_ if second is None else second + ds_
    bias = _chunk_map(pos_ref[0], wf_ref) + _chunk_map(pos_ref[1], ws_ref)
    cmp_out = first + second + bias[0:1, :]
    kc = cmp_out[:, 0:hw]
    kc = (kc * lax.rsqrt(_group_mean_sq(kc, HEAD_DIM) + RMS_EPS) * kg_ref[...]).astype(bf16)
    vc = cmp_out[:, hw:2 * hw].astype(bf16)

    qb = q_ref[...]
    newkv = kvb_ref[...]
    zpad = jnp.zeros((PAD_KEYS - n_new, hw), bf16)
    ks_new = jnp.concatenate([newkv[:, 0:hw], zpad], axis=0)
    vs_new = jnp.concatenate([newkv[:, hw:2 * hw], zpad], axis=0)
    kw_new = jnp.concatenate([newkv[:, 2 * hw:3 * hw], zpad], axis=0)
    vw_new = jnp.concatenate([newkv[:, 3 * hw:4 * hw], zpad], axis=0)
    wcache = win_ref[0]
    wbuf = wcache.shape[0]
    kw_old = wcache[:, 0:hw].astype(bf16)
    vw_old = wcache[:, hw:2 * hw].astype(bf16)

    tl = lax.broadcasted_iota(jnp.int32, (rows_q, 1), 0) % n_new
    t_abs = past_len + tl
    new_col = lax.broadcasted_iota(jnp.int32, (rows_q, PAD_KEYS), 1)
    new_ok = new_col <= tl
    nt_dims = (((1,), (1,)), ((), ()))
    gsig = jax.nn.sigmoid(misc_ref[...])
    out = jnp.zeros((n_new, Q_W), f32)

    for h in range(NSA_KV_HEADS):
        q4 = jnp.concatenate(
            [jnp.dot(qb, psel_ref[h * NSA_GROUP + g], preferred_element_type=f32).astype(bf16)
             for g in range(NSA_GROUP)], axis=0)

        s = lax.dot_general(q4, kc, nt_dims, preferred_element_type=f32)
        col = lax.broadcasted_iota(jnp.int32, (rows_q, ncp), 1)
        cidx = (col % nsb) * 4 + col // nsb
        valid = cidx * CMP_STRIDE + (CMP_BLOCK - 1) <= t_abs
        s = jnp.where(valid, s, MASK_VALUE)
        e = jnp.exp(s - jnp.max(s, axis=1, keepdims=True))
        pc = jnp.where(valid, e / jnp.sum(e, axis=1, keepdims=True), 0.0)
        o_c = jnp.dot(pc.astype(bf16), vc, preferred_element_type=f32)
        psum = pc[0:n_new]
        for g in range(1, NSA_GROUP):
            psum = psum + pc[g * n_new:(g + 1) * n_new]

        a3 = psum[:, 3 * nsb:4 * nsb]
        blk = lax.broadcasted_iota(jnp.int32, (n_new, nsb), 1)
        imp = psum[:, 0:nsb] + psum[:, nsb:2 * nsb] + psum[:, 2 * nsb:3 * nsb] + a3
        imp = imp + jnp.where(blk == 0, 0.0, pltpu.roll(a3, 1, axis=1))
        score = jnp.where(blk == 0, FORCE_SCORE, imp)
        blkf = blk.astype(f32)
        selm = jnp.zeros_like(score)
        for _ in range(SEL_TOPK - 1):
            mx = jnp.max(score, axis=1, keepdims=True)
            firstb = jnp.min(jnp.where(score == mx, blkf, float(nsb)), axis=1, keepdims=True)
            hit = blkf == firstb
            selm = jnp.where(hit, 1.0, selm)
            score = jnp.where(hit, NEG_BIG, score)
        bq = jnp.where(selm > 0.0, 0.0, MASK_VALUE).astype(bf16)

        s_new = jnp.where(new_ok, lax.dot_general(q4, ks_new, nt_dims, preferred_element_type=f32), MASK_VALUE)
        m_run = jnp.max(s_new, axis=1, keepdims=True)
        p_new = jnp.exp(s_new - m_run)
        l_run = jnp.sum(p_new, axis=1, keepdims=True)
        acc = jnp.dot(p_new.astype(bf16), vs_new, preferred_element_type=f32)

        def sel_tile(kt, carry):
            m_run, l_run, acc = carry
            k0 = pl.multiple_of(kt * SEL_TILE, SEL_TILE)
            e_off = pl.multiple_of(nsb - kt * SEL_PER_TILE, SEL_PER_TILE)
            bias = jnp.dot(bq, ebig_ref[pl.ds(e_off, nsb), :], preferred_element_type=f32)
            kt_rows = pages_ref[slot, 0, pl.ds(k0, SEL_TILE), :].astype(bf16)
            vt_rows = pages_ref[slot, 1, pl.ds(k0, SEL_TILE), :].astype(bf16)
            s = lax.dot_general(q4, kt_rows, nt_dims, preferred_element_type=f32)
            s = s + jnp.concatenate([bias] * NSA_GROUP, axis=0)
            m_new = jnp.maximum(m_run, jnp.max(s, axis=1, keepdims=True))
            p = jnp.exp(s - m_new)
            alpha = jnp.exp(m_run - m_new)
            l_new = alpha * l_run + jnp.sum(p, axis=1, keepdims=True)
            acc = acc * alpha + jnp.dot(p.astype(bf16), vt_rows, preferred_element_type=f32)
            return m_new, l_new, acc

        m_run, l_run, acc = lax.fori_loop(0, past_len // SEL_TILE, sel_tile, (m_run, l_run, acc))
        o_s = acc / l_run

        wpos = past_len - wbuf + lax.broadcasted_iota(jnp.int32, (rows_q, wbuf), 1)
        d = t_abs - wpos
        ok_old = (d >= 0) & (d <= WINDOW) & (wpos >= 0)
        s_old = jnp.where(ok_old, lax.dot_general(q4, kw_old, nt_dims, preferred_element_type=f32), MASK_VALUE)
        s_nw = jnp.where(new_ok, lax.dot_general(q4, kw_new, nt_dims, preferred_element_type=f32), MASK_VALUE)
        mw = _softmax_piece_max([s_old, s_nw])
        e_old = jnp.exp(s_old - mw)
        e_nw = jnp.exp(s_nw - mw)
        lw = jnp.sum(e_old, axis=1, keepdims=True) + jnp.sum(e_nw, axis=1, keepdims=True)
        o_w = (jnp.dot((e_old / lw).astype(bf16), vw_old, preferred_element_type=f32)
               + jnp.dot((e_nw / lw).astype(bf16), vw_new, preferred_element_type=f32))

        for g in range(NSA_GROUP):
            hg = h * NSA_GROUP + g
            r = slice(g * n_new, (g + 1) * n_new)
            mix = (gsig[:, 3 * hg:3 * hg + 1] * o_c[r] + gsig[:, 3 * hg + 1:3 * hg + 2] * o_s[r]
                   + gsig[:, 3 * hg + 2:3 * hg + 3] * o_w[r])
            out = out + jnp.dot(mix.astype(bf16), pselt_ref[hg], preferred_element_type=f32)
    o_ref[...] = out.astype(o_ref.dtype)


def _nsa_sample_call(q, misc, kvb, cache_kv_l, cache_win_l, page_table, cmp_pos, w_cmp, k_norm_g, n_new, after):
    bsz, n_pages = page_table.shape
    past_len = n_pages * PAGE_SIZE
    assert past_len % SEL_TILE == 0 and n_new % 8 == 0 and n_new <= PAD_KEYS
    assert (past_len + n_new - CMP_BLOCK) // CMP_STRIDE + 1 == past_len // CMP_STRIDE - 1
    nsb = past_len // SEL_BLOCK
    hw = NSA_KV_HEADS * HEAD_DIM
    wbuf = cache_win_l.shape[1]
    psel, pselt, ebig = _nsa_constants(nsb)
    wf, ws, pos = _compress_weights(w_cmp, cmp_pos)
    cache = cache_kv_l.reshape(cache_kv_l.shape[0], PAGE_SIZE, ROW_LANES)
    win = cache_win_l.reshape(bsz, wbuf, 2 * hw)
    row = lambda i, pt: (i, 0)
    c2 = lambda i, pt: (0, 0)
    c3 = lambda i, pt: (0, 0, 0)
    grid_spec = pltpu.PrefetchScalarGridSpec(
        num_scalar_prefetch=1,
        grid=(bsz,),
        in_specs=[
            pl.BlockSpec((n_new, Q_W), row),
            pl.BlockSpec((n_new, MISC_W), row),
            pl.BlockSpec((n_new, 4 * hw), row),
            pl.BlockSpec((1, wbuf, 2 * hw), lambda i, pt: (i, 0, 0)),
            pl.BlockSpec(memory_space=pl.ANY),
            pl.BlockSpec(wf.shape, c3), pl.BlockSpec(ws.shape, c3), pl.BlockSpec(pos.shape, c3),
            pl.BlockSpec((1, hw), c2),
            pl.BlockSpec(psel.shape, c3), pl.BlockSpec(pselt.shape, c3), pl.BlockSpec(ebig.shape, c2),
            pl.BlockSpec(after.shape, c2),
        ],
        out_specs=pl.BlockSpec((n_new, Q_W), row),
        scratch_shapes=[pltpu.VMEM((2, 2, past_len, hw), jnp.float32),
                        pltpu.VMEM((2, 2, SEL_BLOCK, nsb + 8, hw), jnp.float32),
                        pltpu.SemaphoreType.DMA((2,))],
    )
    return pl.pallas_call(
        functools.partial(_nsa_sample_kernel, past_len=past_len, n_new=n_new),
        grid_spec=grid_spec,
        out_shape=jax.ShapeDtypeStruct((bsz * n_new, Q_W), jnp.bfloat16),
        compiler_params=pltpu.CompilerParams(
            dimension_semantics=("arbitrary",), vmem_limit_bytes=V7X_VMEM_LIMIT),
        name="nsa_sample",
    )(page_table, q, misc, kvb, win, cache, wf, ws, pos,
      jnp.tile(k_norm_g[0], NSA_KV_HEADS).reshape(1, -1), psel, pselt, ebig, after)


GLA_J_GROUP = 8


def _split3(x):
    hi = x.astype(jnp.bfloat16)
    r = x - hi.astype(jnp.float32)
    mid = r.astype(jnp.bfloat16)
    lo = (r - mid.astype(jnp.float32)).astype(jnp.bfloat16)
    return hi, mid, lo


def _gla_kernel(gq_ref, gk_ref, gv_ref, gog_ref, misc_ref, s0_ref, wg_ref, bg_ref, ng_ref,
                o_ref, sout_ref, sbd_ref, la_ref, cum_ref, *, chunk):
    f32, bf16 = jnp.float32, jnp.bfloat16
    tstep = pl.program_id(1)
    n_tsteps = pl.num_programs(1)
    tb = gq_ref.shape[0]
    c = chunk
    mm = bf16 if c % 16 == 0 else f32
    hk, hv = GQ_W, GV_W

    @pl.when(tstep == 0)
    def _():
        sbd_ref[...] = jnp.zeros_like(sbd_ref)
        for h in range(GLA_HEADS):
            sbd_ref[h * GLA_DK:(h + 1) * GLA_DK, h * GLA_DV:(h + 1) * GLA_DV] = s0_ref[0, h]

    z = jnp.dot(misc_ref[...].astype(bf16), wg_ref[...], preferred_element_type=f32) + bg_ref[...]
    la_ref[...] = (jnp.minimum(z, 0.0) - jnp.log1p(jnp.exp(-jnp.abs(z)))) * (1.0 / GLA_GATE_TEMP)

    ri = lax.broadcasted_iota(jnp.int32, (c, c), 0)
    ci = lax.broadcasted_iota(jnp.int32, (c, c), 1)
    tril = jnp.where(ri >= ci, 1.0, 0.0).astype(bf16)
    kr = lax.broadcasted_iota(jnp.int32, (hk, hk), 0) // GLA_DK
    kc = lax.broadcasted_iota(jnp.int32, (hk, hk), 1) // GLA_DK
    head_rep = jnp.where(kr == kc, 1.0, 0.0).astype(bf16)
    eye_k = (lax.broadcasted_iota(jnp.int32, (hk, hk), 0) == lax.broadcasted_iota(jnp.int32, (hk, hk), 1))
    bd_mask = (lax.broadcasted_iota(jnp.int32, (hk, hv), 0) // GLA_DK
               == lax.broadcasted_iota(jnp.int32, (hk, hv), 1) // GLA_DV)
    lane_j = lax.broadcasted_iota(jnp.int32, (c, hk), 1) % GLA_DK
    row_i = lax.broadcasted_iota(jnp.int32, (c, hk), 0)

    def one_chunk(ch, carry):
        rows = pl.ds(pl.multiple_of(ch * c, c), c)
        q = gq_ref[rows, :] * (GLA_DK ** -0.5)
        k = gk_ref[rows, :]
        v = gv_ref[rows, :]
        la = la_ref[rows, :]
        hi, mid, lo = _split3(la)
        cum = (jnp.dot(tril, hi, preferred_element_type=f32) + jnp.dot(tril, mid, preferred_element_type=f32)
               + jnp.dot(tril, lo, preferred_element_type=f32))
        last = cum[c - 1:c, :]
        cum_ref[...] = cum

        def j_group(g, att):
            ws = []
            for jj in range(GLA_J_GROUP):
                j = g * GLA_J_GROUP + jj
                jrow = pl.ds(ch * c + j, 1)
                kj = gk_ref[jrow, :]
                cumj = cum_ref[pl.ds(j, 1), :]
                dec = jnp.where(row_i >= j, jnp.exp(jnp.minimum(cum - cumj, 0.0)), 0.0)
                ws.append((q * kj * dec).astype(bf16))
            r = jnp.dot(jnp.concatenate(ws, axis=0), head_rep, preferred_element_type=f32)
            for jj in range(GLA_J_GROUP):
                j = g * GLA_J_GROUP + jj
                att = att + jnp.where(lane_j == j, r[jj * c:(jj + 1) * c, :], 0.0)
            return att

        att = lax.fori_loop(0, c // GLA_J_GROUP, j_group, jnp.zeros((c, hk), f32))

        vt = jnp.concatenate([v] * (GLA_DK // c), axis=0) if c < GLA_DK else v
        vbd = jnp.where(bd_mask, jnp.concatenate([vt] * GLA_HEADS, axis=0), 0.0)
        sbd = sbd_ref[...]
        o = jnp.dot(att.astype(bf16), vbd.astype(bf16), preferred_element_type=f32)
        o = o + jnp.dot((q * jnp.exp(cum)).astype(bf16), sbd.astype(bf16), preferred_element_type=f32)

        ke = k * jnp.exp(last - cum)
        upd = lax.dot_general(ke.astype(mm), v.astype(mm), (((0,), (0,)), ((), ())), preferred_element_type=f32)
        dcol = jnp.sum(jnp.where(eye_k, jnp.exp(last), 0.0), axis=1, keepdims=True)
        sbd_ref[...] = sbd * dcol + jnp.where(bd_mask, upd, 0.0)

        gog = gog_ref[rows, :]
        for h in range(GLA_HEADS):
            sl = slice(h * GLA_DV, (h + 1) * GLA_DV)
            oh = o[:, sl]
            oh = oh * lax.rsqrt(jnp.mean(oh * oh, axis=1, keepdims=True) + RMS_EPS) * ng_ref[...]
            gh = gog[:, sl]
            o_ref[rows, sl] = (oh * gh * jax.nn.sigmoid(gh)).astype(o_ref.dtype)
        return carry

    lax.fori_loop(0, tb // c, one_chunk, 0)

    @pl.when(tstep == n_tsteps - 1)
    def _():
        for h in range(GLA_HEADS):
            sout_ref[0, h] = sbd_ref[h * GLA_DK:(h + 1) * GLA_DK, h * GLA_DV:(h + 1) * GLA_DV]


def _gla_call(gq, gk, gv, gog, misc, s0, w_gate, b_gate, norm_g, batch, seq):
    c = math.gcd(seq, GLA_CHUNK)
    tb = _token_tile(seq, (512, 256, 128, 64, 32, 16, 8))
    tb = max(tb, c)
    nt = seq // tb
    f32, bf16 = jnp.float32, jnp.bfloat16
    wg = jnp.zeros((MISC_W, GQ_W), f32).at[GATE_W:GATE_W + GLA_GATE_RANK].set(w_gate).astype(bf16)
    row = lambda b, t: (b * nt + t, 0)
    const = lambda b, t: (0, 0)
    state_spec = pl.BlockSpec((1, GLA_HEADS, GLA_DK, GLA_DV), lambda b, t: (b, 0, 0, 0))
    return pl.pallas_call(
        functools.partial(_gla_kernel, chunk=c),
        grid=(batch, nt),
        in_specs=[
            pl.BlockSpec((tb, GQ_W), row), pl.BlockSpec((tb, GQ_W), row),
            pl.BlockSpec((tb, GV_W), row), pl.BlockSpec((tb, GV_W), row),
            pl.BlockSpec((tb, MISC_W), row), state_spec,
            pl.BlockSpec((MISC_W, GQ_W), const), pl.BlockSpec((1, GQ_W), const), pl.BlockSpec((1, GLA_DV), const),
        ],
        out_specs=[pl.BlockSpec((tb, GV_W), row), state_spec],
        out_shape=[jax.ShapeDtypeStruct((batch * seq, GV_W), bf16),
                   jax.ShapeDtypeStruct((batch, GLA_HEADS, GLA_DK, GLA_DV), f32)],
        scratch_shapes=[pltpu.VMEM((GQ_W, GV_W), f32), pltpu.VMEM((tb, GQ_W), f32), pltpu.VMEM((c, GQ_W), f32)],
        compiler_params=pltpu.CompilerParams(
            dimension_semantics=("arbitrary", "arbitrary"), vmem_limit_bytes=V7X_VMEM_LIMIT),
        name="gla",
    )(gq, gk, gv, gog, misc, s0, wg, b_gate.reshape(1, -1), norm_g.reshape(1, -1))


PEER_GROUPS = 2 * PEER_HEADS
PEER_HALF = PEER_KEY_DIM // 2
PEER_SLOTS = PEER_HEADS * PEER_TOPK
PEER_WORDS = D_MODEL // 2


def _tail_kernel(x_ref, a_ref, g_ref, wo_ref, n2_ref, wq_ref, sk_ref, x1_ref, h2_ref, st_ref):
    f32, bf16 = jnp.float32, jnp.bfloat16
    half = wo_ref.shape[0] // 2
    mix = jnp.dot(a_ref[...], wo_ref[0:half, :], preferred_element_type=f32)
    mix = mix + jnp.dot(g_ref[...], wo_ref[half:2 * half, :], preferred_element_type=f32)
    x1 = x_ref[...] + mix
    x1_ref[...] = x1
    h2 = x1 * lax.rsqrt(jnp.mean(x1 * x1, axis=-1, keepdims=True) + RMS_EPS) * n2_ref[...]
    h2_ref[...] = h2
    qh = jnp.dot(h2.astype(bf16), wq_ref[...], preferred_element_type=f32).astype(bf16)
    for c in range(PEER_GROUPS):
        st_ref[c] = lax.dot_general(sk_ref[c], qh[:, c * PEER_HALF:(c + 1) * PEER_HALF],
                                    (((1,), (1,)), ((), ())), preferred_element_type=f32)


def _tail_call(x2d, a, g, w_out, norm2_g, w_query, subkeys, *, tm, row0, n):
    assert n % tm == 0 and row0 % tm == 0
    f32, bf16 = jnp.float32, jnp.bfloat16
    row = lambda i: (i, 0)
    off = lambda i: (row0 // tm + i, 0)
    const = lambda i: (0, 0)
    sk = subkeys.reshape(PEER_GROUPS, PEER_NKEYS, PEER_HALF).astype(bf16)
    return pl.pallas_call(
        _tail_kernel,
        grid=(n // tm,),
        in_specs=[
            pl.BlockSpec((tm, D_MODEL), off),
            pl.BlockSpec((tm, Q_W), row),
            pl.BlockSpec((tm, GV_W), off),
            pl.BlockSpec((Q_W + GV_W, D_MODEL), const),
            pl.BlockSpec((1, D_MODEL), const),
            pl.BlockSpec((D_MODEL, PEER_HEADS * PEER_KEY_DIM), const),
            pl.BlockSpec((PEER_GROUPS, PEER_NKEYS, PEER_HALF), lambda i: (0, 0, 0)),
        ],
        out_specs=[pl.BlockSpec((tm, D_MODEL), row), pl.BlockSpec((tm, D_MODEL), row),
                   pl.BlockSpec((PEER_GROUPS, PEER_NKEYS, tm), lambda i: (0, 0, i))],
        out_shape=[jax.ShapeDtypeStruct((n, D_MODEL), f32), jax.ShapeDtypeStruct((n, D_MODEL), f32),
                   jax.ShapeDtypeStruct((PEER_GROUPS, PEER_NKEYS, n), f32)],
        compiler_params=pltpu.CompilerParams(
            dimension_semantics=("arbitrary",), vmem_limit_bytes=V7X_VMEM_LIMIT),
        name="tail_proj",
    )(x2d, a, g, w_out.astype(bf16), norm2_g.reshape(1, -1), w_query.astype(bf16), sk)


def _extract_topk(x, ids, k):
    r = x.shape[0]
    rows = lax.broadcasted_iota(jnp.int32, x.shape, 0).astype(jnp.float32)
    vals, picked = [], []
    for _ in range(k):
        mx = jnp.max(x, axis=0, keepdims=True)
        first = jnp.min(jnp.where(x == mx, rows, float(r)), axis=0, keepdims=True)
        hit = rows == first
        vals.append(mx)
        picked.append(first if ids is None else jnp.sum(jnp.where(hit, ids, 0.0), axis=0, keepdims=True))
        x = jnp.where(hit, NEG_BIG, x)
    return vals, picked


def _grid_candidates(v1, i1, v2, i2):
    s2 = jnp.concatenate(v2, axis=0)
    j2 = jnp.concatenate(i2, axis=0)
    cand, cidx = [], []
    for a in range(PEER_TOPK // 2):
        nb = PEER_TOPK if a == 0 else PEER_TOPK // 2
        cand.append(v1[a] + s2[0:nb])
        cidx.append(i1[a] * float(PEER_NKEYS) + j2[0:nb])
    tail = range(PEER_TOPK // 2, PEER_TOPK)
    cand.append(jnp.concatenate([v1[a] for a in tail], axis=0) + v2[0])
    cidx.append(jnp.concatenate([i1[a] for a in tail], axis=0) * float(PEER_NKEYS) + i2[0])
    return jnp.concatenate(cand, axis=0), jnp.concatenate(cidx, axis=0)


def _peer_topk_kernel(st_ref, e_ref, g_ref, ids_ref):
    f32 = jnp.float32

    def head(h, carry):
        v1, i1 = _extract_topk(st_ref[2 * h], None, PEER_TOPK)
        v2, i2 = _extract_topk(st_ref[2 * h + 1], None, PEER_TOPK)
        cand, cidx = _grid_candidates(v1, i1, v2, i2)
        top, eid = _extract_topk(cand, cidx, PEER_TOPK)
        top = jnp.concatenate(top, axis=0)
        e = jnp.exp(top - top[0:1, :])
        rows = pl.ds(pl.multiple_of(h * PEER_TOPK, PEER_TOPK), PEER_TOPK)
        g_ref[rows, :] = e / jnp.sum(e, axis=0, keepdims=True)
        ids_ref[rows, :] = jnp.concatenate(eid, axis=0)
        return carry

    lax.fori_loop(0, PEER_HEADS, head, 0)
    e_ref[...] = ids_ref[...].T.astype(jnp.int32)


def _peer_topk_call(st, *, tt):
    n = st.shape[2]
    assert n % tt == 0
    return pl.pallas_call(
        _peer_topk_kernel,
        grid=(n // tt,),
        in_specs=[pl.BlockSpec((PEER_GROUPS, PEER_NKEYS, tt), lambda i: (0, 0, i))],
        out_specs=[pl.BlockSpec((tt, PEER_SLOTS), lambda i: (i, 0)), pl.BlockSpec((PEER_SLOTS, tt), lambda i: (0, i))],
        out_shape=[jax.ShapeDtypeStruct((n, PEER_SLOTS), jnp.int32),
                   jax.ShapeDtypeStruct((PEER_SLOTS, n), jnp.float32)],
        scratch_shapes=[pltpu.VMEM((PEER_SLOTS, tt), jnp.float32)],
        compiler_params=pltpu.CompilerParams(
            dimension_semantics=("arbitrary",), vmem_limit_bytes=V7X_VMEM_LIMIT),
        name="peer_topk",
    )(st)


SC_CORES = 2
SC_SUBCORES = 16
SC_WORKERS = SC_CORES * SC_SUBCORES
SC_CHUNK = 64
SC_IDX_BLOCK = 2048
SC_CHUNKS_PER_BLOCK = SC_IDX_BLOCK // SC_CHUNK


def _sc_gather_rows(table, idx):
    m = idx.shape[0]
    width = table.shape[1]
    assert m % (SC_WORKERS * SC_IDX_BLOCK) == 0 and SC_CHUNKS_PER_BLOCK % 2 == 0
    chunks_per_worker = m // SC_WORKERS // SC_CHUNK
    cpb = SC_CHUNKS_PER_BLOCK
    mesh = plsc.VectorSubcoreMesh(core_axis_name="c", subcore_axis_name="s",
                                  num_cores=SC_CORES, num_subcores=SC_SUBCORES)

    @functools.partial(
        pl.kernel, mesh=mesh,
        out_type=jax.ShapeDtypeStruct((m, width), table.dtype),
        scratch_types=[pltpu.VMEM((cpb, SC_CHUNK), jnp.int32),
                       pltpu.VMEM((SC_CHUNK, width), table.dtype),
                       pltpu.VMEM((SC_CHUNK, width), table.dtype),
                       pltpu.SemaphoreType.DMA, pltpu.SemaphoreType.DMA,
                       pltpu.SemaphoreType.DMA, pltpu.SemaphoreType.DMA],
        name="peer_gather",
    )
    def gather_kernel(table_hbm, idx_hbm, out_hbm, idx_v, buf0, buf1, gsem0, gsem1, wsem0, wsem1):
        wid = lax.axis_index("s") * SC_CORES + lax.axis_index("c")
        base_chunk = wid * chunks_per_worker

        def gather(j, buf, sem):
            return pltpu.make_async_copy(table_hbm.at[idx_v.at[j]], buf, sem)

        def write(chunk, buf, sem):
            rows = pl.ds(pl.multiple_of(chunk * SC_CHUNK, SC_CHUNK), SC_CHUNK)
            return pltpu.make_async_copy(buf, out_hbm.at[rows], sem)

        @pl.loop(0, chunks_per_worker // cpb)
        def _(blk):
            c0 = base_chunk + blk * cpb
            pltpu.sync_copy(idx_hbm.at[pl.ds(pl.multiple_of(c0, cpb), cpb)], idx_v)
            gather(0, buf0, gsem0).start()

            @pl.loop(0, cpb // 2)
            def _(p):
                j = p * 2
                gather(j, buf0, gsem0).wait()
                write(c0 + j, buf0, wsem0).start()

                @pl.when(p > 0)
                def _():
                    write(c0 + j - 1, buf1, wsem1).wait()

                gather(j + 1, buf1, gsem1).start()
                gather(j + 1, buf1, gsem1).wait()
                write(c0 + j + 1, buf1, wsem1).start()
                write(c0 + j, buf0, wsem0).wait()

                @pl.when(p < cpb // 2 - 1)
                def _():
                    gather(j + 2, buf0, gsem0).start()

            write(c0 + cpb - 1, buf1, wsem1).wait()

    return gather_kernel(table, idx.reshape(m // SC_CHUNK, SC_CHUNK))


def _pack_rows(w):
    b = lax.bitcast_convert_type(w.astype(jnp.bfloat16), jnp.uint16).astype(jnp.uint32)
    words = (b[:, :PEER_WORDS] << 16) | b[:, PEER_WORDS:]
    return lax.bitcast_convert_type(words, jnp.int32)


def _unpack_rows(words):
    hi = pltpu.bitcast(words & jnp.int32(-65536), jnp.float32)
    lo = pltpu.bitcast(words << 16, jnp.float32)
    return hi, lo


PEER_TOKEN_UNROLL = 4


def _peer_combine_kernel(gu_ref, gv_ref, h2_ref, gt_ref, x1_ref, y_ref):
    f32 = jnp.float32
    tb = h2_ref.shape[0]
    gate_tokens = gt_ref.shape[1]
    lane = lax.broadcasted_iota(jnp.int32, (PEER_SLOTS, gate_tokens), 1)
    lane0 = (pl.program_id(0) % (gate_tokens // tb)) * tb

    def dots(n, dmat):
        rows = pl.ds(pl.multiple_of(n * PEER_SLOTS, PEER_SLOTS), PEER_SLOTS)
        x = h2_ref[pl.ds(n, 1), :]
        uh, ul = _unpack_rows(gu_ref[rows, :])
        d = jnp.sum(uh * x[:, 0:PEER_WORDS] + ul * x[:, PEER_WORDS:], axis=1, keepdims=True)
        return jnp.where(lane == lane0 + n, d, dmat)

    dmat = lax.fori_loop(0, tb, dots, jnp.zeros((PEER_SLOTS, gate_tokens), f32), unroll=PEER_TOKEN_UNROLL)
    wmat = gt_ref[...] * jax.nn.gelu(dmat)

    def combine(n, carry):
        rows = pl.ds(pl.multiple_of(n * PEER_SLOTS, PEER_SLOTS), PEER_SLOTS)
        w = jnp.sum(jnp.where(lane == lane0 + n, wmat, 0.0), axis=1, keepdims=True)
        vh, vl = _unpack_rows(gv_ref[rows, :])
        x1 = x1_ref[pl.ds(n, 1), :]
        y_ref[pl.ds(n, 1), 0:PEER_WORDS] = x1[:, 0:PEER_WORDS] + jnp.sum(w * vh, axis=0, keepdims=True)
        y_ref[pl.ds(n, 1), PEER_WORDS:] = x1[:, PEER_WORDS:] + jnp.sum(w * vl, axis=0, keepdims=True)
        return carry

    lax.fori_loop(0, tb, combine, 0, unroll=PEER_TOKEN_UNROLL)


def _peer_combine_call(gu, gv, h2, gates_t, x1, *, tb):
    n = h2.shape[0]
    gate_tokens = 128
    assert n % gate_tokens == 0 and gate_tokens % tb == 0
    row = lambda i: (i, 0)
    return pl.pallas_call(
        _peer_combine_kernel,
        grid=(n // tb,),
        in_specs=[
            pl.BlockSpec((tb * PEER_SLOTS, PEER_WORDS), row),
            pl.BlockSpec((tb * PEER_SLOTS, PEER_WORDS), row),
            pl.BlockSpec((tb, D_MODEL), row),
            pl.BlockSpec((PEER_SLOTS, gate_tokens), lambda i: (0, i // (gate_tokens // tb))),
            pl.BlockSpec((tb, D_MODEL), row),
        ],
        out_specs=pl.BlockSpec((tb, D_MODEL), row),
        out_shape=jax.ShapeDtypeStruct((n, D_MODEL), jnp.float32),
        compiler_params=pltpu.CompilerParams(
            dimension_semantics=("arbitrary",), vmem_limit_bytes=V7X_VMEM_LIMIT),
        name="peer_combine",
    )(gu, gv, h2, gates_t, x1)


def _tail_pallas(x2d, a, g, w_out, norm2_g, w_query, subkeys, u_words, v_words, row0, n):
    tm = _token_tile(n, (256, 128))
    x1, h2, st = _tail_call(x2d, a, g, w_out, norm2_g, w_query, subkeys, tm=tm, row0=row0, n=n)
    eidx, gates_t = _peer_topk_call(st, tt=tm)
    flat = eidx.reshape(n * PEER_SLOTS)
    gu = _sc_gather_rows(u_words, flat)
    gv = _sc_gather_rows(v_words, flat)
    return _peer_combine_call(gu, gv, h2, gates_t, x1, tb=16), eidx[0:8]


def _token_tile(n, candidates=(512, 256, 128, 64, 32, 16, 8)):
    for tm in candidates:
        if n % tm == 0:
            return tm
    raise ValueError(n)


def kernel(x_prompt, x_sample, cache_kv, cache_win, state_gla, page_table, norm1_g, w_in, q_norm_g, k_norm_g, cmp_pos, w_cmp, gla_w_gate, gla_b_gate, gla_norm_g, w_out, norm2_g, peer_w_query, peer_subkeys, peer_u, peer_v):
    depth = w_in.shape[0]
    xp, xs = x_prompt, x_sample
    kv_p, win_p, gla_p, kv_s, win_s, gla_s = [], [], [], [], [], []
    for l in range(depth):
        w_in_r = _reorder_w_in(w_in[l])
        u_words, v_words = _pack_rows(peer_u[l]), _pack_rows(peer_v[l])
        kv_shape = (KV_ROWS, NSA_KV_HEADS, HEAD_DIM)
        win_shape = (2, NSA_KV_HEADS, HEAD_DIM)

        bp, tp = xp.shape[0], xp.shape[1]
        x2d = xp.reshape(bp * tp, D_MODEL)
        q2d, kvr, winr, misc, gq, gk, gv, gog, kvb = _inproj(
            x2d, norm1_g[l], w_in_r, q_norm_g[l], k_norm_g[l], tm=_token_tile(bp * tp))
        kcp, vcp = _prompt_compressed_kv(kvr, cmp_pos[l], w_cmp[l], k_norm_g[l], bp, tp)
        s0 = jnp.zeros((bp, GLA_HEADS, GLA_DK, GLA_DV), jnp.float32)
        g, s_new = _gla_call(gq, gk, gv, gog, misc, s0, gla_w_gate[l], gla_b_gate[l], gla_norm_g[l], bp, tp)
        kv_p.append(kvr.reshape((bp, tp) + kv_shape))
        win_p.append(winr.reshape((bp, tp) + win_shape)[:, -min(WINDOW, tp):])
        gla_p.append(s_new.astype(state_gla.dtype))
        nb = tp // Q_BLOCK
        nblk = _token_tile(nb, (nb // PROMPT_SPLITS, nb))
        ys = {}
        first_ids = None
        for blk0 in reversed(range(0, nb, nblk)):
            for b in reversed(range(bp)):
                a = _nsa_prompt_call(q2d, misc, kcp, vcp, kvb, tp, b, blk0, nblk)
                ys[(b, blk0)], ids = _tail_pallas(x2d, a, g, w_out[l], norm2_g[l], peer_w_query[l], peer_subkeys[l],
                                                  u_words, v_words, (b * nb + blk0) * Q_BLOCK, nblk * Q_BLOCK)
                first_ids = ids if first_ids is None else first_ids
        xp = jnp.concatenate([ys[k] for k in sorted(ys)], axis=0).reshape(bp, tp, D_MODEL)

        bs, ts = xs.shape[0], xs.shape[1]
        ns = bs * ts
        xs2d = xs.reshape(ns, D_MODEL)
        q2d, kvr, winr, misc, gq, gk, gv, gog, kvb = _inproj(
            xs2d, norm1_g[l], w_in_r, q_norm_g[l], k_norm_g[l], tm=_token_tile(ns))
        a = _nsa_sample_call(q2d, misc, kvb, cache_kv[l], cache_win[l], page_table,
                             cmp_pos[l], w_cmp[l], k_norm_g[l], ts, first_ids)
        g, s_new = _gla_call(gq, gk, gv, gog, misc, state_gla[l].astype(jnp.float32), gla_w_gate[l],
                             gla_b_gate[l], gla_norm_g[l], bs, ts)
        win_all = jnp.concatenate([cache_win[l], winr.reshape((bs, ts) + win_shape)], axis=1)
        kv_s.append(kvr.reshape((bs, ts) + kv_shape))
        win_s.append(win_all[:, -min(WINDOW, page_table.shape[1] * PAGE_SIZE + ts):])
        gla_s.append(s_new.astype(state_gla.dtype))
        xs = _tail_pallas(xs2d, a, g, w_out[l], norm2_g[l], peer_w_query[l], peer_subkeys[l],
                          u_words, v_words, 0, ns)[0].reshape(xs.shape)
    return (xp, xs, jnp.stack(kv_p), jnp.stack(win_p), jnp.stack(gla_p),
            jnp.stack(kv_s), jnp.stack(win_s), jnp.stack(gla_s))
```
